```python
import math
import jax, jax.numpy as jnp
from jax import lax
import numpy as np

D_MODEL = 2048
BATCH = 4
SEQ = 8192
DEPTH = 1

PLE_DIM = 256
MIX_WIDTH = D_MODEL
ATTN_WIDTH = MIX_WIDTH // 2
HEAD_DIM = 64
N_HEADS = ATTN_WIDTH // HEAD_DIM
SSM_WIDTH = MIX_WIDTH - ATTN_WIDTH
SSM_GROUP = 16
N_SSM_GROUPS = SSM_WIDTH // SSM_GROUP
SSM_STATE = 64
D_FF = ((8 * D_MODEL // 3 + 127) // 128) * 128
DILATIONS = ((128, 1), (512, 4), (2048, 16))
SWA_BLOCK = 128
NORM_EPS = 1e-6
DT_MIN = 1e-3
DT_MAX = 1e-1
MASK_VALUE = -1e30

kernel_name = 'hymba_dilated_s5_macaron'


def rms_norm(x, g):
    xf = x.astype(jnp.float32)
    y = xf * lax.rsqrt(jnp.mean(xf * xf, axis=-1, keepdims=True) + NORM_EPS)
    return (y * g.astype(jnp.float32)).astype(x.dtype)


def swiglu(x, w_gate, w_up, w_down):
    return (jax.nn.silu(x @ w_gate) * (x @ w_up)) @ w_down


def banded_attention_stats(q, k, v, span):
    N, H, L, E = q.shape
    Q = SWA_BLOCK
    nb = -(-L // Q)
    Lp = nb * Q
    qf, kf, vf = (t.astype(jnp.float32) for t in (q, k, v))
    qb = jnp.pad(qf, ((0, 0), (0, 0), (0, Lp - L), (0, 0))).reshape(N, H, nb, Q, E)
    kp = jnp.pad(kf, ((0, 0), (0, 0), (Q, Lp - L), (0, 0)))
    vp = jnp.pad(vf, ((0, 0), (0, 0), (Q, Lp - L), (0, 0)))
    kb = jnp.concatenate([kp[:, :, :Lp].reshape(N, H, nb, Q, E), kp[:, :, Q:].reshape(N, H, nb, Q, E)], axis=3)
    vb = jnp.concatenate([vp[:, :, :Lp].reshape(N, H, nb, Q, E), vp[:, :, Q:].reshape(N, H, nb, Q, E)], axis=3)
    s = jnp.einsum('nhbqe,nhbke->nhbqk', qb, kb) * (E ** -0.5)
    qi = jnp.arange(Q)[:, None]
    ki = jnp.arange(2 * Q)[None, :]
    dist = qi + Q - ki
    blk = jnp.arange(nb)[:, None, None]
    valid = (dist >= 0) & (dist <= span) & (blk * Q + ki - Q >= 0)
    s = jnp.where(valid, s, MASK_VALUE)
    m = jnp.max(s, axis=-1)
    pexp = jnp.exp(s - m[..., None])
    l = jnp.sum(pexp, axis=-1)
    o = jnp.einsum('nhbqk,nhbke->nhbqe', pexp, vb)
    o = o.reshape(N, H, Lp, E)[:, :, :L]
    m = m.reshape(N, H, Lp)[:, :, :L]
    l = l.reshape(N, H, Lp)[:, :, :L]
    return o, m, l


def dilated_attention(q, k, v):
    B, S, H, E = q.shape
    outs, maxes, dens = [], [], []
    for window, d in DILATIONS:
        L = S // d
        span = window // d

        def to_residue(t):
            return t.reshape(B, L, d, H, E).transpose(0, 2, 3, 1, 4).reshape(B * d, H, L, E)

        o, m, l = banded_attention_stats(to_residue(q), to_residue(k), to_residue(v), span)
        outs.append(o.reshape(B, d, H, L, E).transpose(0, 3, 1, 2, 4).reshape(B, S, H, E))
        maxes.append(m.reshape(B, d, H, L).transpose(0, 3, 1, 2).reshape(B, S, H))
        dens.append(l.reshape(B, d, H, L).transpose(0, 3, 1, 2).reshape(B, S, H))
    m_all = jnp.stack(maxes, axis=0)
    m_glob = jnp.max(m_all, axis=0)
    w = jnp.exp(m_all - m_glob[None])
    num = sum(w[i][..., None] * outs[i] for i in range(len(DILATIONS)))
    den = sum(w[i] * dens[i] for i in range(len(DILATIONS)))
    return (num / den[..., None]).astype(q.dtype)


def _ssm_combine(left, right):
    ar_l, ai_l, br_l, bi_l = left
    ar_r, ai_r, br_r, bi_r = right
    return (ar_r * ar_l - ai_r * ai_l,
            ar_r * ai_l + ai_r * ar_l,
            ar_r * br_l - ai_r * bi_l + br_r,
            ar_r * bi_l + ai_r * br_l + bi_r)


def s5_mixer(u, lam_re, lam_im, log_dt, b_re, b_im, c_re, c_im, d_skip, w_glu, b_glu):
    B, S, _ = u.shape
    G, P, C = N_SSM_GROUPS, SSM_STATE, SSM_GROUP
    uf = u.astype(jnp.float32).reshape(B, S, G, C)
    lr = lam_re.astype(jnp.float32)
    li = lam_im.astype(jnp.float32)
    dt = jnp.exp(log_dt.astype(jnp.float32))[:, None]
    mag = jnp.exp(lr * dt)
    ar = mag * jnp.cos(li * dt)
    ai = mag * jnp.sin(li * dt)
    nr, ni = ar - 1.0, ai
    den = lr * lr + li * li
    cr = (nr * lr + ni * li) / den
    ci = (ni * lr - nr * li) / den
    br, bi = b_re.astype(jnp.float32), b_im.astype(jnp.float32)
    bbr = cr[..., None] * br - ci[..., None] * bi
    bbi = cr[..., None] * bi + ci[..., None] * br
    xr = jnp.einsum('gpc,bsgc->bsgp', bbr, uf)
    xi = jnp.einsum('gpc,bsgc->bsgp', bbi, uf)
    a_r = jnp.broadcast_to(ar[None, None], (1, S, G, P))
    a_i = jnp.broadcast_to(ai[None, None], (1, S, G, P))
    _, _, hr, hi = lax.associative_scan(_ssm_combine, (a_r, a_i, xr, xi), axis=1)
    y = (jnp.einsum('gcp,bsgp->bsgc', c_re.astype(jnp.float32), hr)
         - jnp.einsum('gcp,bsgp->bsgc', c_im.astype(jnp.float32), hi)
         + d_skip.astype(jnp.float32).reshape(G, C) * uf)
    y = jax.nn.gelu(y.reshape(B, S, SSM_WIDTH)).astype(u.dtype)
    return y * jax.nn.sigmoid(y @ w_glu + b_glu)


def setup_inputs(seed: int = 0) -> dict:
    key = jax.random.key(seed)
    ks = iter(jax.random.split(key, 40))

    def nrm(shape, scale):
        return jax.random.normal(next(ks), shape, jnp.float32) * scale

    def gain(shape):
        return 1.0 + nrm(shape, 0.02)

    L_ = DEPTH
    G, P, C = N_SSM_GROUPS, SSM_STATE, SSM_GROUP
    return {
        'x': nrm((BATCH, SEQ, D_MODEL), 1.0),
        'p': nrm((DEPTH, BATCH, SEQ, PLE_DIM), 1.0),
        'ffn1_norm': gain((L_, D_MODEL)),
        'ffn1_w_gate': nrm((L_, D_MODEL, D_FF), D_MODEL ** -0.5),
        'ffn1_w_up': nrm((L_, D_MODEL, D_FF), D_MODEL ** -0.5),
        'ffn1_w_down': nrm((L_, D_FF, D_MODEL), D_FF ** -0.5),
        'mix_norm': gain((L_, D_MODEL)),
        'w_in': nrm((L_, D_MODEL, 3 * ATTN_WIDTH + SSM_WIDTH), D_MODEL ** -0.5),
        'attn_out_norm': gain((L_, ATTN_WIDTH)),
        'ssm_lambda_re': -0.5 + nrm((L_, G, P), 0.01),
        'ssm_lambda_im': math.pi * jnp.arange(P, dtype=jnp.float32)[None, None, :] + nrm((L_, G, P), 0.01),
        'ssm_log_dt': jax.random.uniform(next(ks), (L_, G), jnp.float32, math.log(DT_MIN), math.log(DT_MAX)),
        'ssm_b_re': nrm((L_, G, P, C), (2.0 * C) ** -0.5),
        'ssm_b_im': nrm((L_, G, P, C), (2.0 * C) ** -0.5),
        'ssm_c_re': nrm((L_, G, C, P), (2.0 * P) ** -0.5),
        'ssm_c_im': nrm((L_, G, C, P), (2.0 * P) ** -0.5),
        'ssm_d': nrm((L_, SSM_WIDTH), 1.0),
        'ssm_w_glu': nrm((L_, SSM_WIDTH, SSM_WIDTH), SSM_WIDTH ** -0.5),
        'ssm_b_glu': nrm((L_, SSM_WIDTH), 0.01),
        'ssm_out_norm': gain((L_, SSM_WIDTH)),
        'w_out': nrm((L_, MIX_WIDTH, D_MODEL), MIX_WIDTH ** -0.5),
        'ffn2_norm': gain((L_, D_MODEL)),
        'ffn2_w_gate': nrm((L_, D_MODEL, D_FF), D_MODEL ** -0.5),
        'ffn2_w_up': nrm((L_, D_MODEL, D_FF), D_MODEL ** -0.5),
        'ffn2_w_down': nrm((L_, D_FF, D_MODEL), D_FF ** -0.5),
        'ple_norm': gain((L_, D_MODEL)),
        'ple_w_gate': nrm((L_, D_MODEL, D_MODEL), D_MODEL ** -0.5),
        'ple_w_proj': nrm((L_, PLE_DIM, D_MODEL), PLE_DIM ** -0.5),
        'final_norm': gain((D_MODEL,)),
    }


def reference(x, p, ffn1_norm, ffn1_w_gate, ffn1_w_up, ffn1_w_down, mix_norm, w_in,
              attn_out_norm, ssm_lambda_re, ssm_lambda_im, ssm_log_dt, ssm_b_re, ssm_b_im,
              ssm_c_re, ssm_c_im, ssm_d, ssm_w_glu, ssm_b_glu, ssm_out_norm, w_out,
              ffn2_norm, ffn2_w_gate, ffn2_w_up, ffn2_w_down, ple_norm, ple_w_gate,
              ple_w_proj, final_norm):
    B, S, _ = x.shape
    h = x
    for i in range(DEPTH):
        h = h + 0.5 * swiglu(rms_norm(h, ffn1_norm[i]), ffn1_w_gate[i], ffn1_w_up[i], ffn1_w_down[i])
        u = rms_norm(h, mix_norm[i])
        z = u @ w_in[i]
        q = z[..., :ATTN_WIDTH].reshape(B, S, N_HEADS, HEAD_DIM)
        k = z[..., ATTN_WIDTH:2 * ATTN_WIDTH].reshape(B, S, N_HEADS, HEAD_DIM)
        v = z[..., 2 * ATTN_WIDTH:3 * ATTN_WIDTH].reshape(B, S, N_HEADS, HEAD_DIM)
        s_in = z[..., 3 * ATTN_WIDTH:]
        ya = dilated_attention(q, k, v).reshape(B, S, ATTN_WIDTH)
        yb = s5_mixer(s_in, ssm_lambda_re[i], ssm_lambda_im[i], ssm_log_dt[i], ssm_b_re[i], ssm_b_im[i],
                      ssm_c_re[i], ssm_c_im[i], ssm_d[i], ssm_w_glu[i], ssm_b_glu[i])
        y = jnp.concatenate([rms_norm(ya, attn_out_norm[i]), rms_norm(yb, ssm_out_norm[i])], axis=-1)
        h = h + y @ w_out[i]
        h = h + 0.5 * swiglu(rms_norm(h, ffn2_norm[i]), ffn2_w_gate[i], ffn2_w_up[i], ffn2_w_down[i])
        gate = jax.nn.sigmoid(rms_norm(h, ple_norm[i]) @ ple_w_gate[i])
        h = h + gate * (p[i] @ ple_w_proj[i])
    return rms_norm(h, final_norm)
```

```python
import functools
import math

import jax
import jax.numpy as jnp
from jax import lax
from jax.experimental import pallas as pl
from jax.experimental.pallas import tpu as pltpu

NORM_EPS = 1e-6
MASK_VALUE = -1e30
HEAD_DIM = 64
ATTN_BLOCK = 128
ATTN_SPAN = 128
DILATIONS = (1, 4, 16)
SSM_GROUP = 16
SSM_STATE = 64
SSM_CHUNK = 16
LANES = 128
GROUPS_PER_BLOCK = LANES // SSM_GROUP
FF_TILE = 512
VMEM_CAP = 56 * 1024 * 1024

BF16 = jnp.bfloat16
F32 = jnp.float32


def _params(sem, vmem_bytes):
    return pltpu.CompilerParams(
        dimension_semantics=sem, vmem_limit_bytes=min(int(vmem_bytes), VMEM_CAP))


def _rms(x, g):
    ms = jnp.mean(x * x, axis=-1, keepdims=True)
    return x * lax.rsqrt(ms + NORM_EPS) * g


def _dot(a, b):
    return jnp.dot(a, b, preferred_element_type=F32)


def _const_spec(shape):
    return pl.BlockSpec(shape, lambda *_: (0,) * len(shape),
                        pipeline_mode=pl.Buffered(1))


def _ffn_kernel(x_ref, g_ref, wg_ref, wu_ref, wd_ref, o_ref, xn_ref):
    f = pl.program_id(1)

    @pl.when(f == 0)
    def _():
        xn_ref[...] = _rms(x_ref[...], g_ref[...]).astype(BF16)

    xn = xn_ref[...]
    gate = _dot(xn, wg_ref[...])
    up = _dot(xn, wu_ref[...])
    act = (gate * jax.nn.sigmoid(gate) * up).astype(BF16)
    part = _dot(act, wd_ref[...])

    @pl.when(f == 0)
    def _():
        o_ref[...] = part

    @pl.when(f > 0)
    def _():
        o_ref[...] += part

    @pl.when(f == pl.num_programs(1) - 1)
    def _():
        o_ref[...] = x_ref[...] + 0.5 * o_ref[...]


def _ffn(h, g, wg, wu, wd, *, tm=512, tf=FF_TILE):
    m, d = h.shape
    ff = wg.shape[1]
    tm = min(tm, m)
    vmem = (4 * tm * d * 4 + tm * d * 2 + 6 * d * tf * 2 + 6 * tm * tf * 4) * 1.25
    return pl.pallas_call(
        _ffn_kernel,
        grid=(m // tm, ff // tf),
        in_specs=[
            pl.BlockSpec((tm, d), lambda i, f: (i, 0)),
            pl.BlockSpec((1, d), lambda i, f: (0, 0)),
            pl.BlockSpec((d, tf), lambda i, f: (0, f)),
            pl.BlockSpec((d, tf), lambda i, f: (0, f)),
            pl.BlockSpec((tf, d), lambda i, f: (f, 0)),
        ],
        out_specs=pl.BlockSpec((tm, d), lambda i, f: (i, 0)),
        out_shape=jax.ShapeDtypeStruct((m, d), F32),
        scratch_shapes=[pltpu.VMEM((tm, d), BF16)],
        compiler_params=_params(("parallel", "arbitrary"), vmem),
        name="ffn",
    )(h, g.reshape(1, d), wg, wu, wd)


def _pad_ff(w, axis, tf=FF_TILE):
    ff = w.shape[axis]
    pad = (-ff) % tf
    widths = [(0, 0), (0, 0)]
    widths[axis] = (0, pad)
    return jnp.pad(w, widths).astype(BF16)


def _norm_matmul_kernel(x_ref, g_ref, w_ref, o_ref, xn_ref):
    @pl.when(pl.program_id(1) == 0)
    def _():
        xn_ref[...] = _rms(x_ref[...], g_ref[...]).astype(BF16)

    o_ref[...] = _dot(xn_ref[...], w_ref[...])


def _norm_matmul(h, g, w, *, tm=512, tn=1024):
    m, d = h.shape
    n = w.shape[1]
    tm, tn = min(tm, m), min(tn, n)
    vmem = (2 * tm * d * 4 + tm * d * 2 + 2 * d * tn * 2 + 3 * tm * tn * 4) * 1.25
    return pl.pallas_call(
        _norm_matmul_kernel,
        grid=(m // tm, n // tn),
        in_specs=[
            pl.BlockSpec((tm, d), lambda i, j: (i, 0)),
            pl.BlockSpec((1, d), lambda i, j: (0, 0)),
            pl.BlockSpec((d, tn), lambda i, j: (0, j)),
        ],
        out_specs=pl.BlockSpec((tm, tn), lambda i, j: (i, j)),
        out_shape=jax.ShapeDtypeStruct((m, n), F32),
        scratch_shapes=[pltpu.VMEM((tm, d), BF16)],
        compiler_params=_params(("parallel", "arbitrary"), vmem),
        name="in_proj",
    )(h, g.reshape(1, d), w)


def _attn_kernel(q_ref, k_ref, v_ref, o_ref, m_ref, l_ref):
    seq = q_ref.shape[0]
    blk = ATTN_BLOCK
    n_units = seq // blk
    lane = lax.broadcasted_iota(jnp.int32, (blk, LANES), 1)
    head0 = lane < HEAD_DIM
    qi = lax.broadcasted_iota(jnp.int32, (blk, 2 * blk), 0)
    ki = lax.broadcasted_iota(jnp.int32, (blk, 2 * blk), 1)
    base_diff = qi - ki
    scale = HEAD_DIM ** -0.5

    def rows(start, size, d):
        if d == 1:
            return pl.ds(pl.multiple_of(start, blk), size)
        return pl.ds(start, size, stride=d)

    for pattern, d in enumerate(DILATIONS):
        first = pattern == 0
        last = pattern == len(DILATIONS) - 1
        blocks_per_residue = n_units // d

        def unit(u, carry, d=d, first=first, last=last,
                 blocks_per_residue=blocks_per_residue):
            r = u // blocks_per_residue
            i = u % blocks_per_residue
            kb = jnp.maximum(i - 1, 0)
            q_rows = rows(r + d * blk * i, blk, d)
            k_rows = rows(r + d * blk * kb, 2 * blk, d)
            dist = base_diff + blk * (i - kb)
            valid = (dist >= 0) & (dist <= ATTN_SPAN)

            q = q_ref[q_rows, :] * scale
            k = k_ref[k_rows, :].astype(BF16)
            v = v_ref[k_rows, :].astype(BF16)

            def one_head(sel):
                qh = jnp.where(sel, q, 0.0).astype(BF16)
                s = lax.dot_general(qh, k, (((1,), (1,)), ((), ())),
                                    preferred_element_type=F32)
                s = jnp.where(valid, s, MASK_VALUE)
                m = jnp.max(s, axis=-1, keepdims=True)
                p = jnp.exp(s - m)
                l = jnp.sum(p, axis=-1, keepdims=True)
                o = _dot(p.astype(BF16), v)
                return o, m, l

            o0, m0, l0 = one_head(head0)
            o1, m1, l1 = one_head(jnp.logical_not(head0))
            o_new = jnp.where(head0, o0, o1)
            m_new = jnp.where(head0, m0, m1)
            l_new = jnp.where(head0, l0, l1)

            if not first:
                o_old = o_ref[q_rows, :]
                m_old = m_ref[q_rows, :]
                l_old = l_ref[q_rows, :]
                m_tot = jnp.maximum(m_old, m_new)
                w_old = jnp.exp(m_old - m_tot)
                w_new = jnp.exp(m_new - m_tot)
                o_new = w_old * o_old + w_new * o_new
                l_new = w_old * l_old + w_new * l_new
                m_new = m_tot
            if last:
                o_ref[q_rows, :] = o_new / l_new
            else:
                o_ref[q_rows, :] = o_new
                m_ref[q_rows, :] = m_new
                l_ref[q_rows, :] = l_new
            return carry

        lax.fori_loop(0, n_units, unit, 0)


def _attention(z, n_pairs, *, q_col, k_col, v_col):
    b, s, _ = z.shape
    assert s % (ATTN_BLOCK * DILATIONS[-1]) == 0

    def spec(col):
        return pl.BlockSpec((None, s, LANES), lambda bi, h: (bi, 0, col + h))

    vmem = (10 * s * LANES * 4) * 1.15 + (8 << 20)
    return pl.pallas_call(
        _attn_kernel,
        grid=(b, n_pairs),
        in_specs=[spec(q_col), spec(k_col), spec(v_col)],
        out_specs=spec(0),
        out_shape=jax.ShapeDtypeStruct((b, s, n_pairs * LANES), F32),
        scratch_shapes=[pltpu.VMEM((s, LANES), F32), pltpu.VMEM((s, LANES), F32)],
        compiler_params=_params(("parallel", "parallel"), vmem),
        name="attention",
    )(z, z, z)


def _ssm_weights(lam_re, lam_im, log_dt, b_re, b_im, c_re, c_im, d_skip, n_chunks):
    hp = lax.Precision.HIGHEST
    g, p = lam_re.shape
    c = b_re.shape[-1]
    t = SSM_CHUNK
    gb = GROUPS_PER_BLOCK
    nj = g // gb
    lr, li = lam_re.astype(F32), lam_im.astype(F32)
    dt = jnp.exp(log_dt.astype(F32))[:, None]
    mag = jnp.exp(lr * dt)
    ar = mag * jnp.cos(li * dt)
    ai = mag * jnp.sin(li * dt)
    nr, ni = ar - 1.0, ai
    den = lr * lr + li * li
    cr = (nr * lr + ni * li) / den
    ci = (ni * lr - nr * li) / den
    br, bi = b_re.astype(F32), b_im.astype(F32)
    bbr = cr[..., None] * br - ci[..., None] * bi
    bbi = cr[..., None] * bi + ci[..., None] * br

    def power(k):
        kk = k.astype(F32)[:, None, None]
        pm = jnp.exp(kk * (lr * dt))
        ang = kk * (li * dt)
        return pm * jnp.cos(ang), pm * jnp.sin(ang)

    pr, pi = power(jnp.arange(t + 1))
    cre, cim = c_re.astype(F32), c_im.astype(F32)
    car = cre[None] * pr[:, :, None, :] - cim[None] * pi[:, :, None, :]
    cai = cre[None] * pi[:, :, None, :] + cim[None] * pr[:, :, None, :]
    kern = (jnp.einsum('kgdp,gpc->kgdc', car[:t], bbr, precision=hp)
            - jnp.einsum('kgdp,gpc->kgdc', cai[:t], bbi, precision=hp))
    kern = kern.at[0].add(d_skip.astype(F32).reshape(g, c)[:, :, None] * jnp.eye(c, dtype=F32))

    eye = jnp.eye(gb, dtype=F32)
    lag = jnp.arange(t)[None, :] - jnp.arange(t)[:, None]
    toep = jnp.where((lag >= 0)[:, :, None, None, None],
                     kern[jnp.clip(lag, 0, t - 1)], 0.0)
    toep = toep.reshape(t, t, nj, gb, c, c)
    m_mat = jnp.einsum('abjgdc,gh->jagcbhd', toep, eye).reshape(nj, t * gb * c, t * gb * c)

    rev_r, rev_i = pr[:t][::-1], pi[:t][::-1]
    abr = rev_r[..., None] * bbr[None] - rev_i[..., None] * bbi[None]
    abi = rev_r[..., None] * bbi[None] + rev_i[..., None] * bbr[None]
    ab = jnp.stack([abr, abi], axis=0).reshape(2, t, nj, gb, p, c)
    b_mat = jnp.einsum('rajgpc,gh->jagcrhp', ab, eye).reshape(nj, t * gb * c, 2 * gb * p)

    cm = jnp.stack([car[1:], -cai[1:]], axis=0).reshape(2, t, nj, gb, c, p)
    c_mat = jnp.einsum('rbjgdp,gh->jrgpbhd', cm, eye).reshape(nj, 2 * gb * p, t * gb * c)

    n_steps = max(1, (n_chunks - 1).bit_length())
    sr, si = power(t * (2 ** jnp.arange(n_steps)))
    rows = -(-n_steps // 8) * 8
    sr = jnp.pad(sr.reshape(n_steps, nj, gb * p).transpose(1, 0, 2), ((0, 0), (0, rows - n_steps), (0, 0)))
    si = jnp.pad(si.reshape(n_steps, nj, gb * p).transpose(1, 0, 2), ((0, 0), (0, rows - n_steps), (0, 0)))
    return m_mat.astype(BF16), b_mat.astype(BF16), c_mat.astype(BF16), sr, si


def _ssm_kernel(s_ref, m_ref, b_ref, c_ref, sr_ref, si_ref, y_ref):
    seq = s_ref.shape[0]
    t = SSM_CHUNK
    n_chunks = seq // t
    half = GROUPS_PER_BLOCK * SSM_STATE

    u = jnp.concatenate(
        [s_ref[pl.ds(tau, n_chunks, stride=t), :].astype(BF16) for tau in range(t)], axis=1)
    y = _dot(u, m_ref[...])
    x = _dot(u, b_ref[...])
    hr, hi = x[:, :half], x[:, half:]

    row = lax.broadcasted_iota(jnp.int32, (n_chunks, half), 0)

    def shifted(h, k):
        return jnp.where(row >= k, pltpu.roll(h, k, 0), 0.0)

    step = 0
    while (1 << step) < n_chunks:
        k = 1 << step
        pr = sr_ref[step:step + 1, :]
        pi = si_ref[step:step + 1, :]
        zr, zi = shifted(hr, k), shifted(hi, k)
        hr, hi = hr + pr * zr - pi * zi, hi + pr * zi + pi * zr
        step += 1

    h_prev = jnp.concatenate([shifted(hr, 1), shifted(hi, 1)], axis=1).astype(BF16)
    y = jax.nn.gelu(y + _dot(h_prev, c_ref[...]))
    for tau in range(t):
        y_ref[pl.ds(tau, n_chunks, stride=t), :] = y[:, tau * LANES:(tau + 1) * LANES]


def _ssm(z, s_col, weights):
    m_mat, b_mat, c_mat, sr, si = weights
    b, s, _ = z.shape
    nj = m_mat.shape[0]
    assert s % SSM_CHUNK == 0

    def wspec(a):
        return pl.BlockSpec((None,) + a.shape[1:], lambda j, bi: (j, 0, 0),
                            pipeline_mode=pl.Buffered(1))

    n_chunks = s // SSM_CHUNK
    wbytes = sum(math.prod(a.shape[1:]) * a.dtype.itemsize for a in weights)
    vmem = (4 * s * LANES * 4 + wbytes + n_chunks * (m_mat.shape[1] * 10 + b_mat.shape[2] * 24)) * 1.2
    return pl.pallas_call(
        _ssm_kernel,
        grid=(nj, b),
        in_specs=[pl.BlockSpec((None, s, LANES), lambda j, bi: (bi, 0, s_col + j)),
                  wspec(m_mat), wspec(b_mat), wspec(c_mat), wspec(sr), wspec(si)],
        out_specs=pl.BlockSpec((None, s, LANES), lambda j, bi: (bi, 0, j)),
        out_shape=jax.ShapeDtypeStruct((b, s, nj * LANES), F32),
        compiler_params=_params(("parallel", "parallel"), vmem),
        name="ssm",
    )(z, m_mat, b_mat, c_mat, sr, si)


def _mix_out_kernel(ya_ref, y_ref, h_ref, ga_ref, gb_ref, wglu_ref, bglu_ref,
                    wa_ref, wb_ref, o_ref):
    y = y_ref[...]
    gate = _dot(y.astype(BF16), wglu_ref[...]) + bglu_ref[...]
    yb = y * jax.nn.sigmoid(gate)
    na = _rms(ya_ref[...], ga_ref[...]).astype(BF16)
    nb = _rms(yb, gb_ref[...]).astype(BF16)
    o_ref[...] = h_ref[...] + _dot(na, wa_ref[...]) + _dot(nb, wb_ref[...])


def _mix_out(ya, y, h, ga, gb, wglu, bglu, wa, wb, *, tm=512):
    m, d = h.shape
    wa_w, wb_w = ya.shape[1], y.shape[1]
    tm = min(tm, m)
    vmem = (2 * tm * (wa_w + wb_w + 2 * d) * 4 + (wglu.size + wa.size + wb.size) * 2
            + 8 * tm * d * 4) * 1.2
    row = lambda w: pl.BlockSpec((tm, w), lambda i: (i, 0))
    return pl.pallas_call(
        _mix_out_kernel,
        grid=(m // tm,),
        in_specs=[row(wa_w), row(wb_w), row(d),
                  _const_spec((1, wa_w)), _const_spec((1, wb_w)),
                  _const_spec(wglu.shape), _const_spec((1, wb_w)),
                  _const_spec(wa.shape), _const_spec(wb.shape)],
        out_specs=row(d),
        out_shape=jax.ShapeDtypeStruct((m, d), F32),
        compiler_params=_params(("parallel",), vmem),
        name="mix_out",
    )(ya, y, h, ga.reshape(1, -1), gb.reshape(1, -1), wglu, bglu.reshape(1, -1), wa, wb)


def _ple_kernel(h_ref, p_ref, gn_ref, wg_ref, wp_ref, fn_ref, o_ref, *, final):
    h = h_ref[...]
    gate = jax.nn.sigmoid(_dot(_rms(h, gn_ref[...]).astype(BF16), wg_ref[...]))
    proj = _dot(p_ref[...].astype(BF16), wp_ref[...])
    out = h + gate * proj
    o_ref[...] = _rms(out, fn_ref[...]) if final else out


def _ple(h, p, gn, wg, wp, fn, *, final, tm=512):
    m, d = h.shape
    pd = p.shape[1]
    tm = min(tm, m)
    vmem = (2 * tm * (2 * d + pd) * 4 + (wg.size + wp.size) * 2 + 8 * tm * d * 4) * 1.2
    row = lambda w: pl.BlockSpec((tm, w), lambda i: (i, 0))
    return pl.pallas_call(
        functools.partial(_ple_kernel, final=final),
        grid=(m // tm,),
        in_specs=[row(d), row(pd), _const_spec((1, d)), _const_spec(wg.shape),
                  _const_spec(wp.shape), _const_spec((1, d))],
        out_specs=row(d),
        out_shape=jax.ShapeDtypeStruct((m, d), F32),
        compiler_params=_params(("parallel",), vmem),
        name="ple",
    )(h, p, gn.reshape(1, d), wg, wp, fn.reshape(1, d))


def kernel(x, p, ffn1_norm, ffn1_w_gate, ffn1_w_up, ffn1_w_down, mix_norm, w_in, attn_out_norm, ssm_lambda_re, ssm_lambda_im, ssm_log_dt, ssm_b_re, ssm_b_im, ssm_c_re, ssm_c_im, ssm_d, ssm_w_glu, ssm_b_glu, ssm_out_norm, w_out, ffn2_norm, ffn2_w_gate, ffn2_w_up, ffn2_w_down, ple_norm, ple_w_gate, ple_w_proj, final_norm):
    b, s, d = x.shape
    depth = p.shape[0]
    attn_w = attn_out_norm.shape[1]
    ssm_w = ssm_out_norm.shape[1]
    n_pairs = attn_w // LANES
    h = x.reshape(b * s, d)
    for i in range(depth):
        h = _ffn(h, ffn1_norm[i], _pad_ff(ffn1_w_gate[i], 1), _pad_ff(ffn1_w_up[i], 1),
                 _pad_ff(ffn1_w_down[i], 0))
        z = _norm_matmul(h, mix_norm[i], w_in[i].astype(BF16)).reshape(b, s, -1)
        ya = _attention(z, n_pairs, q_col=0, k_col=n_pairs, v_col=2 * n_pairs)
        weights = _ssm_weights(ssm_lambda_re[i], ssm_lambda_im[i], ssm_log_dt[i],
                               ssm_b_re[i], ssm_b_im[i], ssm_c_re[i], ssm_c_im[i],
                               ssm_d[i], s // SSM_CHUNK)
        y = _ssm(z, 3 * n_pairs, weights)
        wo = w_out[i].astype(BF16)
        h = _mix_out(ya.reshape(b * s, attn_w), y.reshape(b * s, ssm_w), h,
                     attn_out_norm[i], ssm_out_norm[i], ssm_w_glu[i].astype(BF16),
                     ssm_b_glu[i], wo[:attn_w], wo[attn_w:])
        h = _ffn(h, ffn2_norm[i], _pad_ff(ffn2_w_gate[i], 1), _pad_ff(ffn2_w_up[i], 1),
                 _pad_ff(ffn2_w_down[i], 0))
        h = _ple(h, p[i].reshape(b * s, -1), ple_norm[i], ple_w_gate[i].astype(BF16),
                 ple_w_proj[i].astype(BF16), final_norm, final=i == depth - 1)
    return h.reshape(b, s, d)
```

```python
import functools
import math

import jax
import jax.numpy as jnp
from jax import lax
from jax.experimental import pallas as pl
from jax.experimental.pallas import tpu as pltpu

NORM_EPS = 1e-6
MASK_VALUE = -1e30
HEAD_DIM = 64
ATTN_BLOCK = 128
ATTN_SPAN = 128
DILATIONS = (1, 4, 16)
ATTN_UNROLL = 4
SSM_GROUP = 16
SSM_STATE = 64
SSM_CHUNK = 16
LANES = 128
GROUPS_PER_BLOCK = LANES // SSM_GROUP
FF_TILE = 512
VMEM_CAP = 56 * 1024 * 1024

BF16 = jnp.bfloat16
F32 = jnp.float32


def _params(sem, vmem_bytes):
    return pltpu.CompilerParams(
        dimension_semantics=sem, vmem_limit_bytes=min(int(vmem_bytes), VMEM_CAP))


def _rms(x, g):
    ms = jnp.mean(x * x, axis=-1, keepdims=True)
    return x * lax.rsqrt(ms + NORM_EPS) * g


def _dot(a, b):
    return jnp.dot(a, b, preferred_element_type=F32)


def _const_spec(shape):
    return pl.BlockSpec(shape, lambda *_: (0,) * len(shape),
                        pipeline_mode=pl.Buffered(1))


def _ffn_kernel(x_ref, g_ref, wg_ref, wu_ref, wd_ref, o_ref, xn_ref):
    f = pl.program_id(1)

    @pl.when(f == 0)
    def _():
        xn_ref[...] = _rms(x_ref[...], g_ref[...]).astype(BF16)

    xn = xn_ref[...]
    gate = _dot(xn, wg_ref[...])
    up = _dot(xn, wu_ref[...])
    act = (gate * jax.nn.sigmoid(gate) * up).astype(BF16)
    part = _dot(act, wd_ref[...])

    @pl.when(f == 0)
    def _():
        o_ref[...] = part

    @pl.when(f > 0)
    def _():
        o_ref[...] += part

    @pl.when(f == pl.num_programs(1) - 1)
    def _():
        o_ref[...] = x_ref[...] + 0.5 * o_ref[...]


def _ffn(h, g, wg, wu, wd, *, tm=512, tf=FF_TILE):
    m, d = h.shape
    ff = wg.shape[1]
    tm = min(tm, m)
    vmem = (4 * tm * d * 4 + tm * d * 2 + 6 * d * tf * 2 + 6 * tm * tf * 4) * 1.25
    return pl.pallas_call(
        _ffn_kernel,
        grid=(m // tm, ff // tf),
        in_specs=[
            pl.BlockSpec((tm, d), lambda i, f: (i, 0)),
            pl.BlockSpec((1, d), lambda i, f: (0, 0)),
            pl.BlockSpec((d, tf), lambda i, f: (0, f)),
            pl.BlockSpec((d, tf), lambda i, f: (0, f)),
            pl.BlockSpec((tf, d), lambda i, f: (f, 0)),
        ],
        out_specs=pl.BlockSpec((tm, d), lambda i, f: (i, 0)),
        out_shape=jax.ShapeDtypeStruct((m, d), F32),
        scratch_shapes=[pltpu.VMEM((tm, d), BF16)],
        compiler_params=_params(("parallel", "arbitrary"), vmem),
        name="ffn",
    )(h, g.reshape(1, d), wg, wu, wd)


def _pad_ff(w, axis, tf=FF_TILE):
    ff = w.shape[axis]
    pad = (-ff) % tf
    widths = [(0, 0), (0, 0)]
    widths[axis] = (0, pad)
    return jnp.pad(w, widths).astype(BF16)


def _norm_matmul_kernel(x_ref, g_ref, w_ref, o_ref, xn_ref):
    @pl.when(pl.program_id(1) == 0)
    def _():
        xn_ref[...] = _rms(x_ref[...], g_ref[...]).astype(BF16)

    o_ref[...] = _dot(xn_ref[...], w_ref[...])


def _norm_matmul(h, g, w, *, tm=512, tn=1024):
    m, d = h.shape
    n = w.shape[1]
    tm, tn = min(tm, m), min(tn, n)
    vmem = (2 * tm * d * 4 + tm * d * 2 + 2 * d * tn * 2 + 3 * tm * tn * 4) * 1.25
    return pl.pallas_call(
        _norm_matmul_kernel,
        grid=(m // tm, n // tn),
        in_specs=[
            pl.BlockSpec((tm, d), lambda i, j: (i, 0)),
            pl.BlockSpec((1, d), lambda i, j: (0, 0)),
            pl.BlockSpec((d, tn), lambda i, j: (0, j)),
        ],
        out_specs=pl.BlockSpec((tm, tn), lambda i, j: (i, j)),
        out_shape=jax.ShapeDtypeStruct((m, n), F32),
        scratch_shapes=[pltpu.VMEM((tm, d), BF16)],
        compiler_params=_params(("parallel", "arbitrary"), vmem),
        name="in_proj",
    )(h, g.reshape(1, d), w)


def _attn_kernel(q_ref, k_ref, v_ref, o_ref, m_ref, l_ref, bias_ref):
    seq = q_ref.shape[0]
    blk = ATTN_BLOCK
    n_units = seq // blk
    lane = lax.broadcasted_iota(jnp.int32, (blk, LANES), 1)
    head0 = lane < HEAD_DIM
    scale = HEAD_DIM ** -0.5

    qi = lax.broadcasted_iota(jnp.int32, (blk, 2 * blk), 0)
    ki = lax.broadcasted_iota(jnp.int32, (blk, 2 * blk), 1)
    for shift in range(2):
        dist = qi - ki + blk * shift
        bias_ref[shift] = jnp.where((dist >= 0) & (dist <= ATTN_SPAN), 0.0, MASK_VALUE)

    def rows(start, size, d):
        if d == 1:
            return pl.ds(pl.multiple_of(start, blk), size)
        return pl.ds(start, size, stride=d)

    def one_head(q, k, v, bias, sel):
        qh = jnp.where(sel, q, 0.0).astype(BF16)
        s = lax.dot_general(qh, k, (((1,), (1,)), ((), ())),
                            preferred_element_type=F32) + bias
        m = jnp.max(s, axis=-1, keepdims=True)
        p = jnp.exp(s - m)
        l = jnp.sum(p, axis=-1, keepdims=True)
        return _dot(p.astype(BF16), v), m, l

    for pattern, d in enumerate(DILATIONS):
        first = pattern == 0
        last = pattern == len(DILATIONS) - 1
        blocks_per_residue = n_units // d

        def group(g, carry, d=d, first=first, last=last,
                  blocks_per_residue=blocks_per_residue):
            loaded = []
            for uu in range(ATTN_UNROLL):
                u = g * ATTN_UNROLL + uu
                r = u // blocks_per_residue
                i = u % blocks_per_residue
                kb = jnp.maximum(i - 1, 0)
                q_rows = rows(r + d * blk * i, blk, d)
                k_rows = rows(r + d * blk * kb, 2 * blk, d)
                old = None if first else (o_ref[q_rows, :], m_ref[q_rows, :], l_ref[q_rows, :])
                loaded.append((q_rows, q_ref[q_rows, :] * scale,
                               k_ref[k_rows, :].astype(BF16), v_ref[k_rows, :].astype(BF16),
                               bias_ref[i - kb], old))
            results = []
            for q_rows, q, k, v, bias, old in loaded:
                o0, m0, l0 = one_head(q, k, v, bias, head0)
                o1, m1, l1 = one_head(q, k, v, bias, jnp.logical_not(head0))
                o_new = jnp.where(head0, o0, o1)
                m_new = jnp.where(head0, m0, m1)
                l_new = jnp.where(head0, l0, l1)
                if not first:
                    o_old, m_old, l_old = old
                    m_tot = jnp.maximum(m_old, m_new)
                    w_old = jnp.exp(m_old - m_tot)
                    w_new = jnp.exp(m_new - m_tot)
                    o_new = w_old * o_old + w_new * o_new
                    l_new = w_old * l_old + w_new * l_new
                    m_new = m_tot
                results.append((q_rows, o_new, m_new, l_new))
            for q_rows, o_new, m_new, l_new in results:
                if last:
                    o_ref[q_rows, :] = o_new / l_new
                else:
                    o_ref[q_rows, :] = o_new
                    m_ref[q_rows, :] = m_new
                    l_ref[q_rows, :] = l_new
            return carry

        lax.fori_loop(0, n_units // ATTN_UNROLL, group, 0)


def _attention(z, n_pairs, *, q_col, k_col, v_col):
    b, s, _ = z.shape
    assert s % (ATTN_BLOCK * DILATIONS[-1]) == 0 and (s // ATTN_BLOCK) % ATTN_UNROLL == 0

    def spec(col):
        return pl.BlockSpec((None, s, LANES), lambda bi, h: (bi, 0, col + h))

    vmem = (10 * s * LANES * 4) * 1.15 + (8 << 20)
    return pl.pallas_call(
        _attn_kernel,
        grid=(b, n_pairs),
        in_specs=[spec(q_col), spec(k_col), spec(v_col)],
        out_specs=spec(0),
        out_shape=jax.ShapeDtypeStruct((b, s, n_pairs * LANES), F32),
        scratch_shapes=[pltpu.VMEM((s, LANES), F32), pltpu.VMEM((s, LANES), F32),
                        pltpu.VMEM((2, ATTN_BLOCK, 2 * ATTN_BLOCK), F32)],
        compiler_params=_params(("parallel", "parallel"), vmem),
        name="attention",
    )(z, z, z)


def _ssm_weights(lam_re, lam_im, log_dt, b_re, b_im, c_re, c_im, d_skip, n_chunks):
    hp = lax.Precision.HIGHEST
    g, p = lam_re.shape
    c = b_re.shape[-1]
    t = SSM_CHUNK
    gb = GROUPS_PER_BLOCK
    nj = g // gb
    lr, li = lam_re.astype(F32), lam_im.astype(F32)
    dt = jnp.exp(log_dt.astype(F32))[:, None]
    mag = jnp.exp(lr * dt)
    ar = mag * jnp.cos(li * dt)
    ai = mag * jnp.sin(li * dt)
    nr, ni = ar - 1.0, ai
    den = lr * lr + li * li
    cr = (nr * lr + ni * li) / den
    ci = (ni * lr - nr * li) / den
    br, bi = b_re.astype(F32), b_im.astype(F32)
    bbr = cr[..., None] * br - ci[..., None] * bi
    bbi = cr[..., None] * bi + ci[..., None] * br

    def power(k):
        kk = k.astype(F32)[:, None, None]
        pm = jnp.exp(kk * (lr * dt))
        ang = kk * (li * dt)
        return pm * jnp.cos(ang), pm * jnp.sin(ang)

    pr, pi = power(jnp.arange(t + 1))
    cre, cim = c_re.astype(F32), c_im.astype(F32)
    car = cre[None] * pr[:, :, None, :] - cim[None] * pi[:, :, None, :]
    cai = cre[None] * pi[:, :, None, :] + cim[None] * pr[:, :, None, :]
    kern = (jnp.einsum('kgdp,gpc->kgdc', car[:t], bbr, precision=hp)
            - jnp.einsum('kgdp,gpc->kgdc', cai[:t], bbi, precision=hp))
    kern = kern.at[0].add(d_skip.astype(F32).reshape(g, c)[:, :, None] * jnp.eye(c, dtype=F32))

    kr = kern.reshape(t, nj, gb, c, c).transpose(1, 0, 4, 2, 3).reshape(nj, t, c, gb * c)

    rev_r, rev_i = pr[:t][::-1], pi[:t][::-1]
    abr = rev_r[..., None] * bbr[None] - rev_i[..., None] * bbi[None]
    abi = rev_r[..., None] * bbi[None] + rev_i[..., None] * bbr[None]

    def rows_form(x):
        x = x.reshape(t, nj, gb, p, c).transpose(1, 0, 2, 4, 3).reshape(nj, t * gb * c, p)
        return jnp.concatenate([x, x], axis=-1).astype(BF16)

    def cols_form(x):
        x = x.reshape(t, nj, gb, c, p).transpose(1, 4, 0, 2, 3).reshape(nj, p, t * gb * c)
        return jnp.concatenate([x, x], axis=1).astype(BF16)

    n_steps = max(1, (n_chunks - 1).bit_length())
    sr, si = power(t * (2 ** jnp.arange(n_steps)))
    rows = -(-n_steps // 8) * 8
    sr = jnp.pad(sr.reshape(n_steps, nj, gb * p).transpose(1, 0, 2), ((0, 0), (0, rows - n_steps), (0, 0)))
    si = jnp.pad(si.reshape(n_steps, nj, gb * p).transpose(1, 0, 2), ((0, 0), (0, rows - n_steps), (0, 0)))
    return (kr.astype(BF16), rows_form(abr), rows_form(abi),
            cols_form(car[1:]), cols_form(-cai[1:]), sr, si)


def _ssm_expand(kr_ref, abr_ref, abi_ref, cmr_ref, cmi_ref, m_sc, b_sc, c_sc):
    t, c, gb, p = SSM_CHUNK, SSM_GROUP, GROUPS_PER_BLOCK, SSM_STATE
    shift = c.bit_length() - 1
    half = gb * p
    pairs = LANES // p

    def group_of(idx):
        return (idx >> shift) & (gb - 1)

    m_sc[...] = jnp.zeros_like(m_sc)
    col_g = group_of(lax.broadcasted_iota(jnp.int32, (c, LANES), 1))
    for k in range(t):
        kk = kr_ref[k]
        dk = jnp.concatenate([jnp.where(col_g == g, kk, jnp.zeros_like(kk)) for g in range(gb)], axis=0)
        for tau in range(t - k):
            m_sc[tau * LANES:(tau + 1) * LANES, (tau + k) * LANES:(tau + k + 1) * LANES] = dk

    row_g = group_of(lax.broadcasted_iota(jnp.int32, (t * LANES, LANES), 0))
    lane_half = lax.broadcasted_iota(jnp.int32, (t * LANES, LANES), 1) >> (p.bit_length() - 1)
    for part, ab_ref in enumerate((abr_ref, abi_ref)):
        ab = ab_ref[...]
        for q in range(gb // pairs):
            b_sc[:, part * half + q * LANES:part * half + (q + 1) * LANES] = jnp.where(
                row_g == pairs * q + lane_half, ab, jnp.zeros_like(ab))

    col_g2 = group_of(lax.broadcasted_iota(jnp.int32, (LANES, t * LANES), 1))
    row_half = lax.broadcasted_iota(jnp.int32, (LANES, t * LANES), 0) >> (p.bit_length() - 1)
    for part, cm_ref in enumerate((cmr_ref, cmi_ref)):
        cm = cm_ref[...]
        for q in range(gb // pairs):
            c_sc[part * half + q * LANES:part * half + (q + 1) * LANES, :] = jnp.where(
                col_g2 == pairs * q + row_half, cm, jnp.zeros_like(cm))


def _ssm_kernel(s_ref, kr_ref, abr_ref, abi_ref, cmr_ref, cmi_ref, sr_ref, si_ref, y_ref,
                m_sc, b_sc, c_sc):
    seq = s_ref.shape[0]
    t = SSM_CHUNK
    n_chunks = seq // t
    half = GROUPS_PER_BLOCK * SSM_STATE

    @pl.when(pl.program_id(1) == 0)
    def _():
        _ssm_expand(kr_ref, abr_ref, abi_ref, cmr_ref, cmi_ref, m_sc, b_sc, c_sc)

    u = jnp.concatenate(
        [s_ref[pl.ds(tau, n_chunks, stride=t), :].astype(BF16) for tau in range(t)], axis=1)
    y = _dot(u, m_sc[...])
    x = _dot(u, b_sc[...])
    hr, hi = x[:, :half], x[:, half:]

    row = lax.broadcasted_iota(jnp.int32, (n_chunks, half), 0)

    def shifted(h, k):
        return jnp.where(row >= k, pltpu.roll(h, k, 0), 0.0)

    step = 0
    while (1 << step) < n_chunks:
        k = 1 << step
        pr = sr_ref[step:step + 1, :]
        pi = si_ref[step:step + 1, :]
        zr, zi = shifted(hr, k), shifted(hi, k)
        hr, hi = hr + pr * zr - pi * zi, hi + pr * zi + pi * zr
        step += 1

    h_prev = jnp.concatenate([shifted(hr, 1), shifted(hi, 1)], axis=1).astype(BF16)
    y = jax.nn.gelu(y + _dot(h_prev, c_sc[...]))
    for tau in range(t):
        y_ref[pl.ds(tau, n_chunks, stride=t), :] = y[:, tau * LANES:(tau + 1) * LANES]


def _ssm(z, s_col, weights):
    b, s, _ = z.shape
    nj = weights[0].shape[0]
    assert s % SSM_CHUNK == 0
    assert SSM_GROUP & (SSM_GROUP - 1) == 0 and SSM_STATE & (SSM_STATE - 1) == 0

    def wspec(a):
        nd = a.ndim - 1
        return pl.BlockSpec((None,) + a.shape[1:], lambda j, bi: (j,) + (0,) * nd)

    n_chunks = s // SSM_CHUNK
    width = SSM_CHUNK * LANES
    state = 2 * GROUPS_PER_BLOCK * SSM_STATE
    scratch = (width * width + 2 * width * state) * 2
    wbytes = 2 * sum(math.prod(a.shape[1:]) * a.dtype.itemsize for a in weights)
    vmem = (4 * s * LANES * 4 + wbytes + scratch + n_chunks * (width * 10 + state * 12)) * 1.2
    return pl.pallas_call(
        _ssm_kernel,
        grid=(nj, b),
        in_specs=[pl.BlockSpec((None, s, LANES), lambda j, bi: (bi, 0, s_col + j))]
        + [wspec(a) for a in weights],
        out_specs=pl.BlockSpec((None, s, LANES), lambda j, bi: (bi, 0, j)),
        out_shape=jax.ShapeDtypeStruct((b, s, nj * LANES), F32),
        scratch_shapes=[pltpu.VMEM((width, width), BF16), pltpu.VMEM((width, state), BF16),
                        pltpu.VMEM((state, width), BF16)],
        compiler_params=_params(("parallel", "arbitrary"), vmem),
        name="ssm",
    )(z, *weights)


def _mix_out_kernel(ya_ref, y_ref, h_ref, ga_ref, gb_ref, wglu_ref, bglu_ref,
                    wa_ref, wb_ref, o_ref):
    y = y_ref[...]
    gate = _dot(y.astype(BF16), wglu_ref[...]) + bglu_ref[...]
    yb = y * jax.nn.sigmoid(gate)
    na = _rms(ya_ref[...], ga_ref[...]).astype(BF16)
    nb = _rms(yb, gb_ref[...]).astype(BF16)
    o_ref[...] = h_ref[...] + _dot(na, wa_ref[...]) + _dot(nb, wb_ref[...])


def _mix_out(ya, y, h, ga, gb, wglu, bglu, wa, wb, *, tm=512):
    m, d = h.shape
    wa_w, wb_w = ya.shape[1], y.shape[1]
    tm = min(tm, m)
    vmem = (2 * tm * (wa_w + wb_w + 2 * d) * 4 + (wglu.size + wa.size + wb.size) * 2
            + 8 * tm * d * 4) * 1.2
    row = lambda w: pl.BlockSpec((tm, w), lambda i: (i, 0))
    return pl.pallas_call(
        _mix_out_kernel,
        grid=(m // tm,),
        in_specs=[row(wa_w), row(wb_w), row(d),
                  _const_spec((1, wa_w)), _const_spec((1, wb_w)),
                  _const_spec(wglu.shape), _const_spec((1, wb_w)),
                  _const_spec(wa.shape), _const_spec(wb.shape)],
        out_specs=row(d),
        out_shape=jax.ShapeDtypeStruct((m, d), F32),
        compiler_params=_params(("parallel",), vmem),
        name="mix_out",
    )(ya, y, h, ga.reshape(1, -1), gb.reshape(1, -1), wglu, bglu.reshape(1, -1), wa, wb)


def _ple_kernel(h_ref, p_ref, gn_ref, wg_ref, wp_ref, fn_ref, o_ref, *, final):
    h = h_ref[...]
    gate = jax.nn.sigmoid(_dot(_rms(h, gn_ref[...]).astype(BF16), wg_ref[...]))
    proj = _dot(p_ref[...].astype(BF16), wp_ref[...])
    out = h + gate * proj
    o_ref[...] = _rms(out, fn_ref[...]) if final else out


def _ple(h, p, gn, wg, wp, fn, *, final, tm=512):
    m, d = h.shape
    pd = p.shape[1]
    tm = min(tm, m)
    vmem = (2 * tm * (2 * d + pd) * 4 + (wg.size + wp.size) * 2 + 8 * tm * d * 4) * 1.2
    row = lambda w: pl.BlockSpec((tm, w), lambda i: (i, 0))
    return pl.pallas_call(
        functools.partial(_ple_kernel, final=final),
        grid=(m // tm,),
        in_specs=[row(d), row(pd), _const_spec((1, d)), _const_spec(wg.shape),
                  _const_spec(wp.shape), _const_spec((1, d))],
        out_specs=row(d),
        out_shape=jax.ShapeDtypeStruct((m, d), F32),
        compiler_params=_params(("parallel",), vmem),
        name="ple",
    )(h, p, gn.reshape(1, d), wg, wp, fn.reshape(1, d))


def kernel(x, p, ffn1_norm, ffn1_w_gate, ffn1_w_up, ffn1_w_down, mix_norm, w_in, attn_out_norm, ssm_lambda_re, ssm_lambda_im, ssm_log_dt, ssm_b_re, ssm_b_im, ssm_c_re, ssm_c_im, ssm_d, ssm_w_glu, ssm_b_glu, ssm_out_norm, w_out, ffn2_norm, ffn2_w_gate, ffn2_w_up, ffn2_w_down, ple_norm, ple_w_gate, ple_w_proj, final_norm):
    b, s, d = x.shape
    depth = p.shape[0]
    attn_w = attn_out_norm.shape[1]
    ssm_w = ssm_out_norm.shape[1]
    n_pairs = attn_w // LANES
    h = x.reshape(b * s, d)
    for i in range(depth):
        h = _ffn(h, ffn1_norm[i], _pad_ff(ffn1_w_gate[i], 1), _pad_ff(ffn1_w_up[i], 1),
                 _pad_ff(ffn1_w_down[i], 0))
        z = _norm_matmul(h, mix_norm[i], w_in[i].astype(BF16)).reshape(b, s, -1)
        ya = _attention(z, n_pairs, q_col=0, k_col=n_pairs, v_col=2 * n_pairs)
        weights = _ssm_weights(ssm_lambda_re[i], ssm_lambda_im[i], ssm_log_dt[i],
                               ssm_b_re[i], ssm_b_im[i], ssm_c_re[i], ssm_c_im[i],
                               ssm_d[i], s // SSM_CHUNK)
        y = _ssm(z, 3 * n_pairs, weights)
        wo = w_out[i].astype(BF16)
        h = _mix_out(ya.reshape(b * s, attn_w), y.reshape(b * s, ssm_w), h,
                     attn_out_norm[i], ssm_out_norm[i], ssm_w_glu[i].astype(BF16),
                     ssm_b_glu[i], wo[:attn_w], wo[attn_w:])
        h = _ffn(h, ffn2_norm[i], _pad_ff(ffn2_w_gate[i], 1), _pad_ff(ffn2_w_up[i], 1),
                 _pad_ff(ffn2_w_down[i], 0))
        h = _ple(h, p[i].reshape(b * s, -1), ple_norm[i], ple_w_gate[i].astype(BF16),
                 ple_w_proj[i].astype(BF16), final_norm, final=i == depth - 1)
    return h.reshape(b, s, d)
```

```python
import functools
import math

import jax
import jax.numpy as jnp
from jax import lax
from jax.experimental import pallas as pl
from jax.experimental.pallas import tpu as pltpu

NORM_EPS = 1e-6
MASK_VALUE = -1e30
HEAD_DIM = 64
ATTN_BLOCK = 128
ATTN_SPAN = 128
DILATIONS = (1, 4, 16)
ATTN_UNROLL = 4
SSM_GROUP = 16
SSM_STATE = 64
SSM_CHUNK = 16
TOKEN_RES = 16
LANES = 128
GROUPS_PER_BLOCK = LANES // SSM_GROUP
FF_TILE = 512
VMEM_CAP = 56 * 1024 * 1024

BF16 = jnp.bfloat16
F32 = jnp.float32


def _params(sem, vmem_bytes):
    return pltpu.CompilerParams(
        dimension_semantics=sem, vmem_limit_bytes=min(int(vmem_bytes), VMEM_CAP))


def _rms(x, g):
    ms = jnp.mean(x * x, axis=-1, keepdims=True)
    return x * lax.rsqrt(ms + NORM_EPS) * g


def _dot(a, b):
    return jnp.dot(a, b, preferred_element_type=F32)


def _const_spec(shape):
    return pl.BlockSpec(shape, lambda *_: (0,) * len(shape),
                        pipeline_mode=pl.Buffered(1))


def _ffn_kernel(x_ref, g_ref, wg_ref, wu_ref, wd_ref, o_ref, xn_ref):
    f = pl.program_id(1)

    @pl.when(f == 0)
    def _():
        xn_ref[...] = _rms(x_ref[...], g_ref[...]).astype(BF16)
        o_ref[...] = jnp.zeros_like(o_ref)

    xn = xn_ref[...]
    gate = _dot(xn, wg_ref[...])
    up = _dot(xn, wu_ref[...])
    act = (gate * jax.nn.sigmoid(gate) * up).astype(BF16)
    o_ref[...] += _dot(act, wd_ref[...])

    @pl.when(f == pl.num_programs(1) - 1)
    def _():
        o_ref[...] = x_ref[...] + 0.5 * o_ref[...]


def _ffn(h, g, wg, wu, wd, *, tm=512, tf=FF_TILE):
    m, d = h.shape
    ff = wg.shape[1]
    tm = min(tm, m)
    vmem = (4 * tm * d * 4 + tm * d * 2 + 6 * d * tf * 2 + 6 * tm * tf * 4) * 1.25
    return pl.pallas_call(
        _ffn_kernel,
        grid=(m // tm, ff // tf),
        in_specs=[
            pl.BlockSpec((tm, d), lambda i, f: (i, 0)),
            pl.BlockSpec((1, d), lambda i, f: (0, 0)),
            pl.BlockSpec((d, tf), lambda i, f: (0, f)),
            pl.BlockSpec((d, tf), lambda i, f: (0, f)),
            pl.BlockSpec((tf, d), lambda i, f: (f, 0)),
        ],
        out_specs=pl.BlockSpec((tm, d), lambda i, f: (i, 0)),
        out_shape=jax.ShapeDtypeStruct((m, d), F32),
        scratch_shapes=[pltpu.VMEM((tm, d), BF16)],
        compiler_params=_params(("parallel", "arbitrary"), vmem),
        name="ffn",
    )(h, g.reshape(1, d), wg, wu, wd)


def _pad_ff(w, axis, tf=FF_TILE):
    ff = w.shape[axis]
    pad = (-ff) % tf
    widths = [(0, 0), (0, 0)]
    widths[axis] = (0, pad)
    return jnp.pad(w, widths).astype(BF16)


def _in_proj_kernel(x_ref, g_ref, w_ref, o_ref, xn_ref):
    n_res, tl, _ = o_ref.shape
    d = g_ref.shape[1]

    @pl.when(pl.program_id(1) == 0)
    def _():
        g = g_ref[...]
        for r in range(n_res):
            xn_ref[r * tl:(r + 1) * tl, :] = _rms(x_ref[:, r * d:(r + 1) * d], g).astype(BF16)

    res = _dot(xn_ref[...], w_ref[...])
    for r in range(n_res):
        o_ref[r] = res[r * tl:(r + 1) * tl]


def _in_proj(h, g, w, batch, *, tm=512, tn=1024):
    m, d = h.shape
    n = w.shape[1]
    seq = m // batch
    tm, tn = min(tm, seq), min(tn, n)
    tl = tm // TOKEN_RES
    tiles = seq // tm
    assert tl % 8 == 0 and seq % tm == 0
    vmem = (2 * tm * d * 4 + tm * d * 2 + 2 * d * tn * 2 + 4 * tm * tn * 4) * 1.25
    return pl.pallas_call(
        _in_proj_kernel,
        grid=(m // tm, n // tn),
        in_specs=[
            pl.BlockSpec((tl, TOKEN_RES * d), lambda i, j: (i, 0)),
            pl.BlockSpec((1, d), lambda i, j: (0, 0)),
            pl.BlockSpec((d, tn), lambda i, j: (0, j)),
        ],
        out_specs=pl.BlockSpec((None, TOKEN_RES, tl, tn),
                               lambda i, j: (i // tiles, 0, i % tiles, j)),
        out_shape=jax.ShapeDtypeStruct((batch, TOKEN_RES, seq // TOKEN_RES, n), F32),
        scratch_shapes=[pltpu.VMEM((tm, d), BF16)],
        compiler_params=_params(("parallel", "arbitrary"), vmem),
        name="in_proj",
    )(h.reshape(m // TOKEN_RES, TOKEN_RES * d), g.reshape(1, d), w)


def _attn_kernel(q_ref, k_ref, v_ref, o_ref, m_ref, l_ref, bias_ref):
    n_res, length, _ = q_ref.shape
    blk = ATTN_BLOCK
    n_units = n_res * length // blk
    head0 = lax.broadcasted_iota(jnp.int32, (blk, LANES), 1) < HEAD_DIM
    kv_head0 = lax.broadcasted_iota(jnp.int32, (2 * blk, LANES), 1) < HEAD_DIM
    scale = HEAD_DIM ** -0.5 * math.log2(math.e)

    rho = lax.broadcasted_iota(jnp.int32, (blk, 2 * blk), 0)
    kap = lax.broadcasted_iota(jnp.int32, (blk, 2 * blk), 1)
    for pattern, d in enumerate(DILATIONS):
        n_sub = n_res // d
        c_rows = blk // n_sub
        sh = c_rows.bit_length() - 1
        iq, lq = rho >> sh, rho & (c_rows - 1)
        ik, lk = kap >> (sh + 1), kap & (2 * c_rows - 1)
        for shift in range(2):
            dist = n_sub * (lq - lk + shift * c_rows) + (iq - ik)
            bias_ref[2 * pattern + shift] = jnp.where(
                (dist >= 0) & (dist <= ATTN_SPAN), 0.0, MASK_VALUE)

    def one_head(q, k, v, bias, sel, kv_sel):
        qh = jnp.where(sel, q, 0.0).astype(BF16)
        s = lax.dot_general(qh, k, (((1,), (1,)), ((), ())),
                            preferred_element_type=F32) + bias
        m = jnp.max(s, axis=-1, keepdims=True)
        p = jnp.exp2(s - m).astype(BF16)
        return _dot(p, jnp.where(kv_sel, v, jnp.ones_like(v))), m

    for pattern, d in enumerate(DILATIONS):
        first = pattern == 0
        last = pattern == len(DILATIONS) - 1
        n_sub = n_res // d
        c_rows = blk // n_sub
        blocks_per_residue = n_units // d

        def gather(ref, res, start, size, d=d, n_sub=n_sub):
            return jnp.concatenate(
                [ref[d * i + res, pl.ds(start, size), :] for i in range(n_sub)], axis=0)

        def scatter(ref, res, start, val, d=d, n_sub=n_sub, c_rows=c_rows):
            for i in range(n_sub):
                ref[d * i + res, pl.ds(start, c_rows), :] = val[i * c_rows:(i + 1) * c_rows]

        def group(g, carry, pattern=pattern, first=first, last=last, c_rows=c_rows,
                  blocks_per_residue=blocks_per_residue, gather=gather, scatter=scatter):
            loaded = []
            for uu in range(ATTN_UNROLL):
                u = g * ATTN_UNROLL + uu
                res = u // blocks_per_residue
                bq = u % blocks_per_residue
                kb = jnp.maximum(bq - 1, 0)
                q0 = pl.multiple_of(c_rows * bq, 8)
                k0 = pl.multiple_of(c_rows * kb, 8)
                old = None if first else tuple(
                    gather(ref, res, q0, c_rows) for ref in (o_ref, m_ref, l_ref))
                loaded.append((res, q0, gather(q_ref, res, q0, c_rows) * scale,
                               gather(k_ref, res, k0, 2 * c_rows).astype(BF16),
                               gather(v_ref, res, k0, 2 * c_rows).astype(BF16),
                               bias_ref[2 * pattern + bq - kb], old))
            results = []
            for res, q0, q, k, v, bias, old in loaded:
                o0, m0 = one_head(q, k, v, bias, head0, kv_head0)
                o1, m1 = one_head(q, k, v, bias, jnp.logical_not(head0),
                                  jnp.logical_not(kv_head0))
                o_new = jnp.where(head0, o0, o1)
                m_new = jnp.where(head0, m0, m1)
                l_new = pltpu.roll(jnp.where(head0, o1, o0), HEAD_DIM, 1)
                if not first:
                    o_old, m_old, l_old = old
                    m_tot = jnp.maximum(m_old, m_new)
                    w_old = jnp.exp2(m_old - m_tot)
                    w_new = jnp.exp2(m_new - m_tot)
                    o_new = w_old * o_old + w_new * o_new
                    l_new = w_old * l_old + w_new * l_new
                    m_new = m_tot
                results.append((res, q0, o_new, m_new, l_new))
            for res, q0, o_new, m_new, l_new in results:
                if last:
                    scatter(o_ref, res, q0, o_new / l_new)
                else:
                    scatter(o_ref, res, q0, o_new)
                    scatter(m_ref, res, q0, m_new)
                    scatter(l_ref, res, q0, l_new)
            return carry

        lax.fori_loop(0, n_units // ATTN_UNROLL, group, 0)


def _attention(z, n_pairs, *, q_col, k_col, v_col):
    b, n_res, length, _ = z.shape
    assert n_res == TOKEN_RES == DILATIONS[-1] and all(n_res % d == 0 for d in DILATIONS)
    assert length % ATTN_BLOCK == 0 and length >= 2 * ATTN_BLOCK and ATTN_BLOCK // n_res >= 8
    assert (n_res * length // ATTN_BLOCK) % ATTN_UNROLL == 0

    def spec(col):
        return pl.BlockSpec((None, n_res, length, LANES), lambda bi, h: (bi, 0, 0, col + h))

    slab = n_res * length * LANES * 4
    vmem = 10 * slab * 1.15 + (8 << 20)
    return pl.pallas_call(
        _attn_kernel,
        grid=(b, n_pairs),
        in_specs=[spec(q_col), spec(k_col), spec(v_col)],
        out_specs=spec(0),
        out_shape=jax.ShapeDtypeStruct((b, n_res, length, n_pairs * LANES), F32),
        scratch_shapes=[pltpu.VMEM((n_res, length, LANES), F32),
                        pltpu.VMEM((n_res, length, LANES), F32),
                        pltpu.VMEM((2 * len(DILATIONS), ATTN_BLOCK, 2 * ATTN_BLOCK), F32)],
        compiler_params=_params(("parallel", "parallel"), vmem),
        name="attention",
    )(z, z, z)


def _ssm_weights(lam_re, lam_im, log_dt, b_re, b_im, c_re, c_im, d_skip, n_chunks):
    hp = lax.Precision.HIGHEST
    g, p = lam_re.shape
    c = b_re.shape[-1]
    t = SSM_CHUNK
    gb = GROUPS_PER_BLOCK
    nj = g // gb
    lr, li = lam_re.astype(F32), lam_im.astype(F32)
    dt = jnp.exp(log_dt.astype(F32))[:, None]
    mag = jnp.exp(lr * dt)
    ar = mag * jnp.cos(li * dt)
    ai = mag * jnp.sin(li * dt)
    nr, ni = ar - 1.0, ai
    den = lr * lr + li * li
    cr = (nr * lr + ni * li) / den
    ci = (ni * lr - nr * li) / den
    br, bi = b_re.astype(F32), b_im.astype(F32)
    bbr = cr[..., None] * br - ci[..., None] * bi
    bbi = cr[..., None] * bi + ci[..., None] * br

    def power(k):
        kk = k.astype(F32)[:, None, None]
        pm = jnp.exp(kk * (lr * dt))
        ang = kk * (li * dt)
        return pm * jnp.cos(ang), pm * jnp.sin(ang)

    pr, pi = power(jnp.arange(t + 1))
    cre, cim = c_re.astype(F32), c_im.astype(F32)
    car = cre[None] * pr[:, :, None, :] - cim[None] * pi[:, :, None, :]
    cai = cre[None] * pi[:, :, None, :] + cim[None] * pr[:, :, None, :]
    kern = (jnp.einsum('kgdp,gpc->kgdc', car[:t], bbr, precision=hp)
            - jnp.einsum('kgdp,gpc->kgdc', cai[:t], bbi, precision=hp))
    kern = kern.at[0].add(d_skip.astype(F32).reshape(g, c)[:, :, None] * jnp.eye(c, dtype=F32))

    kr = kern.reshape(t, nj, gb, c, c).transpose(1, 0, 4, 2, 3).reshape(nj, t, c, gb * c)

    rev_r, rev_i = pr[:t][::-1], pi[:t][::-1]
    abr = rev_r[..., None] * bbr[None] - rev_i[..., None] * bbi[None]
    abi = rev_r[..., None] * bbi[None] + rev_i[..., None] * bbr[None]

    def rows_form(x):
        x = x.reshape(t, nj, gb, p, c).transpose(1, 0, 2, 4, 3).reshape(nj, t * gb * c, p)
        return jnp.concatenate([x, x], axis=-1).astype(BF16)

    def cols_form(x):
        x = x.reshape(t, nj, gb, c, p).transpose(1, 4, 0, 2, 3).reshape(nj, p, t * gb * c)
        return jnp.concatenate([x, x], axis=1).astype(BF16)

    n_steps = max(1, (n_chunks - 1).bit_length())
    sr, si = power(t * (2 ** jnp.arange(n_steps)))
    rows = -(-n_steps // 8) * 8
    sr = jnp.pad(sr.reshape(n_steps, nj, gb * p).transpose(1, 0, 2), ((0, 0), (0, rows - n_steps), (0, 0)))
    si = jnp.pad(si.reshape(n_steps, nj, gb * p).transpose(1, 0, 2), ((0, 0), (0, rows - n_steps), (0, 0)))
    return (kr.astype(BF16), rows_form(abr), rows_form(abi),
            cols_form(car[1:]), cols_form(-cai[1:]), sr, si)


def _ssm_expand(kr_ref, abr_ref, abi_ref, cmr_ref, cmi_ref, m_sc, b_sc, c_sc):
    t, c, gb, p = SSM_CHUNK, SSM_GROUP, GROUPS_PER_BLOCK, SSM_STATE
    shift = c.bit_length() - 1
    half = gb * p
    pairs = LANES // p

    def group_of(idx):
        return (idx >> shift) & (gb - 1)

    m_sc[...] = jnp.zeros_like(m_sc)
    col_g = group_of(lax.broadcasted_iota(jnp.int32, (c, LANES), 1))
    for k in range(t):
        kk = kr_ref[k]
        dk = jnp.concatenate([jnp.where(col_g == g, kk, jnp.zeros_like(kk)) for g in range(gb)], axis=0)
        for tau in range(t - k):
            m_sc[tau * LANES:(tau + 1) * LANES, (tau + k) * LANES:(tau + k + 1) * LANES] = dk

    row_g = group_of(lax.broadcasted_iota(jnp.int32, (t * LANES, LANES), 0))
    lane_half = lax.broadcasted_iota(jnp.int32, (t * LANES, LANES), 1) >> (p.bit_length() - 1)
    for part, ab_ref in enumerate((abr_ref, abi_ref)):
        ab = ab_ref[...]
        for q in range(gb // pairs):
            b_sc[:, part * half + q * LANES:part * half + (q + 1) * LANES] = jnp.where(
                row_g == pairs * q + lane_half, ab, jnp.zeros_like(ab))

    col_g2 = group_of(lax.broadcasted_iota(jnp.int32, (LANES, t * LANES), 1))
    row_half = lax.broadcasted_iota(jnp.int32, (LANES, t * LANES), 0) >> (p.bit_length() - 1)
    for part, cm_ref in enumerate((cmr_ref, cmi_ref)):
        cm = cm_ref[...]
        for q in range(gb // pairs):
            c_sc[part * half + q * LANES:part * half + (q + 1) * LANES, :] = jnp.where(
                col_g2 == pairs * q + row_half, cm, jnp.zeros_like(cm))


def _ssm_kernel(s_ref, kr_ref, abr_ref, abi_ref, cmr_ref, cmi_ref, sr_ref, si_ref, y_ref,
                m_sc, b_sc, c_sc):
    t, n_chunks, _ = s_ref.shape
    half = GROUPS_PER_BLOCK * SSM_STATE

    @pl.when(pl.program_id(1) == 0)
    def _():
        _ssm_expand(kr_ref, abr_ref, abi_ref, cmr_ref, cmi_ref, m_sc, b_sc, c_sc)

    u = jnp.concatenate([s_ref[tau].astype(BF16) for tau in range(t)], axis=1)
    y = _dot(u, m_sc[...])
    x = _dot(u, b_sc[...])
    hr, hi = x[:, :half], x[:, half:]

    row = lax.broadcasted_iota(jnp.int32, (n_chunks, half), 0)

    def shifted(h, k):
        return jnp.where(row >= k, pltpu.roll(h, k, 0), 0.0)

    step = 0
    while (1 << step) < n_chunks:
        k = 1 << step
        pr = sr_ref[step:step + 1, :]
        pi = si_ref[step:step + 1, :]
        zr, zi = shifted(hr, k), shifted(hi, k)
        hr, hi = hr + pr * zr - pi * zi, hi + pr * zi + pi * zr
        step += 1

    h_prev = jnp.concatenate([shifted(hr, 1), shifted(hi, 1)], axis=1).astype(BF16)
    y = jax.nn.gelu(y + _dot(h_prev, c_sc[...]))
    for tau in range(t):
        y_ref[tau] = y[:, tau * LANES:(tau + 1) * LANES]


def _ssm(z, s_col, weights):
    b, n_res, n_chunks, _ = z.shape
    nj = weights[0].shape[0]
    assert n_res == SSM_CHUNK
    assert SSM_GROUP & (SSM_GROUP - 1) == 0 and SSM_STATE & (SSM_STATE - 1) == 0

    def wspec(a):
        nd = a.ndim - 1
        return pl.BlockSpec((None,) + a.shape[1:], lambda j, bi: (j,) + (0,) * nd)

    s = n_res * n_chunks
    width = SSM_CHUNK * LANES
    state = 2 * GROUPS_PER_BLOCK * SSM_STATE
    scratch = (width * width + 2 * width * state) * 2
    wbytes = 2 * sum(math.prod(a.shape[1:]) * a.dtype.itemsize for a in weights)
    vmem = (4 * s * LANES * 4 + wbytes + scratch + n_chunks * (width * 10 + state * 12)) * 1.2
    return pl.pallas_call(
        _ssm_kernel,
        grid=(nj, b),
        in_specs=[pl.BlockSpec((None, n_res, n_chunks, LANES),
                               lambda j, bi: (bi, 0, 0, s_col + j))]
        + [wspec(a) for a in weights],
        out_specs=pl.BlockSpec((None, n_res, n_chunks, LANES), lambda j, bi: (bi, 0, 0, j)),
        out_shape=jax.ShapeDtypeStruct((b, n_res, n_chunks, nj * LANES), F32),
        scratch_shapes=[pltpu.VMEM((width, width), BF16), pltpu.VMEM((width, state), BF16),
                        pltpu.VMEM((state, width), BF16)],
        compiler_params=_params(("parallel", "arbitrary"), vmem),
        name="ssm",
    )(z, *weights)


def _mix_out_kernel(ya_ref, y_ref, h_ref, ga_ref, gb_ref, wglu_ref, bglu_ref,
                    wa_ref, wb_ref, o_ref):
    n_res, tl, _ = ya_ref.shape
    y = y_ref[...].reshape(n_res * tl, -1)
    gate = _dot(y.astype(BF16), wglu_ref[...]) + bglu_ref[...]
    yb = y * jax.nn.sigmoid(gate)
    na = _rms(ya_ref[...].reshape(n_res * tl, -1), ga_ref[...]).astype(BF16)
    nb = _rms(yb, gb_ref[...]).astype(BF16)
    res = _dot(na, wa_ref[...]) + _dot(nb, wb_ref[...])
    d = res.shape[1]
    for r in range(n_res):
        cols = slice(r * d, (r + 1) * d)
        o_ref[:, cols] = h_ref[:, cols] + res[r * tl:(r + 1) * tl]


def _mix_out(ya, y, h, ga, gb, wglu, bglu, wa, wb, *, tm=512):
    m, d = h.shape
    batch, n_res, length, wa_w = ya.shape
    wb_w = y.shape[-1]
    tm = min(tm, n_res * length)
    tl = tm // n_res
    tiles = length // tl
    assert tl % 8 == 0 and length % tl == 0
    vmem = (2 * tm * (wa_w + wb_w + 2 * d) * 4 + (wglu.size + wa.size + wb.size) * 2
            + 8 * tm * d * 4) * 1.2
    wide = pl.BlockSpec((tl, n_res * d), lambda i: (i, 0))
    res_major = lambda w: pl.BlockSpec((None, n_res, tl, w),
                                       lambda i: (i // tiles, 0, i % tiles, 0))
    out = pl.pallas_call(
        _mix_out_kernel,
        grid=(m // tm,),
        in_specs=[res_major(wa_w), res_major(wb_w), wide,
                  _const_spec((1, wa_w)), _const_spec((1, wb_w)),
                  _const_spec(wglu.shape), _const_spec((1, wb_w)),
                  _const_spec(wa.shape), _const_spec(wb.shape)],
        out_specs=wide,
        out_shape=jax.ShapeDtypeStruct((m // n_res, n_res * d), F32),
        compiler_params=_params(("parallel",), vmem),
        name="mix_out",
    )(ya, y, h.reshape(m // n_res, n_res * d), ga.reshape(1, -1), gb.reshape(1, -1),
      wglu, bglu.reshape(1, -1), wa, wb)
    return out.reshape(m, d)


def _ple_kernel(h_ref, p_ref, gn_ref, wg_ref, wp_ref, fn_ref, o_ref, *, final):
    h = h_ref[...]
    gate = jax.nn.sigmoid(_dot(_rms(h, gn_ref[...]).astype(BF16), wg_ref[...]))
    proj = _dot(p_ref[...].astype(BF16), wp_ref[...])
    out = h + gate * proj
    o_ref[...] = _rms(out, fn_ref[...]) if final else out


def _ple(h, p, gn, wg, wp, fn, *, final, tm=512):
    m, d = h.shape
    pd = p.shape[1]
    tm = min(tm, m)
    vmem = (2 * tm * (2 * d + pd) * 4 + (wg.size + wp.size) * 2 + 8 * tm * d * 4) * 1.2
    row = lambda w: pl.BlockSpec((tm, w), lambda i: (i, 0))
    return pl.pallas_call(
        functools.partial(_ple_kernel, final=final),
        grid=(m // tm,),
        in_specs=[row(d), row(pd), _const_spec((1, d)), _const_spec(wg.shape),
                  _const_spec(wp.shape), _const_spec((1, d))],
        out_specs=row(d),
        out_shape=jax.ShapeDtypeStruct((m, d), F32),
        compiler_params=_params(("parallel",), vmem),
        name="ple",
    )(h, p, gn.reshape(1, d), wg, wp, fn.reshape(1, d))


def kernel(x, p, ffn1_norm, ffn1_w_gate, ffn1_w_up, ffn1_w_down, mix_norm, w_in, attn_out_norm, ssm_lambda_re, ssm_lambda_im, ssm_log_dt, ssm_b_re, ssm_b_im, ssm_c_re, ssm_c_im, ssm_d, ssm_w_glu, ssm_b_glu, ssm_out_norm, w_out, ffn2_norm, ffn2_w_gate, ffn2_w_up, ffn2_w_down, ple_norm, ple_w_gate, ple_w_proj, final_norm):
    b, s, d = x.shape
    depth = p.shape[0]
    attn_w = attn_out_norm.shape[1]
    ssm_w = ssm_out_norm.shape[1]
    n_pairs = attn_w // LANES
    h = x.reshape(b * s, d)
    for i in range(depth):
        h = _ffn(h, ffn1_norm[i], _pad_ff(ffn1_w_gate[i], 1), _pad_ff(ffn1_w_up[i], 1),
                 _pad_ff(ffn1_w_down[i], 0))
        z = _in_proj(h, mix_norm[i], w_in[i].astype(BF16), b)
        ya = _attention(z, n_pairs, q_col=0, k_col=n_pairs, v_col=2 * n_pairs)
        weights = _ssm_weights(ssm_lambda_re[i], ssm_lambda_im[i], ssm_log_dt[i],
                               ssm_b_re[i], ssm_b_im[i], ssm_c_re[i], ssm_c_im[i],
                               ssm_d[i], s // SSM_CHUNK)
        y = _ssm(z, 3 * n_pairs, weights)
        wo = w_out[i].astype(BF16)
        h = _mix_out(ya, y, h, attn_out_norm[i], ssm_out_norm[i], ssm_w_glu[i].astype(BF16),
                     ssm_b_glu[i], wo[:attn_w], wo[attn_w:])
        h = _ffn(h, ffn2_norm[i], _pad_ff(ffn2_w_gate[i], 1), _pad_ff(ffn2_w_up[i], 1),
                 _pad_ff(ffn2_w_down[i], 0))
        h = _ple(h, p[i].reshape(b * s, -1), ple_norm[i], ple_w_gate[i].astype(BF16),
                 ple_w_proj[i].astype(BF16), final_norm, final=i == depth - 1)
    return h.reshape(b, s, d)
```

```python
import functools
import math

import jax
import jax.numpy as jnp
from jax import lax
from jax.experimental import pallas as pl
from jax.experimental.pallas import tpu as pltpu

NORM_EPS = 1e-6
MASK_VALUE = -1e30
HEAD_DIM = 64
ATTN_BLOCK = 128
ATTN_SPAN = 128
DILATIONS = (1, 4, 16)
ATTN_UNROLL = 4
SSM_GROUP = 16
SSM_STATE = 64
SSM_CHUNK = 16
TOKEN_RES = 16
LANES = 128
GROUPS_PER_BLOCK = LANES // SSM_GROUP
FF_TILE = 512
VMEM_CAP = 56 * 1024 * 1024

BF16 = jnp.bfloat16
F32 = jnp.float32


def _params(sem, vmem_bytes):
    return pltpu.CompilerParams(
        dimension_semantics=sem, vmem_limit_bytes=min(int(vmem_bytes), VMEM_CAP))


def _rms(x, g):
    ms = jnp.mean(x * x, axis=-1, keepdims=True)
    return x * lax.rsqrt(ms + NORM_EPS) * g


def _dot(a, b):
    return jnp.dot(a, b, preferred_element_type=F32)


def _const_spec(shape):
    return pl.BlockSpec(shape, lambda *_: (0,) * len(shape),
                        pipeline_mode=pl.Buffered(1))


def _ffn_kernel(x_ref, g_ref, wg_ref, wu_ref, wd_ref, o_ref, xn_ref):
    f = pl.program_id(1)

    @pl.when(f == 0)
    def _():
        xn_ref[...] = _rms(x_ref[...], g_ref[...]).astype(BF16)
        o_ref[...] = jnp.zeros_like(o_ref)

    xn = xn_ref[...]
    gate = _dot(xn, wg_ref[...])
    up = _dot(xn, wu_ref[...])
    act = (gate * jax.nn.sigmoid(gate) * up).astype(BF16)
    o_ref[...] += _dot(act, wd_ref[...])

    @pl.when(f == pl.num_programs(1) - 1)
    def _():
        o_ref[...] = x_ref[...] + 0.5 * o_ref[...]


def _ffn(h, g, wg, wu, wd, *, tm=512, tf=FF_TILE):
    m, d = h.shape
    ff = wg.shape[1]
    tm = min(tm, m)
    vmem = (4 * tm * d * 4 + tm * d * 2 + 6 * d * tf * 2 + 6 * tm * tf * 4) * 1.25
    return pl.pallas_call(
        _ffn_kernel,
        grid=(m // tm, ff // tf),
        in_specs=[
            pl.BlockSpec((tm, d), lambda i, f: (i, 0)),
            pl.BlockSpec((1, d), lambda i, f: (0, 0)),
            pl.BlockSpec((d, tf), lambda i, f: (0, f)),
            pl.BlockSpec((d, tf), lambda i, f: (0, f)),
            pl.BlockSpec((tf, d), lambda i, f: (f, 0)),
        ],
        out_specs=pl.BlockSpec((tm, d), lambda i, f: (i, 0)),
        out_shape=jax.ShapeDtypeStruct((m, d), F32),
        scratch_shapes=[pltpu.VMEM((tm, d), BF16)],
        compiler_params=_params(("parallel", "arbitrary"), vmem),
        name="ffn",
    )(h, g.reshape(1, d), wg, wu, wd)


def _pad_ff(w, axis, tf=FF_TILE):
    ff = w.shape[axis]
    pad = (-ff) % tf
    widths = [(0, 0), (0, 0)]
    widths[axis] = (0, pad)
    return jnp.pad(w.astype(BF16), widths)


def _residue_perm(n_res, tl, transpose):
    tm = n_res * tl
    assert tl & (tl - 1) == 0
    a = lax.broadcasted_iota(jnp.int32, (tm, tm), 1 if transpose else 0)
    b = lax.broadcasted_iota(jnp.int32, (tm, tm), 0 if transpose else 1)
    hit = b == n_res * (a & (tl - 1)) + (a >> (tl.bit_length() - 1))
    return jnp.where(hit, 1.0, 0.0).astype(BF16)


def _in_proj_kernel(x_ref, g_ref, w_ref, o_ref, xn_ref):
    n_res, tl, _ = o_ref.shape

    @pl.when(pl.program_id(1) == 0)
    def _():
        xn = _rms(x_ref[...], g_ref[...]).astype(BF16)
        xn_ref[...] = _dot(_residue_perm(n_res, tl, False), xn).astype(BF16)

    res = _dot(xn_ref[...], w_ref[...])
    for r in range(n_res):
        o_ref[r] = res[r * tl:(r + 1) * tl]


def _in_proj(h, g, w, batch, *, tm=512, tn=2048):
    m, d = h.shape
    n = w.shape[1]
    seq = m // batch
    tm, tn = min(tm, seq), min(tn, n)
    tl = tm // TOKEN_RES
    tiles = seq // tm
    assert tl % 8 == 0 and seq % tm == 0
    vmem = (2 * tm * d * 4 + tm * d * 2 + 2 * d * tn * 2 + 4 * tm * tn * 4 + 4 * tm * tm) * 1.25
    return pl.pallas_call(
        _in_proj_kernel,
        grid=(m // tm, n // tn),
        in_specs=[
            pl.BlockSpec((tm, d), lambda i, j: (i, 0)),
            pl.BlockSpec((1, d), lambda i, j: (0, 0)),
            pl.BlockSpec((d, tn), lambda i, j: (0, j)),
        ],
        out_specs=pl.BlockSpec((None, TOKEN_RES, tl, tn),
                               lambda i, j: (i // tiles, 0, i % tiles, j)),
        out_shape=jax.ShapeDtypeStruct((batch, TOKEN_RES, seq // TOKEN_RES, n), F32),
        scratch_shapes=[pltpu.VMEM((tm, d), BF16)],
        compiler_params=_params(("parallel", "arbitrary"), vmem),
        name="in_proj",
    )(h, g.reshape(1, d), w)


def _attn_kernel(q_ref, k_ref, v_ref, o_ref, m_ref, l_ref, bias_ref, p_sc, mx_sc):
    n_res, length, _ = q_ref.shape
    blk = ATTN_BLOCK
    n_units = n_res * length // blk
    head0 = lax.broadcasted_iota(jnp.int32, (blk, LANES), 1) < HEAD_DIM
    kv_head0 = lax.broadcasted_iota(jnp.int32, (2 * blk, LANES), 1) < HEAD_DIM
    scale = HEAD_DIM ** -0.5 * math.log2(math.e)

    rho = lax.broadcasted_iota(jnp.int32, (blk, 2 * blk), 0)
    kap = lax.broadcasted_iota(jnp.int32, (blk, 2 * blk), 1)
    for pattern, d in enumerate(DILATIONS):
        n_sub = n_res // d
        c_rows = blk // n_sub
        sh = c_rows.bit_length() - 1
        iq, lq = rho >> sh, rho & (c_rows - 1)
        ik, lk = kap >> (sh + 1), kap & (2 * c_rows - 1)
        for shift in range(2):
            dist = n_sub * (lq - lk + shift * c_rows) + (iq - ik)
            bias_ref[2 * pattern + shift] = jnp.where(
                (dist >= 0) & (dist <= ATTN_SPAN), 0.0, MASK_VALUE)

    def probs(q, k, bias, sel):
        qh = jnp.where(sel, q, 0.0).astype(BF16)
        s = lax.dot_general(qh, k, (((1,), (1,)), ((), ())),
                            preferred_element_type=F32) + bias
        m = jnp.max(s, axis=-1, keepdims=True)
        return jnp.exp2(s - m).astype(BF16), m

    n_groups = n_units // ATTN_UNROLL
    for pattern, d in enumerate(DILATIONS):
        first = pattern == 0
        last = pattern == len(DILATIONS) - 1
        n_sub = n_res // d
        c_rows = blk // n_sub
        blocks_per_residue = n_units // d

        def gather(ref, res, start, size, d=d, n_sub=n_sub):
            return jnp.concatenate(
                [ref[d * i + res, pl.ds(start, size), :] for i in range(n_sub)], axis=0)

        def scatter(ref, res, start, val, d=d, n_sub=n_sub, c_rows=c_rows):
            for i in range(n_sub):
                ref[d * i + res, pl.ds(start, c_rows), :] = val[i * c_rows:(i + 1) * c_rows]

        def place(g, uu, c_rows=c_rows, blocks_per_residue=blocks_per_residue):
            u = g * ATTN_UNROLL + uu
            res = u // blocks_per_residue
            bq = u % blocks_per_residue
            kb = jnp.maximum(bq - 1, 0)
            return (res, pl.multiple_of(c_rows * bq, 8), pl.multiple_of(c_rows * kb, 8), bq - kb)

        def score_stage(g, slot, pattern=pattern, c_rows=c_rows, gather=gather, place=place):
            loaded = []
            for uu in range(ATTN_UNROLL):
                res, q0, k0, shift = place(g, uu)
                loaded.append((gather(q_ref, res, q0, c_rows) * scale,
                               gather(k_ref, res, k0, 2 * c_rows).astype(BF16),
                               bias_ref[2 * pattern + shift]))
            for uu, (q, k, bias) in enumerate(loaded):
                p0, m0 = probs(q, k, bias, head0)
                p1, m1 = probs(q, k, bias, jnp.logical_not(head0))
                p_sc[slot, uu, 0] = p0
                p_sc[slot, uu, 1] = p1
                mx_sc[slot, uu] = jnp.where(head0, m0, m1)

        def value_stage(g, slot, first=first, last=last, c_rows=c_rows,
                        gather=gather, scatter=scatter, place=place):
            loaded = []
            for uu in range(ATTN_UNROLL):
                res, q0, k0, _ = place(g, uu)
                old = None if first else tuple(
                    gather(ref, res, q0, c_rows) for ref in (o_ref, m_ref, l_ref))
                loaded.append((res, q0, gather(v_ref, res, k0, 2 * c_rows).astype(BF16),
                               p_sc[slot, uu, 0], p_sc[slot, uu, 1], mx_sc[slot, uu], old))
            results = []
            for res, q0, v, p0, p1, m_new, old in loaded:
                ones = jnp.ones_like(v)
                o0 = _dot(p0, jnp.where(kv_head0, v, ones))
                o1 = _dot(p1, jnp.where(kv_head0, ones, v))
                o_new = jnp.where(head0, o0, o1)
                l_new = pltpu.roll(jnp.where(head0, o1, o0), HEAD_DIM, 1)
                if not first:
                    o_old, m_old, l_old = old
                    m_tot = jnp.maximum(m_old, m_new)
                    w_old = jnp.exp2(m_old - m_tot)
                    w_new = jnp.exp2(m_new - m_tot)
                    o_new = w_old * o_old + w_new * o_new
                    l_new = w_old * l_old + w_new * l_new
                    m_new = m_tot
                results.append((res, q0, o_new, m_new, l_new))
            for res, q0, o_new, m_new, l_new in results:
                if last:
                    scatter(o_ref, res, q0, o_new / l_new)
                else:
                    scatter(o_ref, res, q0, o_new)
                    scatter(m_ref, res, q0, m_new)
                    scatter(l_ref, res, q0, l_new)

        def step(g, carry, score_stage=score_stage, value_stage=value_stage):
            slot = g & 1
            value_stage(g - 1, 1 - slot)
            score_stage(g, slot)
            return carry

        score_stage(0, 0)
        lax.fori_loop(1, n_groups, step, 0)
        value_stage(n_groups - 1, (n_groups - 1) & 1)


def _attention(z, n_pairs, *, q_col, k_col, v_col):
    b, n_res, length, _ = z.shape
    assert n_res == TOKEN_RES == DILATIONS[-1] and all(n_res % d == 0 for d in DILATIONS)
    assert length % ATTN_BLOCK == 0 and length >= 2 * ATTN_BLOCK and ATTN_BLOCK // n_res >= 8
    assert (n_res * length // ATTN_BLOCK) % ATTN_UNROLL == 0

    def spec(col):
        return pl.BlockSpec((None, n_res, length, LANES), lambda bi, h: (bi, 0, 0, col + h))

    slab = n_res * length * LANES * 4
    vmem = 10 * slab * 1.15 + (8 << 20)
    return pl.pallas_call(
        _attn_kernel,
        grid=(b, n_pairs),
        in_specs=[spec(q_col), spec(k_col), spec(v_col)],
        out_specs=spec(0),
        out_shape=jax.ShapeDtypeStruct((b, n_res, length, n_pairs * LANES), F32),
        scratch_shapes=[pltpu.VMEM((n_res, length, LANES), F32),
                        pltpu.VMEM((n_res, length, LANES), F32),
                        pltpu.VMEM((2 * len(DILATIONS), ATTN_BLOCK, 2 * ATTN_BLOCK), F32),
                        pltpu.VMEM((2, ATTN_UNROLL, 2, ATTN_BLOCK, 2 * ATTN_BLOCK), BF16),
                        pltpu.VMEM((2, ATTN_UNROLL, ATTN_BLOCK, LANES), F32)],
        compiler_params=_params(("parallel", "parallel"), vmem),
        name="attention",
    )(z, z, z)


def _ssm_weights(lam_re, lam_im, log_dt, b_re, b_im, c_re, c_im, d_skip, n_chunks):
    hp = lax.Precision.HIGHEST
    g, p = lam_re.shape
    c = b_re.shape[-1]
    t = SSM_CHUNK
    gb = GROUPS_PER_BLOCK
    nj = g // gb
    lr, li = lam_re.astype(F32), lam_im.astype(F32)
    dt = jnp.exp(log_dt.astype(F32))[:, None]
    mag = jnp.exp(lr * dt)
    ar = mag * jnp.cos(li * dt)
    ai = mag * jnp.sin(li * dt)
    nr, ni = ar - 1.0, ai
    den = lr * lr + li * li
    cr = (nr * lr + ni * li) / den
    ci = (ni * lr - nr * li) / den
    br, bi = b_re.astype(F32), b_im.astype(F32)
    bbr = cr[..., None] * br - ci[..., None] * bi
    bbi = cr[..., None] * bi + ci[..., None] * br

    def power(k):
        kk = k.astype(F32)[:, None, None]
        pm = jnp.exp(kk * (lr * dt))
        ang = kk * (li * dt)
        return pm * jnp.cos(ang), pm * jnp.sin(ang)

    pr, pi = power(jnp.arange(t + 1))
    cre, cim = c_re.astype(F32), c_im.astype(F32)
    car = cre[None] * pr[:, :, None, :] - cim[None] * pi[:, :, None, :]
    cai = cre[None] * pi[:, :, None, :] + cim[None] * pr[:, :, None, :]
    kern = (jnp.einsum('kgdp,gpc->kgdc', car[:t], bbr, precision=hp)
            - jnp.einsum('kgdp,gpc->kgdc', cai[:t], bbi, precision=hp))
    kern = kern.at[0].add(d_skip.astype(F32).reshape(g, c)[:, :, None] * jnp.eye(c, dtype=F32))

    kr = kern.reshape(t, nj, gb, c, c).transpose(1, 0, 4, 2, 3).reshape(nj, t, c, gb * c)

    rev_r, rev_i = pr[:t][::-1], pi[:t][::-1]
    abr = rev_r[..., None] * bbr[None] - rev_i[..., None] * bbi[None]
    abi = rev_r[..., None] * bbi[None] + rev_i[..., None] * bbr[None]

    def rows_form(x):
        x = x.reshape(t, nj, gb, p, c).transpose(1, 0, 2, 4, 3).reshape(nj, t * gb * c, p)
        return jnp.concatenate([x, x], axis=-1).astype(BF16)

    def cols_form(x):
        x = x.reshape(t, nj, gb, c, p).transpose(1, 4, 0, 2, 3).reshape(nj, p, t * gb * c)
        return jnp.concatenate([x, x], axis=1).astype(BF16)

    n_steps = max(1, (n_chunks - 1).bit_length())
    sr, si = power(t * (2 ** jnp.arange(n_steps)))
    rows = -(-n_steps // 8) * 8
    sr = jnp.pad(sr.reshape(n_steps, nj, gb * p).transpose(1, 0, 2), ((0, 0), (0, rows - n_steps), (0, 0)))
    si = jnp.pad(si.reshape(n_steps, nj, gb * p).transpose(1, 0, 2), ((0, 0), (0, rows - n_steps), (0, 0)))
    return (kr.astype(BF16), rows_form(abr), rows_form(abi),
            cols_form(car[1:]), cols_form(-cai[1:]), sr, si)


def _ssm_expand(kr_ref, abr_ref, abi_ref, cmr_ref, cmi_ref, m_sc, b_sc, c_sc):
    t, c, gb, p = SSM_CHUNK, SSM_GROUP, GROUPS_PER_BLOCK, SSM_STATE
    shift = c.bit_length() - 1
    half = gb * p
    pairs = LANES // p

    def group_of(idx):
        return (idx >> shift) & (gb - 1)

    m_sc[...] = jnp.zeros_like(m_sc)
    col_g = group_of(lax.broadcasted_iota(jnp.int32, (c, LANES), 1))
    for k in range(t):
        kk = kr_ref[k]
        dk = jnp.concatenate([jnp.where(col_g == g, kk, jnp.zeros_like(kk)) for g in range(gb)], axis=0)
        for tau in range(t - k):
            m_sc[tau * LANES:(tau + 1) * LANES, (tau + k) * LANES:(tau + k + 1) * LANES] = dk

    row_g = group_of(lax.broadcasted_iota(jnp.int32, (t * LANES, LANES), 0))
    lane_half = lax.broadcasted_iota(jnp.int32, (t * LANES, LANES), 1) >> (p.bit_length() - 1)
    for part, ab_ref in enumerate((abr_ref, abi_ref)):
        ab = ab_ref[...]
        for q in range(gb // pairs):
            b_sc[:, part * half + q * LANES:part * half + (q + 1) * LANES] = jnp.where(
                row_g == pairs * q + lane_half, ab, jnp.zeros_like(ab))

    col_g2 = group_of(lax.broadcasted_iota(jnp.int32, (LANES, t * LANES), 1))
    row_half = lax.broadcasted_iota(jnp.int32, (LANES, t * LANES), 0) >> (p.bit_length() - 1)
    for part, cm_ref in enumerate((cmr_ref, cmi_ref)):
        cm = cm_ref[...]
        for q in range(gb // pairs):
            c_sc[part * half + q * LANES:part * half + (q + 1) * LANES, :] = jnp.where(
                col_g2 == pairs * q + row_half, cm, jnp.zeros_like(cm))


def _ssm_kernel(s_ref, kr_ref, abr_ref, abi_ref, cmr_ref, cmi_ref, sr_ref, si_ref, y_ref,
                m_sc, b_sc, c_sc):
    t, n_chunks, _ = s_ref.shape
    half = GROUPS_PER_BLOCK * SSM_STATE

    @pl.when(pl.program_id(1) == 0)
    def _():
        _ssm_expand(kr_ref, abr_ref, abi_ref, cmr_ref, cmi_ref, m_sc, b_sc, c_sc)

    u = jnp.concatenate([s_ref[tau].astype(BF16) for tau in range(t)], axis=1)
    y = _dot(u, m_sc[...])
    x = _dot(u, b_sc[...])
    hr, hi = x[:, :half], x[:, half:]

    row = lax.broadcasted_iota(jnp.int32, (n_chunks, half), 0)

    def shifted(h, k):
        return jnp.where(row >= k, pltpu.roll(h, k, 0), 0.0)

    step = 0
    while (1 << step) < n_chunks:
        k = 1 << step
        pr = sr_ref[step:step + 1, :]
        pi = si_ref[step:step + 1, :]
        zr, zi = shifted(hr, k), shifted(hi, k)
        hr, hi = hr + pr * zr - pi * zi, hi + pr * zi + pi * zr
        step += 1

    h_prev = jnp.concatenate([shifted(hr, 1), shifted(hi, 1)], axis=1).astype(BF16)
    y = jax.nn.gelu(y + _dot(h_prev, c_sc[...]))
    for tau in range(t):
        y_ref[tau] = y[:, tau * LANES:(tau + 1) * LANES]


def _ssm(z, s_col, weights):
    b, n_res, n_chunks, _ = z.shape
    nj = weights[0].shape[0]
    assert n_res == SSM_CHUNK
    assert SSM_GROUP & (SSM_GROUP - 1) == 0 and SSM_STATE & (SSM_STATE - 1) == 0

    def wspec(a):
        nd = a.ndim - 1
        return pl.BlockSpec((None,) + a.shape[1:], lambda j, bi: (j,) + (0,) * nd)

    s = n_res * n_chunks
    width = SSM_CHUNK * LANES
    state = 2 * GROUPS_PER_BLOCK * SSM_STATE
    scratch = (width * width + 2 * width * state) * 2
    wbytes = 2 * sum(math.prod(a.shape[1:]) * a.dtype.itemsize for a in weights)
    vmem = (4 * s * LANES * 4 + wbytes + scratch + n_chunks * (width * 10 + state * 12)) * 1.2
    return pl.pallas_call(
        _ssm_kernel,
        grid=(nj, b),
        in_specs=[pl.BlockSpec((None, n_res, n_chunks, LANES),
                               lambda j, bi: (bi, 0, 0, s_col + j))]
        + [wspec(a) for a in weights],
        out_specs=pl.BlockSpec((None, n_res, n_chunks, LANES), lambda j, bi: (bi, 0, 0, j)),
        out_shape=jax.ShapeDtypeStruct((b, n_res, n_chunks, nj * LANES), F32),
        scratch_shapes=[pltpu.VMEM((width, width), BF16), pltpu.VMEM((width, state), BF16),
                        pltpu.VMEM((state, width), BF16)],
        compiler_params=_params(("parallel", "arbitrary"), vmem),
        name="ssm",
    )(z, *weights)


def _mix_out_kernel(ya_ref, y_ref, h_ref, ga_ref, gb_ref, wglu_ref, bglu_ref,
                    wa_ref, wb_ref, o_ref):
    n_res, tl, _ = ya_ref.shape
    y = y_ref[...].reshape(n_res * tl, -1)
    gate = _dot(y.astype(BF16), wglu_ref[...]) + bglu_ref[...]
    yb = y * jax.nn.sigmoid(gate)
    na = _rms(ya_ref[...].reshape(n_res * tl, -1), ga_ref[...]).astype(BF16)
    nb = _rms(yb, gb_ref[...]).astype(BF16)
    perm = _residue_perm(n_res, tl, True)
    na = _dot(perm, na).astype(BF16)
    nb = _dot(perm, nb).astype(BF16)
    o_ref[...] = h_ref[...] + _dot(na, wa_ref[...]) + _dot(nb, wb_ref[...])


def _mix_out(ya, y, h, ga, gb, wglu, bglu, wa, wb, *, tm=512):
    m, d = h.shape
    batch, n_res, length, wa_w = ya.shape
    wb_w = y.shape[-1]
    tm = min(tm, n_res * length)
    tl = tm // n_res
    tiles = length // tl
    assert tl % 8 == 0 and length % tl == 0
    vmem = (2 * tm * (wa_w + wb_w + 2 * d) * 4 + (wglu.size + wa.size + wb.size) * 2
            + 8 * tm * d * 4) * 1.2
    row = pl.BlockSpec((tm, d), lambda i: (i, 0))
    res_major = lambda w: pl.BlockSpec((None, n_res, tl, w),
                                       lambda i: (i // tiles, 0, i % tiles, 0))
    return pl.pallas_call(
        _mix_out_kernel,
        grid=(m // tm,),
        in_specs=[res_major(wa_w), res_major(wb_w), row,
                  _const_spec((1, wa_w)), _const_spec((1, wb_w)),
                  _const_spec(wglu.shape), _const_spec((1, wb_w)),
                  _const_spec(wa.shape), _const_spec(wb.shape)],
        out_specs=row,
        out_shape=jax.ShapeDtypeStruct((m, d), F32),
        compiler_params=_params(("parallel",), vmem),
        name="mix_out",
    )(ya, y, h, ga.reshape(1, -1), gb.reshape(1, -1), wglu, bglu.reshape(1, -1), wa, wb)


def _ple_kernel(h_ref, p_ref, gn_ref, wg_ref, wp_ref, fn_ref, o_ref, *, final):
    h = h_ref[...]
    gate = jax.nn.sigmoid(_dot(_rms(h, gn_ref[...]).astype(BF16), wg_ref[...]))
    proj = _dot(p_ref[...].astype(BF16), wp_ref[...])
    out = h + gate * proj
    o_ref[...] = _rms(out, fn_ref[...]) if final else out


def _ple(h, p, gn, wg, wp, fn, *, final, tm=512):
    m, d = h.shape
    pd = p.shape[1]
    tm = min(tm, m)
    vmem = (2 * tm * (2 * d + pd) * 4 + (wg.size + wp.size) * 2 + 8 * tm * d * 4) * 1.2
    row = lambda w: pl.BlockSpec((tm, w), lambda i: (i, 0))
    return pl.pallas_call(
        functools.partial(_ple_kernel, final=final),
        grid=(m // tm,),
        in_specs=[row(d), row(pd), _const_spec((1, d)), _const_spec(wg.shape),
                  _const_spec(wp.shape), _const_spec((1, d))],
        out_specs=row(d),
        out_shape=jax.ShapeDtypeStruct((m, d), F32),
        compiler_params=_params(("parallel",), vmem),
        name="ple",
    )(h, p, gn.reshape(1, d), wg, wp, fn.reshape(1, d))


def kernel(x, p, ffn1_norm, ffn1_w_gate, ffn1_w_up, ffn1_w_down, mix_norm, w_in, attn_out_norm, ssm_lambda_re, ssm_lambda_im, ssm_log_dt, ssm_b_re, ssm_b_im, ssm_c_re, ssm_c_im, ssm_d, ssm_w_glu, ssm_b_glu, ssm_out_norm, w_out, ffn2_norm, ffn2_w_gate, ffn2_w_up, ffn2_w_down, ple_norm, ple_w_gate, ple_w_proj, final_norm):
    b, s, d = x.shape
    depth = p.shape[0]
    attn_w = attn_out_norm.shape[1]
    ssm_w = ssm_out_norm.shape[1]
    n_pairs = attn_w // LANES
    h = x.reshape(b * s, d)
    for i in range(depth):
        h = _ffn(h, ffn1_norm[i], _pad_ff(ffn1_w_gate[i], 1), _pad_ff(ffn1_w_up[i], 1),
                 _pad_ff(ffn1_w_down[i], 0))
        z = _in_proj(h, mix_norm[i], w_in[i].astype(BF16), b)
        ya = _attention(z, n_pairs, q_col=0, k_col=n_pairs, v_col=2 * n_pairs)
        weights = _ssm_weights(ssm_lambda_re[i], ssm_lambda_im[i], ssm_log_dt[i],
                               ssm_b_re[i], ssm_b_im[i], ssm_c_re[i], ssm_c_im[i],
                               ssm_d[i], s // SSM_CHUNK)
        y = _ssm(z, 3 * n_pairs, weights)
        wo = w_out[i].astype(BF16)
        h = _mix_out(ya, y, h, attn_out_norm[i], ssm_out_norm[i], ssm_w_glu[i].astype(BF16),
                     ssm_b_glu[i], wo[:attn_w], wo[attn_w:])
        h = _ffn(h, ffn2_norm[i], _pad_ff(ffn2_w_gate[i], 1), _pad_ff(ffn2_w_up[i], 1),
                 _pad_ff(ffn2_w_down[i], 0))
        h = _ple(h, p[i].reshape(b * s, -1), ple_norm[i], ple_w_gate[i].astype(BF16),
                 ple_w_proj[i].astype(BF16), final_norm, final=i == depth - 1)
    return h.reshape(b, s, d)
```

```python
import functools
import math

import jax
import jax.numpy as jnp
from jax import lax
from jax.experimental import pallas as pl
from jax.experimental.pallas import tpu as pltpu

NORM_EPS = 1e-6
MASK_VALUE = -1e30
HEAD_DIM = 64
ATTN_BLOCK = 128
ATTN_SPAN = 128
DILATIONS = (1, 4, 16)
ATTN_UNROLL = 8
SSM_GROUP = 16
SSM_STATE = 64
SSM_CHUNK = 16
TOKEN_RES = 16
LANES = 128
GROUPS_PER_BLOCK = LANES // SSM_GROUP
FF_TILE = 512
VMEM_CAP = 56 * 1024 * 1024

BF16 = jnp.bfloat16
F32 = jnp.float32


def _params(sem, vmem_bytes):
    return pltpu.CompilerParams(
        dimension_semantics=sem, vmem_limit_bytes=min(int(vmem_bytes), VMEM_CAP))


def _rms(x, g):
    ms = jnp.mean(x * x, axis=-1, keepdims=True)
    return x * lax.rsqrt(ms + NORM_EPS) * g


def _dot(a, b):
    return jnp.dot(a, b, preferred_element_type=F32)


def _const_spec(shape):
    return pl.BlockSpec(shape, lambda *_: (0,) * len(shape),
                        pipeline_mode=pl.Buffered(1))


def _ffn_kernel(x_ref, g_ref, wg_ref, wu_ref, wd_ref, o_ref, xn_ref):
    f = pl.program_id(1)

    @pl.when(f == 0)
    def _():
        xn_ref[...] = _rms(x_ref[...], g_ref[...]).astype(BF16)
        o_ref[...] = jnp.zeros_like(o_ref)

    xn = xn_ref[...]
    gate = _dot(xn, wg_ref[...])
    up = _dot(xn, wu_ref[...])
    act = (gate * jax.nn.sigmoid(gate) * up).astype(BF16)
    o_ref[...] += _dot(act, wd_ref[...])

    @pl.when(f == pl.num_programs(1) - 1)
    def _():
        o_ref[...] = x_ref[...] + 0.5 * o_ref[...]


def _ffn(h, g, wg, wu, wd, *, tm=512, tf=FF_TILE):
    m, d = h.shape
    ff = wg.shape[1]
    tm = min(tm, m)
    vmem = (4 * tm * d * 4 + tm * d * 2 + 6 * d * tf * 2 + 6 * tm * tf * 4) * 1.25
    return pl.pallas_call(
        _ffn_kernel,
        grid=(m // tm, ff // tf),
        in_specs=[
            pl.BlockSpec((tm, d), lambda i, f: (i, 0)),
            pl.BlockSpec((1, d), lambda i, f: (0, 0)),
            pl.BlockSpec((d, tf), lambda i, f: (0, f)),
            pl.BlockSpec((d, tf), lambda i, f: (0, f)),
            pl.BlockSpec((tf, d), lambda i, f: (f, 0)),
        ],
        out_specs=pl.BlockSpec((tm, d), lambda i, f: (i, 0)),
        out_shape=jax.ShapeDtypeStruct((m, d), F32),
        scratch_shapes=[pltpu.VMEM((tm, d), BF16)],
        compiler_params=_params(("parallel", "arbitrary"), vmem),
        name="ffn",
    )(h, g.reshape(1, d), wg, wu, wd)


def _pad_ff(w, axis, tf=FF_TILE):
    ff = w.shape[axis]
    pad = (-ff) % tf
    w = w.astype(BF16)
    if not pad:
        return w
    shape = list(w.shape)
    shape[axis] = pad
    return jnp.concatenate([w, jnp.zeros(shape, BF16)], axis=axis)


def _residue_perm(n_res, tl, transpose):
    tm = n_res * tl
    assert tl & (tl - 1) == 0
    a = lax.broadcasted_iota(jnp.int32, (tm, tm), 1 if transpose else 0)
    b = lax.broadcasted_iota(jnp.int32, (tm, tm), 0 if transpose else 1)
    hit = b == n_res * (a & (tl - 1)) + (a >> (tl.bit_length() - 1))
    return jnp.where(hit, 1.0, 0.0).astype(BF16)


def _in_proj_kernel(x_ref, g_ref, w_ref, o_ref):
    n_res, tl, _ = o_ref.shape
    xn = _rms(x_ref[...], g_ref[...]).astype(BF16)
    xn = _dot(_residue_perm(n_res, tl, False), xn).astype(BF16)
    res = _dot(xn, w_ref[...])
    for r in range(n_res):
        o_ref[r] = res[r * tl:(r + 1) * tl]


def _in_proj(h, g, w, batch, *, tm=256):
    m, d = h.shape
    n = w.shape[1]
    seq = m // batch
    tm = min(tm, seq)
    tl = tm // TOKEN_RES
    tiles = seq // tm
    assert tl % 8 == 0 and seq % tm == 0
    vmem = (2 * tm * d * 4 + tm * d * 6 + d * n * 2 + 4 * tm * n * 4 + 4 * tm * tm) * 1.2
    return pl.pallas_call(
        _in_proj_kernel,
        grid=(m // tm,),
        in_specs=[
            pl.BlockSpec((tm, d), lambda i: (i, 0)),
            _const_spec((1, d)),
            _const_spec((d, n)),
        ],
        out_specs=pl.BlockSpec((None, TOKEN_RES, tl, n),
                               lambda i: (i // tiles, 0, i % tiles, 0)),
        out_shape=jax.ShapeDtypeStruct((batch, TOKEN_RES, seq // TOKEN_RES, n), F32),
        compiler_params=_params(("parallel",), vmem),
        name="in_proj",
    )(h, g.reshape(1, d), w)


def _attn_kernel(q_ref, k_ref, v_ref, o_ref, m_ref, l_ref, bias_ref, p_sc, mx_sc):
    n_res, length, _ = q_ref.shape
    blk = ATTN_BLOCK
    n_units = n_res * length // blk
    head0 = lax.broadcasted_iota(jnp.int32, (blk, LANES), 1) < HEAD_DIM
    kv_head0 = lax.broadcasted_iota(jnp.int32, (2 * blk, LANES), 1) < HEAD_DIM
    scale = HEAD_DIM ** -0.5 * math.log2(math.e)

    rho = lax.broadcasted_iota(jnp.int32, (blk, 2 * blk), 0)
    kap = lax.broadcasted_iota(jnp.int32, (blk, 2 * blk), 1)
    for pattern, d in enumerate(DILATIONS):
        n_sub = n_res // d
        c_rows = blk // n_sub
        sh = c_rows.bit_length() - 1
        iq, lq = rho >> sh, rho & (c_rows - 1)
        ik, lk = kap >> (sh + 1), kap & (2 * c_rows - 1)
        for shift in range(2):
            dist = n_sub * (lq - lk + shift * c_rows) + (iq - ik)
            bias_ref[2 * pattern + shift] = jnp.where(
                (dist >= 0) & (dist <= ATTN_SPAN), 0.0, MASK_VALUE)

    def probs(q, k, bias, sel):
        qh = jnp.where(sel, q, 0.0).astype(BF16)
        s = lax.dot_general(qh, k, (((1,), (1,)), ((), ())),
                            preferred_element_type=F32) + bias
        m = jnp.max(s, axis=-1, keepdims=True)
        return jnp.exp2(s - m).astype(BF16), m

    n_groups = n_units // ATTN_UNROLL
    for pattern, d in enumerate(DILATIONS):
        first = pattern == 0
        last = pattern == len(DILATIONS) - 1
        n_sub = n_res // d
        c_rows = blk // n_sub
        blocks_per_residue = n_units // d

        def gather(ref, res, start, size, d=d, n_sub=n_sub):
            return jnp.concatenate(
                [ref[d * i + res, pl.ds(start, size), :] for i in range(n_sub)], axis=0)

        def scatter(ref, res, start, val, d=d, n_sub=n_sub, c_rows=c_rows):
            for i in range(n_sub):
                ref[d * i + res, pl.ds(start, c_rows), :] = val[i * c_rows:(i + 1) * c_rows]

        def place(g, uu, c_rows=c_rows, blocks_per_residue=blocks_per_residue):
            u = g * ATTN_UNROLL + uu
            res = u // blocks_per_residue
            bq = u % blocks_per_residue
            kb = jnp.maximum(bq - 1, 0)
            return (res, pl.multiple_of(c_rows * bq, 8), pl.multiple_of(c_rows * kb, 8), bq - kb)

        def score_stage(g, slot, pattern=pattern, c_rows=c_rows, gather=gather, place=place):
            loaded = []
            for uu in range(ATTN_UNROLL):
                res, q0, k0, shift = place(g, uu)
                loaded.append((gather(q_ref, res, q0, c_rows) * scale,
                               gather(k_ref, res, k0, 2 * c_rows).astype(BF16),
                               bias_ref[2 * pattern + shift]))
            for uu, (q, k, bias) in enumerate(loaded):
                p0, m0 = probs(q, k, bias, head0)
                p1, m1 = probs(q, k, bias, jnp.logical_not(head0))
                p_sc[slot, uu, 0] = p0
                p_sc[slot, uu, 1] = p1
                mx_sc[slot, uu] = jnp.where(head0, m0, m1)

        def value_stage(g, slot, first=first, last=last, c_rows=c_rows,
                        gather=gather, scatter=scatter, place=place):
            loaded = []
            for uu in range(ATTN_UNROLL):
                res, q0, k0, _ = place(g, uu)
                old = None if first else tuple(
                    gather(ref, res, q0, c_rows) for ref in (o_ref, m_ref, l_ref))
                loaded.append((res, q0, gather(v_ref, res, k0, 2 * c_rows).astype(BF16),
                               p_sc[slot, uu, 0], p_sc[slot, uu, 1], mx_sc[slot, uu], old))
            results = []
            for res, q0, v, p0, p1, m_new, old in loaded:
                ones = jnp.ones_like(v)
                o0 = _dot(p0, jnp.where(kv_head0, v, ones))
                o1 = _dot(p1, jnp.where(kv_head0, ones, v))
                o_new = jnp.where(head0, o0, o1)
                l_new = pltpu.roll(jnp.where(head0, o1, o0), HEAD_DIM, 1)
                if not first:
                    o_old, m_old, l_old = old
                    m_tot = jnp.maximum(m_old, m_new)
                    w_old = jnp.exp2(m_old - m_tot)
                    w_new = jnp.exp2(m_new - m_tot)
                    o_new = w_old * o_old + w_new * o_new
                    l_new = w_old * l_old + w_new * l_new
                    m_new = m_tot
                results.append((res, q0, o_new, m_new, l_new))
            for res, q0, o_new, m_new, l_new in results:
                if last:
                    scatter(o_ref, res, q0, o_new / l_new)
                else:
                    scatter(o_ref, res, q0, o_new)
                    scatter(m_ref, res, q0, m_new)
                    scatter(l_ref, res, q0, l_new)

        def step(g, carry, score_stage=score_stage, value_stage=value_stage):
            slot = g & 1
            value_stage(g - 1, 1 - slot)
            score_stage(g, slot)
            return carry

        score_stage(0, 0)
        lax.fori_loop(1, n_groups, step, 0)
        value_stage(n_groups - 1, (n_groups - 1) & 1)


def _attention(z, n_pairs, *, q_col, k_col, v_col):
    b, n_res, length, _ = z.shape
    assert n_res == TOKEN_RES == DILATIONS[-1] and all(n_res % d == 0 for d in DILATIONS)
    assert length % ATTN_BLOCK == 0 and length >= 2 * ATTN_BLOCK and ATTN_BLOCK // n_res >= 8
    assert (n_res * length // ATTN_BLOCK) % ATTN_UNROLL == 0

    def spec(col):
        return pl.BlockSpec((None, n_res, length, LANES), lambda bi, h: (bi, 0, 0, col + h))

    slab = n_res * length * LANES * 4
    vmem = 10 * slab * 1.15 + (8 << 20)
    return pl.pallas_call(
        _attn_kernel,
        grid=(b, n_pairs),
        in_specs=[spec(q_col), spec(k_col), spec(v_col)],
        out_specs=spec(0),
        out_shape=jax.ShapeDtypeStruct((b, n_res, length, n_pairs * LANES), F32),
        scratch_shapes=[pltpu.VMEM((n_res, length, LANES), F32),
                        pltpu.VMEM((n_res, length, LANES), F32),
                        pltpu.VMEM((2 * len(DILATIONS), ATTN_BLOCK, 2 * ATTN_BLOCK), F32),
                        pltpu.VMEM((2, ATTN_UNROLL, 2, ATTN_BLOCK, 2 * ATTN_BLOCK), BF16),
                        pltpu.VMEM((2, ATTN_UNROLL, ATTN_BLOCK, LANES), F32)],
        compiler_params=_params(("parallel", "parallel"), vmem),
        name="attention",
    )(z, z, z)


def _ssm_weights(lam_re, lam_im, log_dt, b_re, b_im, c_re, c_im, d_skip, n_chunks):
    hp = lax.Precision.HIGHEST
    g, p = lam_re.shape
    c = b_re.shape[-1]
    t = SSM_CHUNK
    gb = GROUPS_PER_BLOCK
    nj = g // gb
    lr, li = lam_re.astype(F32), lam_im.astype(F32)
    dt = jnp.exp(log_dt.astype(F32))[:, None]
    mag = jnp.exp(lr * dt)
    ar = mag * jnp.cos(li * dt)
    ai = mag * jnp.sin(li * dt)
    nr, ni = ar - 1.0, ai
    den = lr * lr + li * li
    cr = (nr * lr + ni * li) / den
    ci = (ni * lr - nr * li) / den
    br, bi = b_re.astype(F32), b_im.astype(F32)
    bbr = cr[..., None] * br - ci[..., None] * bi
    bbi = cr[..., None] * bi + ci[..., None] * br

    def power(k):
        kk = k.astype(F32)[:, None, None]
        pm = jnp.exp(kk * (lr * dt))
        ang = kk * (li * dt)
        return pm * jnp.cos(ang), pm * jnp.sin(ang)

    pr, pi = power(jnp.arange(t + 1))
    cre, cim = c_re.astype(F32), c_im.astype(F32)
    car = cre[None] * pr[:, :, None, :] - cim[None] * pi[:, :, None, :]
    cai = cre[None] * pi[:, :, None, :] + cim[None] * pr[:, :, None, :]
    kern = (jnp.einsum('kgdp,gpc->kgdc', car[:t], bbr, precision=hp)
            - jnp.einsum('kgdp,gpc->kgdc', cai[:t], bbi, precision=hp))
    kern = kern.at[0].add(d_skip.astype(F32).reshape(g, c)[:, :, None] * jnp.eye(c, dtype=F32))

    kr = kern.reshape(t, nj, gb, c, c).transpose(1, 0, 4, 2, 3).reshape(nj, t, c, gb * c)

    rev_r, rev_i = pr[:t][::-1], pi[:t][::-1]
    abr = rev_r[..., None] * bbr[None] - rev_i[..., None] * bbi[None]
    abi = rev_r[..., None] * bbi[None] + rev_i[..., None] * bbr[None]

    def rows_form(x):
        x = x.reshape(t, nj, gb, p, c).transpose(1, 0, 2, 4, 3).reshape(nj, t * gb * c, p)
        return jnp.concatenate([x, x], axis=-1).astype(BF16)

    def cols_form(x):
        x = x.reshape(t, nj, gb, c, p).transpose(1, 4, 0, 2, 3).reshape(nj, p, t * gb * c)
        return jnp.concatenate([x, x], axis=1).astype(BF16)

    n_steps = max(1, (n_chunks - 1).bit_length())
    sr, si = power(t * (2 ** jnp.arange(n_steps)))
    rows = -(-n_steps // 8) * 8
    sr = jnp.pad(sr.reshape(n_steps, nj, gb * p).transpose(1, 0, 2), ((0, 0), (0, rows - n_steps), (0, 0)))
    si = jnp.pad(si.reshape(n_steps, nj, gb * p).transpose(1, 0, 2), ((0, 0), (0, rows - n_steps), (0, 0)))
    return (kr.astype(BF16), rows_form(abr), rows_form(abi),
            cols_form(car[1:]), cols_form(-cai[1:]), sr, si)


def _ssm_expand(kr_ref, abr_ref, abi_ref, cmr_ref, cmi_ref, m_sc, b_sc, c_sc):
    t, c, gb, p = SSM_CHUNK, SSM_GROUP, GROUPS_PER_BLOCK, SSM_STATE
    shift = c.bit_length() - 1
    half = gb * p
    pairs = LANES // p

    def group_of(idx):
        return (idx >> shift) & (gb - 1)

    m_sc[...] = jnp.zeros_like(m_sc)
    col_g = group_of(lax.broadcasted_iota(jnp.int32, (c, LANES), 1))
    for k in range(t):
        kk = kr_ref[k]
        dk = jnp.concatenate([jnp.where(col_g == g, kk, jnp.zeros_like(kk)) for g in range(gb)], axis=0)
        for tau in range(t - k):
            m_sc[tau * LANES:(tau + 1) * LANES, (tau + k) * LANES:(tau + k + 1) * LANES] = dk

    row_g = group_of(lax.broadcasted_iota(jnp.int32, (t * LANES, LANES), 0))
    lane_half = lax.broadcasted_iota(jnp.int32, (t * LANES, LANES), 1) >> (p.bit_length() - 1)
    for part, ab_ref in enumerate((abr_ref, abi_ref)):
        ab = ab_ref[...]
        for q in range(gb // pairs):
            b_sc[:, part * half + q * LANES:part * half + (q + 1) * LANES] = jnp.where(
                row_g == pairs * q + lane_half, ab, jnp.zeros_like(ab))

    col_g2 = group_of(lax.broadcasted_iota(jnp.int32, (LANES, t * LANES), 1))
    row_half = lax.broadcasted_iota(jnp.int32, (LANES, t * LANES), 0) >> (p.bit_length() - 1)
    for part, cm_ref in enumerate((cmr_ref, cmi_ref)):
        cm = cm_ref[...]
        for q in range(gb // pairs):
            c_sc[part * half + q * LANES:part * half + (q + 1) * LANES, :] = jnp.where(
                col_g2 == pairs * q + row_half, cm, jnp.zeros_like(cm))


def _ssm_kernel(s_ref, kr_ref, abr_ref, abi_ref, cmr_ref, cmi_ref, sr_ref, si_ref, y_ref,
                m_sc, b_sc, c_sc):
    t, n_chunks, _ = s_ref.shape
    half = GROUPS_PER_BLOCK * SSM_STATE

    @pl.when(pl.program_id(1) == 0)
    def _():
        _ssm_expand(kr_ref, abr_ref, abi_ref, cmr_ref, cmi_ref, m_sc, b_sc, c_sc)

    u = jnp.concatenate([s_ref[tau].astype(BF16) for tau in range(t)], axis=1)
    y = _dot(u, m_sc[...])
    x = _dot(u, b_sc[...])
    hr, hi = x[:, :half], x[:, half:]

    row = lax.broadcasted_iota(jnp.int32, (n_chunks, half), 0)

    def shifted(h, k):
        return jnp.where(row >= k, pltpu.roll(h, k, 0), 0.0)

    step = 0
    while (1 << step) < n_chunks:
        k = 1 << step
        pr = sr_ref[step:step + 1, :]
        pi = si_ref[step:step + 1, :]
        zr, zi = shifted(hr, k), shifted(hi, k)
        hr, hi = hr + pr * zr - pi * zi, hi + pr * zi + pi * zr
        step += 1

    h_prev = jnp.concatenate([shifted(hr, 1), shifted(hi, 1)], axis=1).astype(BF16)
    y = jax.nn.gelu(y + _dot(h_prev, c_sc[...]))
    for tau in range(t):
        y_ref[tau] = y[:, tau * LANES:(tau + 1) * LANES]


def _ssm(z, s_col, weights):
    b, n_res, n_chunks, _ = z.shape
    nj = weights[0].shape[0]
    assert n_res == SSM_CHUNK
    assert SSM_GROUP & (SSM_GROUP - 1) == 0 and SSM_STATE & (SSM_STATE - 1) == 0

    def wspec(a):
        nd = a.ndim - 1
        return pl.BlockSpec((None,) + a.shape[1:], lambda j, bi: (j,) + (0,) * nd)

    s = n_res * n_chunks
    width = SSM_CHUNK * LANES
    state = 2 * GROUPS_PER_BLOCK * SSM_STATE
    scratch = (width * width + 2 * width * state) * 2
    wbytes = 2 * sum(math.prod(a.shape[1:]) * a.dtype.itemsize for a in weights)
    vmem = (4 * s * LANES * 4 + wbytes + scratch + n_chunks * (width * 10 + state * 12)) * 1.2
    return pl.pallas_call(
        _ssm_kernel,
        grid=(nj, b),
        in_specs=[pl.BlockSpec((None, n_res, n_chunks, LANES),
                               lambda j, bi: (bi, 0, 0, s_col + j))]
        + [wspec(a) for a in weights],
        out_specs=pl.BlockSpec((None, n_res, n_chunks, LANES), lambda j, bi: (bi, 0, 0, j)),
        out_shape=jax.ShapeDtypeStruct((b, n_res, n_chunks, nj * LANES), F32),
        scratch_shapes=[pltpu.VMEM((width, width), BF16), pltpu.VMEM((width, state), BF16),
                        pltpu.VMEM((state, width), BF16)],
        compiler_params=_params(("parallel", "arbitrary"), vmem),
        name="ssm",
    )(z, *weights)


def _mix_out_kernel(ya_ref, y_ref, h_ref, ga_ref, gb_ref, wglu_ref, bglu_ref,
                    wa_ref, wb_ref, o_ref):
    n_res, tl, _ = ya_ref.shape
    y = y_ref[...].reshape(n_res * tl, -1)
    gate = _dot(y.astype(BF16), wglu_ref[...]) + bglu_ref[...]
    yb = y * jax.nn.sigmoid(gate)
    na = _rms(ya_ref[...].reshape(n_res * tl, -1), ga_ref[...]).astype(BF16)
    nb = _rms(yb, gb_ref[...]).astype(BF16)
    perm = _residue_perm(n_res, tl, True)
    na = _dot(perm, na).astype(BF16)
    nb = _dot(perm, nb).astype(BF16)
    o_ref[...] = h_ref[...] + _dot(na, wa_ref[...]) + _dot(nb, wb_ref[...])


def _mix_out(ya, y, h, ga, gb, wglu, bglu, wa, wb, *, tm=512):
    m, d = h.shape
    batch, n_res, length, wa_w = ya.shape
    wb_w = y.shape[-1]
    tm = min(tm, n_res * length)
    tl = tm // n_res
    tiles = length // tl
    assert tl % 8 == 0 and length % tl == 0
    vmem = (2 * tm * (wa_w + wb_w + 2 * d) * 4 + (wglu.size + wa.size + wb.size) * 2
            + 8 * tm * d * 4) * 1.2
    row = pl.BlockSpec((tm, d), lambda i: (i, 0))
    res_major = lambda w: pl.BlockSpec((None, n_res, tl, w),
                                       lambda i: (i // tiles, 0, i % tiles, 0))
    return pl.pallas_call(
        _mix_out_kernel,
        grid=(m // tm,),
        in_specs=[res_major(wa_w), res_major(wb_w), row,
                  _const_spec((1, wa_w)), _const_spec((1, wb_w)),
                  _const_spec(wglu.shape), _const_spec((1, wb_w)),
                  _const_spec(wa.shape), _const_spec(wb.shape)],
        out_specs=row,
        out_shape=jax.ShapeDtypeStruct((m, d), F32),
        compiler_params=_params(("parallel",), vmem),
        name="mix_out",
    )(ya, y, h, ga.reshape(1, -1), gb.reshape(1, -1), wglu, bglu.reshape(1, -1), wa, wb)


def _ple_kernel(h_ref, p_ref, gn_ref, wg_ref, wp_ref, fn_ref, o_ref, *, final):
    h = h_ref[...]
    gate = jax.nn.sigmoid(_dot(_rms(h, gn_ref[...]).astype(BF16), wg_ref[...]))
    proj = _dot(p_ref[...].astype(BF16), wp_ref[...])
    out = h + gate * proj
    o_ref[...] = _rms(out, fn_ref[...]) if final else out


def _ple(h, p, gn, wg, wp, fn, *, final, tm=512):
    m, d = h.shape
    pd = p.shape[1]
    tm = min(tm, m)
    vmem = (2 * tm * (2 * d + pd) * 4 + (wg.size + wp.size) * 2 + 8 * tm * d * 4) * 1.2
    row = lambda w: pl.BlockSpec((tm, w), lambda i: (i, 0))
    return pl.pallas_call(
        functools.partial(_ple_kernel, final=final),
        grid=(m // tm,),
        in_specs=[row(d), row(pd), _const_spec((1, d)), _const_spec(wg.shape),
                  _const_spec(wp.shape), _const_spec((1, d))],
        out_specs=row(d),
        out_shape=jax.ShapeDtypeStruct((m, d), F32),
        compiler_params=_params(("parallel",), vmem),
        name="ple",
    )(h, p, gn.reshape(1, d), wg, wp, fn.reshape(1, d))


def kernel(x, p, ffn1_norm, ffn1_w_gate, ffn1_w_up, ffn1_w_down, mix_norm, w_in, attn_out_norm, ssm_lambda_re, ssm_lambda_im, ssm_log_dt, ssm_b_re, ssm_b_im, ssm_c_re, ssm_c_im, ssm_d, ssm_w_glu, ssm_b_glu, ssm_out_norm, w_out, ffn2_norm, ffn2_w_gate, ffn2_w_up, ffn2_w_down, ple_norm, ple_w_gate, ple_w_proj, final_norm):
    b, s, d = x.shape
    depth = p.shape[0]
    attn_w = attn_out_norm.shape[1]
    ssm_w = ssm_out_norm.shape[1]
    n_pairs = attn_w // LANES
    h = x.reshape(b * s, d)
    for i in range(depth):
        h = _ffn(h, ffn1_norm[i], _pad_ff(ffn1_w_gate[i], 1), _pad_ff(ffn1_w_up[i], 1),
                 _pad_ff(ffn1_w_down[i], 0))
        z = _in_proj(h, mix_norm[i], w_in[i].astype(BF16), b)
        ya = _attention(z, n_pairs, q_col=0, k_col=n_pairs, v_col=2 * n_pairs)
        weights = _ssm_weights(ssm_lambda_re[i], ssm_lambda_im[i], ssm_log_dt[i],
                               ssm_b_re[i], ssm_b_im[i], ssm_c_re[i], ssm_c_im[i],
                               ssm_d[i], s // SSM_CHUNK)
        y = _ssm(z, 3 * n_pairs, weights)
        wo = w_out[i].astype(BF16)
        h = _mix_out(ya, y, h, attn_out_norm[i], ssm_out_norm[i], ssm_w_glu[i].astype(BF16),
                     ssm_b_glu[i], wo[:attn_w], wo[attn_w:])
        h = _ffn(h, ffn2_norm[i], _pad_ff(ffn2_w_gate[i], 1), _pad_ff(ffn2_w_up[i], 1),
                 _pad_ff(ffn2_w_down[i], 0))
        h = _ple(h, p[i].reshape(b * s, -1), ple_norm[i], ple_w_gate[i].astype(BF16),
                 ple_w_proj[i].astype(BF16), final_norm, final=i == depth - 1)
    return h.reshape(b, s, d)
```

```python
import functools
import math

import jax
import jax.numpy as jnp
from jax import lax
from jax.experimental import pallas as pl
from jax.experimental.pallas import tpu as pltpu

NORM_EPS = 1e-6
MASK_VALUE = -1e30
HEAD_DIM = 64
ATTN_BLOCK = 128
ATTN_SPAN = 128
DILATIONS = (1, 4, 16)
ATTN_UNROLL = 16
SSM_GROUP = 16
SSM_STATE = 64
SSM_CHUNK = 16
TOKEN_RES = 16
LANES = 128
GROUPS_PER_BLOCK = LANES // SSM_GROUP
FF_TILE = 512
VMEM_CAP = 56 * 1024 * 1024

BF16 = jnp.bfloat16
F32 = jnp.float32


def _params(sem, vmem_bytes):
    return pltpu.CompilerParams(
        dimension_semantics=sem, vmem_limit_bytes=min(int(vmem_bytes), VMEM_CAP))


def _rms(x, g):
    ms = jnp.mean(x * x, axis=-1, keepdims=True)
    return x * lax.rsqrt(ms + NORM_EPS) * g


def _dot(a, b):
    return jnp.dot(a, b, preferred_element_type=F32)


def _const_spec(shape):
    return pl.BlockSpec(shape, lambda *_: (0,) * len(shape),
                        pipeline_mode=pl.Buffered(1))


def _ffn_kernel(x_ref, g_ref, wg_ref, wu_ref, wd_ref, o_ref, xn_ref):
    f = pl.program_id(1)

    @pl.when(f == 0)
    def _():
        xn_ref[...] = _rms(x_ref[...], g_ref[...]).astype(BF16)
        o_ref[...] = jnp.zeros_like(o_ref)

    xn = xn_ref[...]
    gate = _dot(xn, wg_ref[...])
    up = _dot(xn, wu_ref[...])
    act = (gate * jax.nn.sigmoid(gate) * up).astype(BF16)
    o_ref[...] += _dot(act, wd_ref[...])

    @pl.when(f == pl.num_programs(1) - 1)
    def _():
        o_ref[...] = x_ref[...] + 0.5 * o_ref[...]


def _ffn(h, g, wg, wu, wd, *, tm=512):
    m, d = h.shape
    nf, _, tf = wg.shape
    tm = min(tm, m)
    vmem = (4 * tm * d * 4 + tm * d * 2 + 6 * d * tf * 2 + 6 * tm * tf * 4) * 1.25
    return pl.pallas_call(
        _ffn_kernel,
        grid=(m // tm, nf),
        in_specs=[
            pl.BlockSpec((tm, d), lambda i, f: (i, 0)),
            pl.BlockSpec((1, d), lambda i, f: (0, 0)),
            pl.BlockSpec((None, d, tf), lambda i, f: (f, 0, 0)),
            pl.BlockSpec((None, d, tf), lambda i, f: (f, 0, 0)),
            pl.BlockSpec((None, tf, d), lambda i, f: (f, 0, 0)),
        ],
        out_specs=pl.BlockSpec((tm, d), lambda i, f: (i, 0)),
        out_shape=jax.ShapeDtypeStruct((m, d), F32),
        scratch_shapes=[pltpu.VMEM((tm, d), BF16)],
        compiler_params=_params(("parallel", "arbitrary"), vmem),
        name="ffn",
    )(h, g.reshape(1, d), wg, wu, wd)


def _ff_tile_kernel(w_ref, o_ref, *, axis, size):
    w = w_ref[...]
    tile = w.shape[axis]
    pos = lax.broadcasted_iota(jnp.int32, w.shape, axis) + pl.program_id(0) * tile
    o_ref[...] = jnp.where(pos < size, w, 0.0).astype(o_ref.dtype)


def _ff_tiles(w, axis, tf=FF_TILE):
    size = w.shape[axis]
    nf = pl.cdiv(size, tf)
    block = list(w.shape)
    block[axis] = tf
    index = (lambda f: (0, f)) if axis == 1 else (lambda f: (f, 0))
    return pl.pallas_call(
        functools.partial(_ff_tile_kernel, axis=axis, size=size),
        grid=(nf,),
        in_specs=[pl.BlockSpec(tuple(block), index)],
        out_specs=pl.BlockSpec((None,) + tuple(block), lambda f: (f, 0, 0)),
        out_shape=jax.ShapeDtypeStruct((nf,) + tuple(block), BF16),
        compiler_params=_params(("parallel",), 6 * math.prod(block) * 4),
        name="ff_tiles",
    )(w)


def _residue_perm(n_res, tl, transpose):
    tm = n_res * tl
    assert tl & (tl - 1) == 0
    a = lax.broadcasted_iota(jnp.int32, (tm, tm), 1 if transpose else 0)
    b = lax.broadcasted_iota(jnp.int32, (tm, tm), 0 if transpose else 1)
    hit = b == n_res * (a & (tl - 1)) + (a >> (tl.bit_length() - 1))
    return jnp.where(hit, 1.0, 0.0).astype(BF16)


def _in_proj_kernel(x_ref, g_ref, w_ref, o_ref):
    n_res, tl, _ = o_ref.shape
    xn = _rms(x_ref[...], g_ref[...]).astype(BF16)
    xn = _dot(_residue_perm(n_res, tl, False), xn).astype(BF16)
    res = _dot(xn, w_ref[...])
    for r in range(n_res):
        o_ref[r] = res[r * tl:(r + 1) * tl]


def _in_proj(h, g, w, batch, *, tm=256):
    m, d = h.shape
    n = w.shape[1]
    seq = m // batch
    tm = min(tm, seq)
    tl = tm // TOKEN_RES
    tiles = seq // tm
    assert tl % 8 == 0 and seq % tm == 0
    vmem = (2 * tm * d * 4 + tm * d * 6 + d * n * 2 + 4 * tm * n * 4 + 4 * tm * tm) * 1.2
    return pl.pallas_call(
        _in_proj_kernel,
        grid=(m // tm,),
        in_specs=[
            pl.BlockSpec((tm, d), lambda i: (i, 0)),
            _const_spec((1, d)),
            _const_spec((d, n)),
        ],
        out_specs=pl.BlockSpec((None, TOKEN_RES, tl, n),
                               lambda i: (i // tiles, 0, i % tiles, 0)),
        out_shape=jax.ShapeDtypeStruct((batch, TOKEN_RES, seq // TOKEN_RES, n), F32),
        compiler_params=_params(("parallel",), vmem),
        name="in_proj",
    )(h, g.reshape(1, d), w)


def _attn_kernel(q_ref, k_ref, v_ref, o_ref, m_ref, l_ref, bias_ref, p_sc, mx_sc):
    n_res, length, _ = q_ref.shape
    blk = ATTN_BLOCK
    n_units = n_res * length // blk
    head0 = lax.broadcasted_iota(jnp.int32, (blk, LANES), 1) < HEAD_DIM
    kv_head0 = lax.broadcasted_iota(jnp.int32, (2 * blk, LANES), 1) < HEAD_DIM
    scale = HEAD_DIM ** -0.5 * math.log2(math.e)

    rho = lax.broadcasted_iota(jnp.int32, (blk, 2 * blk), 0)
    kap = lax.broadcasted_iota(jnp.int32, (blk, 2 * blk), 1)
    for pattern, d in enumerate(DILATIONS):
        n_sub = n_res // d
        c_rows = blk // n_sub
        sh = c_rows.bit_length() - 1
        iq, lq = rho >> sh, rho & (c_rows - 1)
        ik, lk = kap >> (sh + 1), kap & (2 * c_rows - 1)
        for shift in range(2):
            dist = n_sub * (lq - lk + shift * c_rows) + (iq - ik)
            bias_ref[2 * pattern + shift] = jnp.where(
                (dist >= 0) & (dist <= ATTN_SPAN), 0.0, MASK_VALUE)

    def probs(q, k, bias, sel):
        qh = jnp.where(sel, q, 0.0).astype(BF16)
        s = lax.dot_general(qh, k, (((1,), (1,)), ((), ())),
                            preferred_element_type=F32) + bias
        m = jnp.max(s, axis=-1, keepdims=True)
        return jnp.exp2(s - m).astype(BF16), m

    n_groups = n_units // ATTN_UNROLL
    for pattern, d in enumerate(DILATIONS):
        first = pattern == 0
        last = pattern == len(DILATIONS) - 1
        n_sub = n_res // d
        c_rows = blk // n_sub
        blocks_per_residue = n_units // d

        def gather(ref, res, start, size, d=d, n_sub=n_sub):
            return jnp.concatenate(
                [ref[d * i + res, pl.ds(start, size), :] for i in range(n_sub)], axis=0)

        def scatter(ref, res, start, val, d=d, n_sub=n_sub, c_rows=c_rows):
            for i in range(n_sub):
                ref[d * i + res, pl.ds(start, c_rows), :] = val[i * c_rows:(i + 1) * c_rows]

        def place(g, uu, c_rows=c_rows, blocks_per_residue=blocks_per_residue):
            u = g * ATTN_UNROLL + uu
            res = u // blocks_per_residue
            bq = u % blocks_per_residue
            kb = jnp.maximum(bq - 1, 0)
            return (res, pl.multiple_of(c_rows * bq, 8), pl.multiple_of(c_rows * kb, 8), bq - kb)

        def score_stage(g, slot, pattern=pattern, c_rows=c_rows, gather=gather, place=place):
            loaded = []
            for uu in range(ATTN_UNROLL):
                res, q0, k0, shift = place(g, uu)
                loaded.append((gather(q_ref, res, q0, c_rows) * scale,
                               gather(k_ref, res, k0, 2 * c_rows).astype(BF16),
                               bias_ref[2 * pattern + shift]))
            for uu, (q, k, bias) in enumerate(loaded):
                p0, m0 = probs(q, k, bias, head0)
                p1, m1 = probs(q, k, bias, jnp.logical_not(head0))
                p_sc[slot, uu, 0] = p0
                p_sc[slot, uu, 1] = p1
                mx_sc[slot, uu] = jnp.where(head0, m0, m1)

        def value_stage(g, slot, first=first, last=last, c_rows=c_rows,
                        gather=gather, scatter=scatter, place=place):
            loaded = []
            for uu in range(ATTN_UNROLL):
                res, q0, k0, _ = place(g, uu)
                old = None if first else tuple(
                    gather(ref, res, q0, c_rows) for ref in (o_ref, m_ref, l_ref))
                loaded.append((res, q0, gather(v_ref, res, k0, 2 * c_rows).astype(BF16),
                               p_sc[slot, uu, 0], p_sc[slot, uu, 1], mx_sc[slot, uu], old))
            results = []
            for res, q0, v, p0, p1, m_new, old in loaded:
                ones = jnp.ones_like(v)
                o0 = _dot(p0, jnp.where(kv_head0, v, ones))
                o1 = _dot(p1, jnp.where(kv_head0, ones, v))
                o_new = jnp.where(head0, o0, o1)
                l_new = pltpu.roll(jnp.where(head0, o1, o0), HEAD_DIM, 1)
                if not first:
                    o_old, m_old, l_old = old
                    m_tot = jnp.maximum(m_old, m_new)
                    w_old = jnp.exp2(m_old - m_tot)
                    w_new = jnp.exp2(m_new - m_tot)
                    o_new = w_old * o_old + w_new * o_new
                    l_new = w_old * l_old + w_new * l_new
                    m_new = m_tot
                results.append((res, q0, o_new, m_new, l_new))
            for res, q0, o_new, m_new, l_new in results:
                if last:
                    scatter(o_ref, res, q0, o_new / l_new)
                else:
                    scatter(o_ref, res, q0, o_new)
                    scatter(m_ref, res, q0, m_new)
                    scatter(l_ref, res, q0, l_new)

        def step(g, carry, score_stage=score_stage, value_stage=value_stage):
            slot = g & 1
            value_stage(g - 1, 1 - slot)
            score_stage(g, slot)
            return carry

        score_stage(0, 0)
        lax.fori_loop(1, n_groups, step, 0)
        value_stage(n_groups - 1, (n_groups - 1) & 1)


def _attention(z, n_pairs, *, q_col, k_col, v_col):
    b, n_res, length, _ = z.shape
    assert n_res == TOKEN_RES == DILATIONS[-1] and all(n_res % d == 0 for d in DILATIONS)
    assert length % ATTN_BLOCK == 0 and length >= 2 * ATTN_BLOCK and ATTN_BLOCK // n_res >= 8
    assert (n_res * length // ATTN_BLOCK) % ATTN_UNROLL == 0

    def spec(col):
        return pl.BlockSpec((None, n_res, length, LANES), lambda bi, h: (bi, 0, 0, col + h))

    slab = n_res * length * LANES * 4
    vmem = 10 * slab * 1.15 + (8 << 20)
    return pl.pallas_call(
        _attn_kernel,
        grid=(b, n_pairs),
        in_specs=[spec(q_col), spec(k_col), spec(v_col)],
        out_specs=spec(0),
        out_shape=jax.ShapeDtypeStruct((b, n_res, length, n_pairs * LANES), F32),
        scratch_shapes=[pltpu.VMEM((n_res, length, LANES), F32),
                        pltpu.VMEM((n_res, length, LANES), F32),
                        pltpu.VMEM((2 * len(DILATIONS), ATTN_BLOCK, 2 * ATTN_BLOCK), F32),
                        pltpu.VMEM((2, ATTN_UNROLL, 2, ATTN_BLOCK, 2 * ATTN_BLOCK), BF16),
                        pltpu.VMEM((2, ATTN_UNROLL, ATTN_BLOCK, LANES), F32)],
        compiler_params=_params(("parallel", "parallel"), vmem),
        name="attention",
    )(z, z, z)


def _ssm_weights(lam_re, lam_im, log_dt, b_re, b_im, c_re, c_im, d_skip, n_chunks):
    hp = lax.Precision.HIGHEST
    g, p = lam_re.shape
    c = b_re.shape[-1]
    t = SSM_CHUNK
    gb = GROUPS_PER_BLOCK
    nj = g // gb
    lr, li = lam_re.astype(F32), lam_im.astype(F32)
    dt = jnp.exp(log_dt.astype(F32))[:, None]
    mag = jnp.exp(lr * dt)
    ar = mag * jnp.cos(li * dt)
    ai = mag * jnp.sin(li * dt)
    nr, ni = ar - 1.0, ai
    den = lr * lr + li * li
    cr = (nr * lr + ni * li) / den
    ci = (ni * lr - nr * li) / den
    br, bi = b_re.astype(F32), b_im.astype(F32)
    bbr = cr[..., None] * br - ci[..., None] * bi
    bbi = cr[..., None] * bi + ci[..., None] * br

    def power(k):
        kk = k.astype(F32)[:, None, None]
        pm = jnp.exp(kk * (lr * dt))
        ang = kk * (li * dt)
        return pm * jnp.cos(ang), pm * jnp.sin(ang)

    pr, pi = power(jnp.arange(t + 1))
    cre, cim = c_re.astype(F32), c_im.astype(F32)
    car = cre[None] * pr[:, :, None, :] - cim[None] * pi[:, :, None, :]
    cai = cre[None] * pi[:, :, None, :] + cim[None] * pr[:, :, None, :]
    kern = (jnp.einsum('kgdp,gpc->kgdc', car[:t], bbr, precision=hp)
            - jnp.einsum('kgdp,gpc->kgdc', cai[:t], bbi, precision=hp))
    kern = kern.at[0].add(d_skip.astype(F32).reshape(g, c)[:, :, None] * jnp.eye(c, dtype=F32))

    kr = kern.reshape(t, nj, gb, c, c).transpose(1, 0, 4, 2, 3).reshape(nj, t, c, gb * c)

    rev_r, rev_i = pr[:t][::-1], pi[:t][::-1]
    abr = rev_r[..., None] * bbr[None] - rev_i[..., None] * bbi[None]
    abi = rev_r[..., None] * bbi[None] + rev_i[..., None] * bbr[None]

    def rows_form(x):
        x = x.reshape(t, nj, gb, p, c).transpose(1, 0, 2, 4, 3).reshape(nj, t * gb * c, p)
        return jnp.concatenate([x, x], axis=-1).astype(BF16)

    def cols_form(x):
        x = x.reshape(t, nj, gb, c, p).transpose(1, 4, 0, 2, 3).reshape(nj, p, t * gb * c)
        return jnp.concatenate([x, x], axis=1).astype(BF16)

    n_steps = max(1, (n_chunks - 1).bit_length())
    sr, si = power(t * (2 ** jnp.arange(n_steps)))
    rows = -(-n_steps // 8) * 8
    sr = jnp.pad(sr.reshape(n_steps, nj, gb * p).transpose(1, 0, 2), ((0, 0), (0, rows - n_steps), (0, 0)))
    si = jnp.pad(si.reshape(n_steps, nj, gb * p).transpose(1, 0, 2), ((0, 0), (0, rows - n_steps), (0, 0)))
    return (kr.astype(BF16), rows_form(abr), rows_form(abi),
            cols_form(car[1:]), cols_form(-cai[1:]), sr, si)


def _ssm_expand(kr_ref, abr_ref, abi_ref, cmr_ref, cmi_ref, m_sc, b_sc, c_sc):
    t, c, gb, p = SSM_CHUNK, SSM_GROUP, GROUPS_PER_BLOCK, SSM_STATE
    shift = c.bit_length() - 1
    half = gb * p
    pairs = LANES // p

    def group_of(idx):
        return (idx >> shift) & (gb - 1)

    m_sc[...] = jnp.zeros_like(m_sc)
    col_g = group_of(lax.broadcasted_iota(jnp.int32, (c, LANES), 1))
    for k in range(t):
        kk = kr_ref[k]
        dk = jnp.concatenate([jnp.where(col_g == g, kk, jnp.zeros_like(kk)) for g in range(gb)], axis=0)
        for tau in range(t - k):
            m_sc[tau * LANES:(tau + 1) * LANES, (tau + k) * LANES:(tau + k + 1) * LANES] = dk

    row_g = group_of(lax.broadcasted_iota(jnp.int32, (t * LANES, LANES), 0))
    lane_half = lax.broadcasted_iota(jnp.int32, (t * LANES, LANES), 1) >> (p.bit_length() - 1)
    for part, ab_ref in enumerate((abr_ref, abi_ref)):
        ab = ab_ref[...]
        for q in range(gb // pairs):
            b_sc[:, part * half + q * LANES:part * half + (q + 1) * LANES] = jnp.where(
                row_g == pairs * q + lane_half, ab, jnp.zeros_like(ab))

    col_g2 = group_of(lax.broadcasted_iota(jnp.int32, (LANES, t * LANES), 1))
    row_half = lax.broadcasted_iota(jnp.int32, (LANES, t * LANES), 0) >> (p.bit_length() - 1)
    for part, cm_ref in enumerate((cmr_ref, cmi_ref)):
        cm = cm_ref[...]
        for q in range(gb // pairs):
            c_sc[part * half + q * LANES:part * half + (q + 1) * LANES, :] = jnp.where(
                col_g2 == pairs * q + row_half, cm, jnp.zeros_like(cm))


def _ssm_kernel(s_ref, kr_ref, abr_ref, abi_ref, cmr_ref, cmi_ref, sr_ref, si_ref, y_ref,
                m_sc, b_sc, c_sc):
    t, n_chunks, _ = s_ref.shape
    half = GROUPS_PER_BLOCK * SSM_STATE

    @pl.when(pl.program_id(1) == 0)
    def _():
        _ssm_expand(kr_ref, abr_ref, abi_ref, cmr_ref, cmi_ref, m_sc, b_sc, c_sc)

    u = jnp.concatenate([s_ref[tau].astype(BF16) for tau in range(t)], axis=1)
    y = _dot(u, m_sc[...])
    x = _dot(u, b_sc[...])
    hr, hi = x[:, :half], x[:, half:]

    row = lax.broadcasted_iota(jnp.int32, (n_chunks, half), 0)

    def shifted(h, k):
        return jnp.where(row >= k, pltpu.roll(h, k, 0), 0.0)

    step = 0
    while (1 << step) < n_chunks:
        k = 1 << step
        pr = sr_ref[step:step + 1, :]
        pi = si_ref[step:step + 1, :]
        zr, zi = shifted(hr, k), shifted(hi, k)
        hr, hi = hr + pr * zr - pi * zi, hi + pr * zi + pi * zr
        step += 1

    h_prev = jnp.concatenate([shifted(hr, 1), shifted(hi, 1)], axis=1).astype(BF16)
    y = jax.nn.gelu(y + _dot(h_prev, c_sc[...]))
    for tau in range(t):
        y_ref[tau] = y[:, tau * LANES:(tau + 1) * LANES]


def _ssm(z, s_col, weights):
    b, n_res, n_chunks, _ = z.shape
    nj = weights[0].shape[0]
    assert n_res == SSM_CHUNK
    assert SSM_GROUP & (SSM_GROUP - 1) == 0 and SSM_STATE & (SSM_STATE - 1) == 0

    def wspec(a):
        nd = a.ndim - 1
        return pl.BlockSpec((None,) + a.shape[1:], lambda j, bi: (j,) + (0,) * nd)

    s = n_res * n_chunks
    width = SSM_CHUNK * LANES
    state = 2 * GROUPS_PER_BLOCK * SSM_STATE
    scratch = (width * width + 2 * width * state) * 2
    wbytes = 2 * sum(math.prod(a.shape[1:]) * a.dtype.itemsize for a in weights)
    vmem = (4 * s * LANES * 4 + wbytes + scratch + n_chunks * (width * 10 + state * 12)) * 1.2
    return pl.pallas_call(
        _ssm_kernel,
        grid=(nj, b),
        in_specs=[pl.BlockSpec((None, n_res, n_chunks, LANES),
                               lambda j, bi: (bi, 0, 0, s_col + j))]
        + [wspec(a) for a in weights],
        out_specs=pl.BlockSpec((None, n_res, n_chunks, LANES), lambda j, bi: (bi, 0, 0, j)),
        out_shape=jax.ShapeDtypeStruct((b, n_res, n_chunks, nj * LANES), F32),
        scratch_shapes=[pltpu.VMEM((width, width), BF16), pltpu.VMEM((width, state), BF16),
                        pltpu.VMEM((state, width), BF16)],
        compiler_params=_params(("parallel", "arbitrary"), vmem),
        name="ssm",
    )(z, *weights)


def _mix_out_kernel(ya_ref, y_ref, h_ref, ga_ref, gb_ref, wglu_ref, bglu_ref,
                    wa_ref, wb_ref, o_ref):
    n_res, tl, _ = ya_ref.shape
    y = y_ref[...].reshape(n_res * tl, -1)
    gate = _dot(y.astype(BF16), wglu_ref[...]) + bglu_ref[...]
    yb = y * jax.nn.sigmoid(gate)
    na = _rms(ya_ref[...].reshape(n_res * tl, -1), ga_ref[...]).astype(BF16)
    nb = _rms(yb, gb_ref[...]).astype(BF16)
    perm = _residue_perm(n_res, tl, True)
    na = _dot(perm, na).astype(BF16)
    nb = _dot(perm, nb).astype(BF16)
    o_ref[...] = h_ref[...] + _dot(na, wa_ref[...]) + _dot(nb, wb_ref[...])


def _mix_out(ya, y, h, ga, gb, wglu, bglu, wa, wb, *, tm=512):
    m, d = h.shape
    batch, n_res, length, wa_w = ya.shape
    wb_w = y.shape[-1]
    tm = min(tm, n_res * length)
    tl = tm // n_res
    tiles = length // tl
    assert tl % 8 == 0 and length % tl == 0
    vmem = (2 * tm * (wa_w + wb_w + 2 * d) * 4 + (wglu.size + wa.size + wb.size) * 2
            + 8 * tm * d * 4) * 1.2
    row = pl.BlockSpec((tm, d), lambda i: (i, 0))
    res_major = lambda w: pl.BlockSpec((None, n_res, tl, w),
                                       lambda i: (i // tiles, 0, i % tiles, 0))
    return pl.pallas_call(
        _mix_out_kernel,
        grid=(m // tm,),
        in_specs=[res_major(wa_w), res_major(wb_w), row,
                  _const_spec((1, wa_w)), _const_spec((1, wb_w)),
                  _const_spec(wglu.shape), _const_spec((1, wb_w)),
                  _const_spec(wa.shape), _const_spec(wb.shape)],
        out_specs=row,
        out_shape=jax.ShapeDtypeStruct((m, d), F32),
        compiler_params=_params(("parallel",), vmem),
        name="mix_out",
    )(ya, y, h, ga.reshape(1, -1), gb.reshape(1, -1), wglu, bglu.reshape(1, -1), wa, wb)


def _ple_kernel(h_ref, p_ref, gn_ref, wg_ref, wp_ref, fn_ref, o_ref, *, final):
    h = h_ref[...]
    gate = jax.nn.sigmoid(_dot(_rms(h, gn_ref[...]).astype(BF16), wg_ref[...]))
    proj = _dot(p_ref[...].astype(BF16), wp_ref[...])
    out = h + gate * proj
    o_ref[...] = _rms(out, fn_ref[...]) if final else out


def _ple(h, p, gn, wg, wp, fn, *, final, tm=512):
    m, d = h.shape
    pd = p.shape[1]
    tm = min(tm, m)
    vmem = (2 * tm * (2 * d + pd) * 4 + (wg.size + wp.size) * 2 + 8 * tm * d * 4) * 1.2
    row = lambda w: pl.BlockSpec((tm, w), lambda i: (i, 0))
    return pl.pallas_call(
        functools.partial(_ple_kernel, final=final),
        grid=(m // tm,),
        in_specs=[row(d), row(pd), _const_spec((1, d)), _const_spec(wg.shape),
                  _const_spec(wp.shape), _const_spec((1, d))],
        out_specs=row(d),
        out_shape=jax.ShapeDtypeStruct((m, d), F32),
        compiler_params=_params(("parallel",), vmem),
        name="ple",
    )(h, p, gn.reshape(1, d), wg, wp, fn.reshape(1, d))


def kernel(x, p, ffn1_norm, ffn1_w_gate, ffn1_w_up, ffn1_w_down, mix_norm, w_in, attn_out_norm, ssm_lambda_re, ssm_lambda_im, ssm_log_dt, ssm_b_re, ssm_b_im, ssm_c_re, ssm_c_im, ssm_d, ssm_w_glu, ssm_b_glu, ssm_out_norm, w_out, ffn2_norm, ffn2_w_gate, ffn2_w_up, ffn2_w_down, ple_norm, ple_w_gate, ple_w_proj, final_norm):
    b, s, d = x.shape
    depth = p.shape[0]
    attn_w = attn_out_norm.shape[1]
    ssm_w = ssm_out_norm.shape[1]
    n_pairs = attn_w // LANES
    h = x.reshape(b * s, d)
    for i in range(depth):
        h = _ffn(h, ffn1_norm[i], _ff_tiles(ffn1_w_gate[i], 1), _ff_tiles(ffn1_w_up[i], 1),
                 _ff_tiles(ffn1_w_down[i], 0))
        z = _in_proj(h, mix_norm[i], w_in[i].astype(BF16), b)
        ya = _attention(z, n_pairs, q_col=0, k_col=n_pairs, v_col=2 * n_pairs)
        weights = _ssm_weights(ssm_lambda_re[i], ssm_lambda_im[i], ssm_log_dt[i],
                               ssm_b_re[i], ssm_b_im[i], ssm_c_re[i], ssm_c_im[i],
                               ssm_d[i], s // SSM_CHUNK)
        y = _ssm(z, 3 * n_pairs, weights)
        wo = w_out[i].astype(BF16)
        h = _mix_out(ya, y, h, attn_out_norm[i], ssm_out_norm[i], ssm_w_glu[i].astype(BF16),
                     ssm_b_glu[i], wo[:attn_w], wo[attn_w:])
        h = _ffn(h, ffn2_norm[i], _ff_tiles(ffn2_w_gate[i], 1), _ff_tiles(ffn2_w_up[i], 1),
                 _ff_tiles(ffn2_w_down[i], 0))
        h = _ple(h, p[i].reshape(b * s, -1), ple_norm[i], ple_w_gate[i].astype(BF16),
                 ple_w_proj[i].astype(BF16), final_norm, final=i == depth - 1)
    return h.reshape(b, s, d)
```

```python
import functools
import math

import jax
import jax.numpy as jnp
from jax import lax
from jax.experimental import pallas as pl
from jax.experimental.pallas import tpu as pltpu

NORM_EPS = 1e-6
MASK_VALUE = -1e30
HEAD_DIM = 64
ATTN_BLOCK = 128
ATTN_SPAN = 128
DILATIONS = (1, 4, 16)
ATTN_UNROLL = 16
SSM_GROUP = 16
SSM_STATE = 64
SSM_CHUNK = 16
TOKEN_RES = 16
LANES = 128
GROUPS_PER_BLOCK = LANES // SSM_GROUP
FF_TILE = 512
ROW_SPLIT = 2
VMEM_CAP = 56 * 1024 * 1024

BF16 = jnp.bfloat16
F32 = jnp.float32


def _params(sem, vmem_bytes):
    return pltpu.CompilerParams(
        dimension_semantics=sem, vmem_limit_bytes=min(int(vmem_bytes), VMEM_CAP))


def _rms(x, g):
    ms = jnp.mean(x * x, axis=-1, keepdims=True)
    return x * lax.rsqrt(ms + NORM_EPS) * g


def _dot(a, b):
    return jnp.dot(a, b, preferred_element_type=F32)


def _const_spec(shape):
    return pl.BlockSpec(shape, lambda *_: (0,) * len(shape),
                        pipeline_mode=pl.Buffered(1))


def _ffn_kernel(x_ref, g_ref, wg_ref, wu_ref, wd_ref, o_ref, xn_ref):
    f = pl.program_id(1)

    @pl.when(f == 0)
    def _():
        xn_ref[...] = _rms(x_ref[...], g_ref[...]).astype(BF16)
        o_ref[...] = x_ref[...]

    xn = xn_ref[...]
    gate = _dot(xn, wg_ref[...])
    up = _dot(xn, wu_ref[...])
    act = (gate * jax.nn.sigmoid(gate) * up).astype(BF16)
    o_ref[...] += _dot(act, wd_ref[...])


def _ffn(h, g, wg, wu, wd, *, tm=512):
    m, d = h.shape
    nf, _, tf = wg.shape
    tm = min(tm, m)
    vmem = (4 * tm * d * 4 + tm * d * 2 + 6 * d * tf * 2 + 6 * tm * tf * 4) * 1.25
    row = pl.BlockSpec((tm, d), lambda i, f: (i, 0))
    return pl.pallas_call(
        _ffn_kernel,
        grid=(m // tm, nf),
        in_specs=[
            row,
            pl.BlockSpec((1, d), lambda i, f: (0, 0)),
            pl.BlockSpec((None, d, tf), lambda i, f: (f, 0, 0)),
            pl.BlockSpec((None, d, tf), lambda i, f: (f, 0, 0)),
            pl.BlockSpec((None, tf, d), lambda i, f: (f, 0, 0)),
        ],
        out_specs=row,
        out_shape=jax.ShapeDtypeStruct((m, d), F32),
        scratch_shapes=[pltpu.VMEM((tm, d), BF16)],
        compiler_params=_params(("parallel", "arbitrary"), vmem),
        name="ffn",
    )(h, g.reshape(1, d), wg, wu, wd)


def _ff_tile_kernel(w_ref, o_ref, *, axis, size, scale):
    w = w_ref[...]
    tile = w.shape[axis]
    pos = lax.broadcasted_iota(jnp.int32, w.shape, axis) + pl.program_id(0) * tile
    o_ref[...] = jnp.where(pos < size, w * scale, 0.0).astype(o_ref.dtype)


def _ff_tiles(w, axis, tf=FF_TILE, scale=1.0):
    size = w.shape[axis]
    nf = pl.cdiv(size, tf)
    block = list(w.shape)
    block[axis] = tf
    index = (lambda f: (0, f)) if axis == 1 else (lambda f: (f, 0))
    return pl.pallas_call(
        functools.partial(_ff_tile_kernel, axis=axis, size=size, scale=scale),
        grid=(nf,),
        in_specs=[pl.BlockSpec(tuple(block), index)],
        out_specs=pl.BlockSpec((None,) + tuple(block), lambda f: (f, 0, 0)),
        out_shape=jax.ShapeDtypeStruct((nf,) + tuple(block), BF16),
        compiler_params=_params(("parallel",), 6 * math.prod(block) * 4),
        name="ff_tiles",
    )(w)


def _residue_perm(n_res, tl, transpose):
    tm = n_res * tl
    assert tl & (tl - 1) == 0
    a = lax.broadcasted_iota(jnp.int32, (tm, tm), 1 if transpose else 0)
    b = lax.broadcasted_iota(jnp.int32, (tm, tm), 0 if transpose else 1)
    hit = b == n_res * (a & (tl - 1)) + (a >> (tl.bit_length() - 1))
    return jnp.where(hit, 1.0, 0.0).astype(BF16)


def _in_proj_kernel(x_ref, g_ref, w_ref, o_ref):
    n_res, tl, _ = o_ref.shape
    xn = _rms(x_ref[...], g_ref[...]).astype(BF16)
    xn = _dot(_residue_perm(n_res, tl, False), xn).astype(BF16)
    res = _dot(xn, w_ref[...])
    for r in range(n_res):
        o_ref[r] = res[r * tl:(r + 1) * tl]


def _in_proj(h, g, w, batch, *, tm=256):
    m, d = h.shape
    n = w.shape[1]
    seq = m // batch
    tm = min(tm, seq)
    tl = tm // TOKEN_RES
    tiles = seq // tm
    assert tl % 8 == 0 and seq % tm == 0
    vmem = (2 * tm * d * 4 + tm * d * 6 + d * n * 2 + 4 * tm * n * 4 + 4 * tm * tm) * 1.2
    return pl.pallas_call(
        _in_proj_kernel,
        grid=(m // tm,),
        in_specs=[
            pl.BlockSpec((tm, d), lambda i: (i, 0)),
            _const_spec((1, d)),
            _const_spec((d, n)),
        ],
        out_specs=pl.BlockSpec((None, TOKEN_RES, tl, n),
                               lambda i: (i // tiles, 0, i % tiles, 0)),
        out_shape=jax.ShapeDtypeStruct((batch, TOKEN_RES, seq // TOKEN_RES, n), F32),
        compiler_params=_params(("parallel",), vmem),
        name="in_proj",
    )(h, g.reshape(1, d), w)


def _attn_kernel(q_ref, k_ref, v_ref, o_ref, m_ref, l_ref, bias_ref, p_sc, mx_sc):
    n_res, length, _ = q_ref.shape
    blk = ATTN_BLOCK
    n_units = n_res * length // blk
    head0 = lax.broadcasted_iota(jnp.int32, (blk, LANES), 1) < HEAD_DIM
    kv_head0 = lax.broadcasted_iota(jnp.int32, (2 * blk, LANES), 1) < HEAD_DIM
    scale = HEAD_DIM ** -0.5 * math.log2(math.e)

    rho = lax.broadcasted_iota(jnp.int32, (blk, 2 * blk), 0)
    kap = lax.broadcasted_iota(jnp.int32, (blk, 2 * blk), 1)
    for pattern, d in enumerate(DILATIONS):
        n_sub = n_res // d
        c_rows = blk // n_sub
        sh = c_rows.bit_length() - 1
        iq, lq = rho >> sh, rho & (c_rows - 1)
        ik, lk = kap >> (sh + 1), kap & (2 * c_rows - 1)
        for shift in range(2):
            dist = n_sub * (lq - lk + shift * c_rows) + (iq - ik)
            bias_ref[2 * pattern + shift] = jnp.where(
                (dist >= 0) & (dist <= ATTN_SPAN), 0.0, MASK_VALUE)

    def probs(q, k, bias, sel):
        qh = jnp.where(sel, q, 0.0).astype(BF16)
        s = lax.dot_general(qh, k, (((1,), (1,)), ((), ())),
                            preferred_element_type=F32) + bias
        m = jnp.max(s, axis=-1, keepdims=True)
        return jnp.exp2(s - m).astype(BF16), m

    n_groups = n_units // ATTN_UNROLL
    for pattern, d in enumerate(DILATIONS):
        first = pattern == 0
        last = pattern == len(DILATIONS) - 1
        n_sub = n_res // d
        c_rows = blk // n_sub
        blocks_per_residue = n_units // d

        def gather(ref, res, start, size, d=d, n_sub=n_sub):
            return jnp.concatenate(
                [ref[d * i + res, pl.ds(start, size), :] for i in range(n_sub)], axis=0)

        def scatter(ref, res, start, val, d=d, n_sub=n_sub, c_rows=c_rows):
            for i in range(n_sub):
                ref[d * i + res, pl.ds(start, c_rows), :] = val[i * c_rows:(i + 1) * c_rows]

        def place(g, uu, c_rows=c_rows, blocks_per_residue=blocks_per_residue):
            u = g * ATTN_UNROLL + uu
            res = u // blocks_per_residue
            bq = u % blocks_per_residue
            kb = jnp.maximum(bq - 1, 0)
            return (res, pl.multiple_of(c_rows * bq, 8), pl.multiple_of(c_rows * kb, 8), bq - kb)

        def score_stage(g, slot, pattern=pattern, c_rows=c_rows, gather=gather, place=place):
            loaded = []
            for uu in range(ATTN_UNROLL):
                res, q0, k0, shift = place(g, uu)
                loaded.append((gather(q_ref, res, q0, c_rows) * scale,
                               gather(k_ref, res, k0, 2 * c_rows).astype(BF16),
                               bias_ref[2 * pattern + shift]))
            for uu, (q, k, bias) in enumerate(loaded):
                p0, m0 = probs(q, k, bias, head0)
                p1, m1 = probs(q, k, bias, jnp.logical_not(head0))
                p_sc[slot, uu, 0] = p0
                p_sc[slot, uu, 1] = p1
                mx_sc[slot, uu] = jnp.where(head0, m0, m1)

        def value_stage(g, slot, first=first, last=last, c_rows=c_rows,
                        gather=gather, scatter=scatter, place=place):
            loaded = []
            for uu in range(ATTN_UNROLL):
                res, q0, k0, _ = place(g, uu)
                old = None if first else tuple(
                    gather(ref, res, q0, c_rows) for ref in (o_ref, m_ref, l_ref))
                loaded.append((res, q0, gather(v_ref, res, k0, 2 * c_rows).astype(BF16),
                               p_sc[slot, uu, 0], p_sc[slot, uu, 1], mx_sc[slot, uu], old))
            results = []
            for res, q0, v, p0, p1, m_new, old in loaded:
                ones = jnp.ones_like(v)
                o0 = _dot(p0, jnp.where(kv_head0, v, ones))
                o1 = _dot(p1, jnp.where(kv_head0, ones, v))
                o_new = jnp.where(head0, o0, o1)
                l_new = pltpu.roll(jnp.where(head0, o1, o0), HEAD_DIM, 1)
                if not first:
                    o_old, m_old, l_old = old
                    m_tot = jnp.maximum(m_old, m_new)
                    w_old = jnp.exp2(m_old - m_tot)
                    w_new = jnp.exp2(m_new - m_tot)
                    o_new = w_old * o_old + w_new * o_new
                    l_new = w_old * l_old + w_new * l_new
                    m_new = m_tot
                results.append((res, q0, o_new, m_new, l_new))
            for res, q0, o_new, m_new, l_new in results:
                if last:
                    scatter(o_ref, res, q0, o_new / l_new)
                else:
                    scatter(o_ref, res, q0, o_new)
                    scatter(m_ref, res, q0, m_new)
                    scatter(l_ref, res, q0, l_new)

        def step(g, carry, score_stage=score_stage, value_stage=value_stage):
            slot = g & 1
            value_stage(g - 1, 1 - slot)
            score_stage(g, slot)
            return carry

        score_stage(0, 0)
        lax.fori_loop(1, n_groups, step, 0)
        value_stage(n_groups - 1, (n_groups - 1) & 1)


def _attention(z, n_pairs, *, q_col, k_col, v_col):
    b, n_res, length, _ = z.shape
    assert n_res == TOKEN_RES == DILATIONS[-1] and all(n_res % d == 0 for d in DILATIONS)
    assert length % ATTN_BLOCK == 0 and length >= 2 * ATTN_BLOCK and ATTN_BLOCK // n_res >= 8
    assert (n_res * length // ATTN_BLOCK) % ATTN_UNROLL == 0

    def spec(col):
        return pl.BlockSpec((None, n_res, length, LANES), lambda bi, h: (bi, 0, 0, col + h))

    slab = n_res * length * LANES * 4
    vmem = 10 * slab * 1.15 + (8 << 20)
    return pl.pallas_call(
        _attn_kernel,
        grid=(b, n_pairs),
        in_specs=[spec(q_col), spec(k_col), spec(v_col)],
        out_specs=spec(0),
        out_shape=jax.ShapeDtypeStruct((b, n_res, length, n_pairs * LANES), F32),
        scratch_shapes=[pltpu.VMEM((n_res, length, LANES), F32),
                        pltpu.VMEM((n_res, length, LANES), F32),
                        pltpu.VMEM((2 * len(DILATIONS), ATTN_BLOCK, 2 * ATTN_BLOCK), F32),
                        pltpu.VMEM((2, ATTN_UNROLL, 2, ATTN_BLOCK, 2 * ATTN_BLOCK), BF16),
                        pltpu.VMEM((2, ATTN_UNROLL, ATTN_BLOCK, LANES), F32)],
        compiler_params=_params(("parallel", "parallel"), vmem),
        name="attention",
    )(z, z, z)


def _ssm_weights(lam_re, lam_im, log_dt, b_re, b_im, c_re, c_im, d_skip, n_chunks):
    hp = lax.Precision.HIGHEST
    g, p = lam_re.shape
    c = b_re.shape[-1]
    t = SSM_CHUNK
    gb = GROUPS_PER_BLOCK
    nj = g // gb
    lr, li = lam_re.astype(F32), lam_im.astype(F32)
    dt = jnp.exp(log_dt.astype(F32))[:, None]
    mag = jnp.exp(lr * dt)
    ar = mag * jnp.cos(li * dt)
    ai = mag * jnp.sin(li * dt)
    nr, ni = ar - 1.0, ai
    den = lr * lr + li * li
    cr = (nr * lr + ni * li) / den
    ci = (ni * lr - nr * li) / den
    br, bi = b_re.astype(F32), b_im.astype(F32)
    bbr = cr[..., None] * br - ci[..., None] * bi
    bbi = cr[..., None] * bi + ci[..., None] * br

    def power(k):
        kk = k.astype(F32)[:, None, None]
        pm = jnp.exp(kk * (lr * dt))
        ang = kk * (li * dt)
        return pm * jnp.cos(ang), pm * jnp.sin(ang)

    pr, pi = power(jnp.arange(t + 1))
    cre, cim = c_re.astype(F32), c_im.astype(F32)
    car = cre[None] * pr[:, :, None, :] - cim[None] * pi[:, :, None, :]
    cai = cre[None] * pi[:, :, None, :] + cim[None] * pr[:, :, None, :]
    kern = (jnp.einsum('kgdp,gpc->kgdc', car[:t], bbr, precision=hp)
            - jnp.einsum('kgdp,gpc->kgdc', cai[:t], bbi, precision=hp))
    kern = kern.at[0].add(d_skip.astype(F32).reshape(g, c)[:, :, None] * jnp.eye(c, dtype=F32))

    kr = kern.reshape(t, nj, gb, c, c).transpose(1, 0, 4, 2, 3).reshape(nj, t, c, gb * c)

    rev_r, rev_i = pr[:t][::-1], pi[:t][::-1]
    abr = rev_r[..., None] * bbr[None] - rev_i[..., None] * bbi[None]
    abi = rev_r[..., None] * bbi[None] + rev_i[..., None] * bbr[None]

    def rows_form(x):
        x = x.reshape(t, nj, gb, p, c).transpose(1, 0, 2, 4, 3).reshape(nj, t * gb * c, p)
        return jnp.concatenate([x, x], axis=-1).astype(BF16)

    def cols_form(x):
        x = x.reshape(t, nj, gb, c, p).transpose(1, 4, 0, 2, 3).reshape(nj, p, t * gb * c)
        return jnp.concatenate([x, x], axis=1).astype(BF16)

    n_steps = max(1, (n_chunks - 1).bit_length())
    sr, si = power(t * (2 ** jnp.arange(n_steps)))
    rows = -(-n_steps // 8) * 8
    sr = jnp.pad(sr.reshape(n_steps, nj, gb * p).transpose(1, 0, 2), ((0, 0), (0, rows - n_steps), (0, 0)))
    si = jnp.pad(si.reshape(n_steps, nj, gb * p).transpose(1, 0, 2), ((0, 0), (0, rows - n_steps), (0, 0)))
    return (kr.astype(BF16), rows_form(abr), rows_form(abi),
            cols_form(car[1:]), cols_form(-cai[1:]), sr, si)


def _ssm_expand(kr_ref, abr_ref, abi_ref, cmr_ref, cmi_ref, m_sc, b_sc, c_sc):
    t, c, gb, p = SSM_CHUNK, SSM_GROUP, GROUPS_PER_BLOCK, SSM_STATE
    shift = c.bit_length() - 1
    half = gb * p
    pairs = LANES // p

    def group_of(idx):
        return (idx >> shift) & (gb - 1)

    m_sc[...] = jnp.zeros_like(m_sc)
    col_g = group_of(lax.broadcasted_iota(jnp.int32, (c, LANES), 1))
    for k in range(t):
        kk = kr_ref[k]
        dk = jnp.concatenate([jnp.where(col_g == g, kk, jnp.zeros_like(kk)) for g in range(gb)], axis=0)
        for tau in range(t - k):
            m_sc[tau * LANES:(tau + 1) * LANES, (tau + k) * LANES:(tau + k + 1) * LANES] = dk

    row_g = group_of(lax.broadcasted_iota(jnp.int32, (t * LANES, LANES), 0))
    lane_half = lax.broadcasted_iota(jnp.int32, (t * LANES, LANES), 1) >> (p.bit_length() - 1)
    for part, ab_ref in enumerate((abr_ref, abi_ref)):
        ab = ab_ref[...]
        for q in range(gb // pairs):
            b_sc[:, part * half + q * LANES:part * half + (q + 1) * LANES] = jnp.where(
                row_g == pairs * q + lane_half, ab, jnp.zeros_like(ab))

    col_g2 = group_of(lax.broadcasted_iota(jnp.int32, (LANES, t * LANES), 1))
    row_half = lax.broadcasted_iota(jnp.int32, (LANES, t * LANES), 0) >> (p.bit_length() - 1)
    for part, cm_ref in enumerate((cmr_ref, cmi_ref)):
        cm = cm_ref[...]
        for q in range(gb // pairs):
            c_sc[part * half + q * LANES:part * half + (q + 1) * LANES, :] = jnp.where(
                col_g2 == pairs * q + row_half, cm, jnp.zeros_like(cm))


def _ssm_kernel(s_ref, kr_ref, abr_ref, abi_ref, cmr_ref, cmi_ref, sr_ref, si_ref, y_ref,
                m_sc, b_sc, c_sc):
    t, n_chunks, _ = s_ref.shape
    half = GROUPS_PER_BLOCK * SSM_STATE

    @pl.when(pl.program_id(1) == 0)
    def _():
        _ssm_expand(kr_ref, abr_ref, abi_ref, cmr_ref, cmi_ref, m_sc, b_sc, c_sc)

    u = jnp.concatenate([s_ref[tau].astype(BF16) for tau in range(t)], axis=1)
    y = _dot(u, m_sc[...])
    x = _dot(u, b_sc[...])
    hr, hi = x[:, :half], x[:, half:]

    row = lax.broadcasted_iota(jnp.int32, (n_chunks, half), 0)

    def shifted(h, k):
        return jnp.where(row >= k, pltpu.roll(h, k, 0), 0.0)

    step = 0
    while (1 << step) < n_chunks:
        k = 1 << step
        pr = sr_ref[step:step + 1, :]
        pi = si_ref[step:step + 1, :]
        zr, zi = shifted(hr, k), shifted(hi, k)
        hr, hi = hr + pr * zr - pi * zi, hi + pr * zi + pi * zr
        step += 1

    h_prev = jnp.concatenate([shifted(hr, 1), shifted(hi, 1)], axis=1).astype(BF16)
    y = jax.nn.gelu(y + _dot(h_prev, c_sc[...]))
    for tau in range(t):
        y_ref[tau] = y[:, tau * LANES:(tau + 1) * LANES]


def _ssm(z, s_col, weights):
    b, n_res, n_chunks, _ = z.shape
    nj = weights[0].shape[0]
    assert n_res == SSM_CHUNK
    assert SSM_GROUP & (SSM_GROUP - 1) == 0 and SSM_STATE & (SSM_STATE - 1) == 0

    def wspec(a):
        nd = a.ndim - 1
        return pl.BlockSpec((None,) + a.shape[1:], lambda j, bi: (j,) + (0,) * nd)

    s = n_res * n_chunks
    width = SSM_CHUNK * LANES
    state = 2 * GROUPS_PER_BLOCK * SSM_STATE
    scratch = (width * width + 2 * width * state) * 2
    wbytes = 2 * sum(math.prod(a.shape[1:]) * a.dtype.itemsize for a in weights)
    vmem = (4 * s * LANES * 4 + wbytes + scratch + n_chunks * (width * 10 + state * 12)) * 1.2
    return pl.pallas_call(
        _ssm_kernel,
        grid=(nj, b),
        in_specs=[pl.BlockSpec((None, n_res, n_chunks, LANES),
                               lambda j, bi: (bi, 0, 0, s_col + j))]
        + [wspec(a) for a in weights],
        out_specs=pl.BlockSpec((None, n_res, n_chunks, LANES), lambda j, bi: (bi, 0, 0, j)),
        out_shape=jax.ShapeDtypeStruct((b, n_res, n_chunks, nj * LANES), F32),
        scratch_shapes=[pltpu.VMEM((width, width), BF16), pltpu.VMEM((width, state), BF16),
                        pltpu.VMEM((state, width), BF16)],
        compiler_params=_params(("parallel", "arbitrary"), vmem),
        name="ssm",
    )(z, *weights)


def _mix_out_kernel(ya_ref, y_ref, h_ref, ga_ref, gb_ref, wglu_ref, bglu_ref,
                    wa_ref, wb_ref, o_ref):
    n_res, tl, _ = ya_ref.shape
    y = y_ref[...].reshape(n_res * tl, -1)
    gate = _dot(y.astype(BF16), wglu_ref[...]) + bglu_ref[...]
    yb = y * jax.nn.sigmoid(gate)
    na = _rms(ya_ref[...].reshape(n_res * tl, -1), ga_ref[...]).astype(BF16)
    nb = _rms(yb, gb_ref[...]).astype(BF16)
    perm = _residue_perm(n_res, tl, True)
    na = _dot(perm, na).astype(BF16)
    nb = _dot(perm, nb).astype(BF16)
    o_ref[...] = h_ref[...] + _dot(na, wa_ref[...]) + _dot(nb, wb_ref[...])


def _mix_out(ya, y, h, ga, gb, wglu, bglu, wa, wb, *, tm=512):
    m, d = h.shape
    batch, n_res, length, wa_w = ya.shape
    wb_w = y.shape[-1]
    tm = min(tm, n_res * length)
    tl = tm // n_res
    tiles = length // tl
    assert tl % 8 == 0 and length % tl == 0
    vmem = (2 * tm * (wa_w + wb_w + 2 * d) * 4 + (wglu.size + wa.size + wb.size) * 2
            + 8 * tm * d * 4) * 1.2
    row = pl.BlockSpec((tm, d), lambda i: (i, 0))
    res_major = lambda w: pl.BlockSpec((None, n_res, tl, w),
                                       lambda i: (i // tiles, 0, i % tiles, 0))
    return pl.pallas_call(
        _mix_out_kernel,
        grid=(m // tm,),
        in_specs=[res_major(wa_w), res_major(wb_w), row,
                  _const_spec((1, wa_w)), _const_spec((1, wb_w)),
                  _const_spec(wglu.shape), _const_spec((1, wb_w)),
                  _const_spec(wa.shape), _const_spec(wb.shape)],
        out_specs=row,
        out_shape=jax.ShapeDtypeStruct((m, d), F32),
        compiler_params=_params(("parallel",), vmem),
        name="mix_out",
    )(ya, y, h, ga.reshape(1, -1), gb.reshape(1, -1), wglu, bglu.reshape(1, -1), wa, wb)


def _ple_kernel(h_ref, p_ref, gn_ref, wg_ref, wp_ref, fn_ref, o_ref, *, final):
    sub = h_ref.shape[0] // ROW_SPLIT
    for s in range(ROW_SPLIT):
        rows = slice(s * sub, (s + 1) * sub)
        h = h_ref[rows, :]
        gate = jax.nn.sigmoid(_dot(_rms(h, gn_ref[...]).astype(BF16), wg_ref[...]))
        proj = _dot(p_ref[rows, :].astype(BF16), wp_ref[...])
        out = h + gate * proj
        o_ref[rows, :] = _rms(out, fn_ref[...]) if final else out


def _ple(h, p, gn, wg, wp, fn, *, final, tm=512):
    m, d = h.shape
    pd = p.shape[1]
    tm = min(tm, m)
    vmem = (2 * tm * (2 * d + pd) * 4 + (wg.size + wp.size) * 2 + 8 * tm * d * 4) * 1.2
    row = lambda w: pl.BlockSpec((tm, w), lambda i: (i, 0))
    return pl.pallas_call(
        functools.partial(_ple_kernel, final=final),
        grid=(m // tm,),
        in_specs=[row(d), row(pd), _const_spec((1, d)), _const_spec(wg.shape),
                  _const_spec(wp.shape), _const_spec((1, d))],
        out_specs=row(d),
        out_shape=jax.ShapeDtypeStruct((m, d), F32),
        compiler_params=_params(("parallel",), vmem),
        name="ple",
    )(h, p, gn.reshape(1, d), wg, wp, fn.reshape(1, d))


def kernel(x, p, ffn1_norm, ffn1_w_gate, ffn1_w_up, ffn1_w_down, mix_norm, w_in, attn_out_norm, ssm_lambda_re, ssm_lambda_im, ssm_log_dt, ssm_b_re, ssm_b_im, ssm_c_re, ssm_c_im, ssm_d, ssm_w_glu, ssm_b_glu, ssm_out_norm, w_out, ffn2_norm, ffn2_w_gate, ffn2_w_up, ffn2_w_down, ple_norm, ple_w_gate, ple_w_proj, final_norm):
    b, s, d = x.shape
    depth = p.shape[0]
    attn_w = attn_out_norm.shape[1]
    ssm_w = ssm_out_norm.shape[1]
    n_pairs = attn_w // LANES
    h = x.reshape(b * s, d)
    for i in range(depth):
        h = _ffn(h, ffn1_norm[i], _ff_tiles(ffn1_w_gate[i], 1), _ff_tiles(ffn1_w_up[i], 1),
                 _ff_tiles(ffn1_w_down[i], 0, scale=0.5))
        z = _in_proj(h, mix_norm[i], w_in[i].astype(BF16), b)
        ya = _attention(z, n_pairs, q_col=0, k_col=n_pairs, v_col=2 * n_pairs)
        weights = _ssm_weights(ssm_lambda_re[i], ssm_lambda_im[i], ssm_log_dt[i],
                               ssm_b_re[i], ssm_b_im[i], ssm_c_re[i], ssm_c_im[i],
                               ssm_d[i], s // SSM_CHUNK)
        y = _ssm(z, 3 * n_pairs, weights)
        wo = w_out[i].astype(BF16)
        h = _mix_out(ya, y, h, attn_out_norm[i], ssm_out_norm[i], ssm_w_glu[i].astype(BF16),
                     ssm_b_glu[i], wo[:attn_w], wo[attn_w:])
        h = _ffn(h, ffn2_norm[i], _ff_tiles(ffn2_w_gate[i], 1), _ff_tiles(ffn2_w_up[i], 1),
                 _ff_tiles(ffn2_w_down[i], 0, scale=0.5))
        h = _ple(h, p[i].reshape(b * s, -1), ple_norm[i], ple_w_gate[i].astype(BF16),
                 ple_w_proj[i].astype(BF16), final_norm, final=i == depth - 1)
    return h.reshape(b, s, d)
```

```python
import functools
import math

import jax
import jax.numpy as jnp
from jax import lax
from jax.experimental import pallas as pl
from jax.experimental.pallas import tpu as pltpu

NORM_EPS = 1e-6
MASK_VALUE = -1e30
HEAD_DIM = 64
ATTN_BLOCK = 128
ATTN_SPAN = 128
DILATIONS = (1, 4, 16)
ATTN_UNROLL = 16
SSM_GROUP = 16
SSM_STATE = 64
SSM_CHUNK = 16
TOKEN_RES = 16
LANES = 128
GROUPS_PER_BLOCK = LANES // SSM_GROUP
FF_TILE = 256
VMEM_CAP = 56 * 1024 * 1024

BF16 = jnp.bfloat16
F32 = jnp.float32


def _params(sem, vmem_bytes):
    return pltpu.CompilerParams(
        dimension_semantics=sem, vmem_limit_bytes=min(int(vmem_bytes), VMEM_CAP))


def _rms(x, g):
    ms = jnp.mean(x * x, axis=-1, keepdims=True)
    return x * lax.rsqrt(ms + NORM_EPS) * g


def _dot(a, b):
    return jnp.dot(a, b, preferred_element_type=F32)


def _const_spec(shape):
    return pl.BlockSpec(shape, lambda *_: (0,) * len(shape),
                        pipeline_mode=pl.Buffered(1))


def _ffn_kernel(x_ref, g_ref, wg_ref, wu_ref, wd_ref, o_ref, xn_ref):
    f = pl.program_id(1)

    @pl.when(f == 0)
    def _():
        xn_ref[...] = _rms(x_ref[...], g_ref[...]).astype(BF16)
        o_ref[...] = jnp.zeros_like(o_ref)

    xn = xn_ref[...]
    gate = _dot(xn, wg_ref[...])
    up = _dot(xn, wu_ref[...])
    act = (gate * jax.nn.sigmoid(gate) * up).astype(BF16)
    o_ref[...] += _dot(act, wd_ref[...])

    @pl.when(f == pl.num_programs(1) - 1)
    def _():
        o_ref[...] = x_ref[...] + 0.5 * o_ref[...]


def _ffn(h, g, wg, wu, wd, *, tm=1024):
    m, d = h.shape
    nf, _, tf = wg.shape
    tm = min(tm, m)
    vmem = (4 * tm * d * 4 + tm * d * 2 + 6 * d * tf * 2 + 6 * tm * tf * 4) * 1.25
    row = pl.BlockSpec((tm, d), lambda i, f: (i, 0))
    return pl.pallas_call(
        _ffn_kernel,
        grid=(m // tm, nf),
        in_specs=[
            row,
            pl.BlockSpec((1, d), lambda i, f: (0, 0)),
            pl.BlockSpec((None, d, tf), lambda i, f: (f, 0, 0)),
            pl.BlockSpec((None, d, tf), lambda i, f: (f, 0, 0)),
            pl.BlockSpec((None, tf, d), lambda i, f: (f, 0, 0)),
        ],
        out_specs=row,
        out_shape=jax.ShapeDtypeStruct((m, d), F32),
        scratch_shapes=[pltpu.VMEM((tm, d), BF16)],
        compiler_params=_params(("parallel", "arbitrary"), vmem),
        name="ffn",
    )(h, g.reshape(1, d), wg, wu, wd)


def _ff_tile_kernel(w_ref, o_ref, *, axis, size):
    w = w_ref[...]
    tile = w.shape[axis]
    pos = lax.broadcasted_iota(jnp.int32, w.shape, axis) + pl.program_id(0) * tile
    o_ref[...] = jnp.where(pos < size, w, 0.0).astype(o_ref.dtype)


def _ff_tiles(w, axis, tf=FF_TILE):
    size = w.shape[axis]
    nf = pl.cdiv(size, tf)
    block = list(w.shape)
    block[axis] = tf
    index = (lambda f: (0, f)) if axis == 1 else (lambda f: (f, 0))
    return pl.pallas_call(
        functools.partial(_ff_tile_kernel, axis=axis, size=size),
        grid=(nf,),
        in_specs=[pl.BlockSpec(tuple(block), index)],
        out_specs=pl.BlockSpec((None,) + tuple(block), lambda f: (f, 0, 0)),
        out_shape=jax.ShapeDtypeStruct((nf,) + tuple(block), BF16),
        compiler_params=_params(("parallel",), 6 * math.prod(block) * 4),
        name="ff_tiles",
    )(w)


def _residue_perm(n_res, tl, transpose):
    tm = n_res * tl
    assert tl & (tl - 1) == 0
    a = lax.broadcasted_iota(jnp.int32, (tm, tm), 1 if transpose else 0)
    b = lax.broadcasted_iota(jnp.int32, (tm, tm), 0 if transpose else 1)
    hit = b == n_res * (a & (tl - 1)) + (a >> (tl.bit_length() - 1))
    return jnp.where(hit, 1.0, 0.0).astype(BF16)


def _in_proj_kernel(x_ref, g_ref, w_ref, o_ref):
    n_res, tl, _ = o_ref.shape
    xn = _rms(x_ref[...], g_ref[...]).astype(BF16)
    xn = _dot(_residue_perm(n_res, tl, False), xn).astype(BF16)
    res = _dot(xn, w_ref[...])
    for r in range(n_res):
        o_ref[r] = res[r * tl:(r + 1) * tl]


def _in_proj(h, g, w, batch, *, tm=256):
    m, d = h.shape
    n = w.shape[1]
    seq = m // batch
    tm = min(tm, seq)
    tl = tm // TOKEN_RES
    tiles = seq // tm
    assert tl % 8 == 0 and seq % tm == 0
    vmem = (2 * tm * d * 4 + tm * d * 6 + d * n * 2 + 4 * tm * n * 4 + 4 * tm * tm) * 1.2
    return pl.pallas_call(
        _in_proj_kernel,
        grid=(m // tm,),
        in_specs=[
            pl.BlockSpec((tm, d), lambda i: (i, 0)),
            _const_spec((1, d)),
            _const_spec((d, n)),
        ],
        out_specs=pl.BlockSpec((None, TOKEN_RES, tl, n),
                               lambda i: (i // tiles, 0, i % tiles, 0)),
        out_shape=jax.ShapeDtypeStruct((batch, TOKEN_RES, seq // TOKEN_RES, n), F32),
        compiler_params=_params(("parallel",), vmem),
        name="in_proj",
    )(h, g.reshape(1, d), w)


def _attn_kernel(q_ref, k_ref, v_ref, o_ref, m_ref, l_ref, bias_ref, p_sc, mx_sc):
    n_res, length, _ = q_ref.shape
    blk = ATTN_BLOCK
    n_units = n_res * length // blk
    head0 = lax.broadcasted_iota(jnp.int32, (blk, LANES), 1) < HEAD_DIM
    kv_head0 = lax.broadcasted_iota(jnp.int32, (2 * blk, LANES), 1) < HEAD_DIM
    scale = HEAD_DIM ** -0.5 * math.log2(math.e)

    rho = lax.broadcasted_iota(jnp.int32, (blk, 2 * blk), 0)
    kap = lax.broadcasted_iota(jnp.int32, (blk, 2 * blk), 1)
    for pattern, d in enumerate(DILATIONS):
        n_sub = n_res // d
        c_rows = blk // n_sub
        sh = c_rows.bit_length() - 1
        iq, lq = rho >> sh, rho & (c_rows - 1)
        ik, lk = kap >> (sh + 1), kap & (2 * c_rows - 1)
        for shift in range(2):
            dist = n_sub * (lq - lk + shift * c_rows) + (iq - ik)
            bias_ref[2 * pattern + shift] = jnp.where(
                (dist >= 0) & (dist <= ATTN_SPAN), 0.0, MASK_VALUE)

    def probs(q, k, bias, sel):
        qh = jnp.where(sel, q, 0.0).astype(BF16)
        s = lax.dot_general(qh, k, (((1,), (1,)), ((), ())),
                            preferred_element_type=F32) + bias
        m = jnp.max(s, axis=-1, keepdims=True)
        return jnp.exp2(s - m).astype(BF16), m

    n_groups = n_units // ATTN_UNROLL
    for pattern, d in enumerate(DILATIONS):
        first = pattern == 0
        last = pattern == len(DILATIONS) - 1
        n_sub = n_res // d
        c_rows = blk // n_sub
        blocks_per_residue = n_units // d

        def gather(ref, res, start, size, d=d, n_sub=n_sub):
            return jnp.concatenate(
                [ref[d * i + res, pl.ds(start, size), :] for i in range(n_sub)], axis=0)

        def scatter(ref, res, start, val, d=d, n_sub=n_sub, c_rows=c_rows):
            for i in range(n_sub):
                ref[d * i + res, pl.ds(start, c_rows), :] = val[i * c_rows:(i + 1) * c_rows]

        def place(g, uu, c_rows=c_rows, blocks_per_residue=blocks_per_residue):
            u = g * ATTN_UNROLL + uu
            res = u // blocks_per_residue
            bq = u % blocks_per_residue
            kb = jnp.maximum(bq - 1, 0)
            return (res, pl.multiple_of(c_rows * bq, 8), pl.multiple_of(c_rows * kb, 8), bq - kb)

        def score_stage(g, slot, pattern=pattern, c_rows=c_rows, gather=gather, place=place):
            loaded = []
            for uu in range(ATTN_UNROLL):
                res, q0, k0, shift = place(g, uu)
                loaded.append((gather(q_ref, res, q0, c_rows) * scale,
                               gather(k_ref, res, k0, 2 * c_rows).astype(BF16),
                               bias_ref[2 * pattern + shift]))
            for uu, (q, k, bias) in enumerate(loaded):
                p0, m0 = probs(q, k, bias, head0)
                p1, m1 = probs(q, k, bias, jnp.logical_not(head0))
                p_sc[slot, uu, 0] = p0
                p_sc[slot, uu, 1] = p1
                mx_sc[slot, uu] = jnp.where(head0, m0, m1)

        def value_stage(g, slot, first=first, last=last, c_rows=c_rows,
                        gather=gather, scatter=scatter, place=place):
            loaded = []
            for uu in range(ATTN_UNROLL):
                res, q0, k0, _ = place(g, uu)
                old = None if first else tuple(
                    gather(ref, res, q0, c_rows) for ref in (o_ref, m_ref, l_ref))
                loaded.append((res, q0, gather(v_ref, res, k0, 2 * c_rows).astype(BF16),
                               p_sc[slot, uu, 0], p_sc[slot, uu, 1], mx_sc[slot, uu], old))
            results = []
            for res, q0, v, p0, p1, m_new, old in loaded:
                ones = jnp.ones_like(v)
                o0 = _dot(p0, jnp.where(kv_head0, v, ones))
                o1 = _dot(p1, jnp.where(kv_head0, ones, v))
                o_new = jnp.where(head0, o0, o1)
                l_new = pltpu.roll(jnp.where(head0, o1, o0), HEAD_DIM, 1)
                if not first:
                    o_old, m_old, l_old = old
                    m_tot = jnp.maximum(m_old, m_new)
                    w_old = jnp.exp2(m_old - m_tot)
                    w_new = jnp.exp2(m_new - m_tot)
                    o_new = w_old * o_old + w_new * o_new
                    l_new = w_old * l_old + w_new * l_new
                    m_new = m_tot
                results.append((res, q0, o_new, m_new, l_new))
            for res, q0, o_new, m_new, l_new in results:
                if last:
                    scatter(o_ref, res, q0, o_new / l_new)
                else:
                    scatter(o_ref, res, q0, o_new)
                    scatter(m_ref, res, q0, m_new)
                    scatter(l_ref, res, q0, l_new)

        def step(g, carry, score_stage=score_stage, value_stage=value_stage):
            slot = g & 1
            value_stage(g - 1, 1 - slot)
            score_stage(g, slot)
            return carry

        score_stage(0, 0)
        lax.fori_loop(1, n_groups, step, 0)
        value_stage(n_groups - 1, (n_groups - 1) & 1)


def _attention(z, n_pairs, *, q_col, k_col, v_col):
    b, n_res, length, _ = z.shape
    assert n_res == TOKEN_RES == DILATIONS[-1] and all(n_res % d == 0 for d in DILATIONS)
    assert length % ATTN_BLOCK == 0 and length >= 2 * ATTN_BLOCK and ATTN_BLOCK // n_res >= 8
    assert (n_res * length // ATTN_BLOCK) % ATTN_UNROLL == 0

    def spec(col):
        return pl.BlockSpec((None, n_res, length, LANES), lambda bi, h: (bi, 0, 0, col + h))

    slab = n_res * length * LANES * 4
    vmem = 10 * slab * 1.15 + (8 << 20)
    return pl.pallas_call(
        _attn_kernel,
        grid=(b, n_pairs),
        in_specs=[spec(q_col), spec(k_col), spec(v_col)],
        out_specs=spec(0),
        out_shape=jax.ShapeDtypeStruct((b, n_res, length, n_pairs * LANES), F32),
        scratch_shapes=[pltpu.VMEM((n_res, length, LANES), F32),
                        pltpu.VMEM((n_res, length, LANES), F32),
                        pltpu.VMEM((2 * len(DILATIONS), ATTN_BLOCK, 2 * ATTN_BLOCK), F32),
                        pltpu.VMEM((2, ATTN_UNROLL, 2, ATTN_BLOCK, 2 * ATTN_BLOCK), BF16),
                        pltpu.VMEM((2, ATTN_UNROLL, ATTN_BLOCK, LANES), F32)],
        compiler_params=_params(("parallel", "parallel"), vmem),
        name="attention",
    )(z, z, z)


def _ssm_weights(lam_re, lam_im, log_dt, b_re, b_im, c_re, c_im, d_skip, n_chunks):
    hp = lax.Precision.HIGHEST
    g, p = lam_re.shape
    c = b_re.shape[-1]
    t = SSM_CHUNK
    gb = GROUPS_PER_BLOCK
    nj = g // gb
    lr, li = lam_re.astype(F32), lam_im.astype(F32)
    dt = jnp.exp(log_dt.astype(F32))[:, None]
    mag = jnp.exp(lr * dt)
    ar = mag * jnp.cos(li * dt)
    ai = mag * jnp.sin(li * dt)
    nr, ni = ar - 1.0, ai
    den = lr * lr + li * li
    cr = (nr * lr + ni * li) / den
    ci = (ni * lr - nr * li) / den
    br, bi = b_re.astype(F32), b_im.astype(F32)
    bbr = cr[..., None] * br - ci[..., None] * bi
    bbi = cr[..., None] * bi + ci[..., None] * br

    def power(k):
        kk = k.astype(F32)[:, None, None]
        pm = jnp.exp(kk * (lr * dt))
        ang = kk * (li * dt)
        return pm * jnp.cos(ang), pm * jnp.sin(ang)

    pr, pi = power(jnp.arange(t + 1))
    cre, cim = c_re.astype(F32), c_im.astype(F32)
    car = cre[None] * pr[:, :, None, :] - cim[None] * pi[:, :, None, :]
    cai = cre[None] * pi[:, :, None, :] + cim[None] * pr[:, :, None, :]
    kern = (jnp.einsum('kgdp,gpc->kgdc', car[:t], bbr, precision=hp)
            - jnp.einsum('kgdp,gpc->kgdc', cai[:t], bbi, precision=hp))
    kern = kern.at[0].add(d_skip.astype(F32).reshape(g, c)[:, :, None] * jnp.eye(c, dtype=F32))

    kr = kern.reshape(t, nj, gb, c, c).transpose(1, 0, 4, 2, 3).reshape(nj, t, c, gb * c)

    rev_r, rev_i = pr[:t][::-1], pi[:t][::-1]
    abr = rev_r[..., None] * bbr[None] - rev_i[..., None] * bbi[None]
    abi = rev_r[..., None] * bbi[None] + rev_i[..., None] * bbr[None]

    def rows_form(x):
        x = x.reshape(t, nj, gb, p, c).transpose(1, 0, 2, 4, 3).reshape(nj, t * gb * c, p)
        return jnp.concatenate([x, x], axis=-1).astype(BF16)

    def cols_form(x):
        x = x.reshape(t, nj, gb, c, p).transpose(1, 4, 0, 2, 3).reshape(nj, p, t * gb * c)
        return jnp.concatenate([x, x], axis=1).astype(BF16)

    n_steps = max(1, (n_chunks - 1).bit_length())
    sr, si = power(t * (2 ** jnp.arange(n_steps)))
    rows = -(-n_steps // 8) * 8
    sr = jnp.pad(sr.reshape(n_steps, nj, gb * p).transpose(1, 0, 2), ((0, 0), (0, rows - n_steps), (0, 0)))
    si = jnp.pad(si.reshape(n_steps, nj, gb * p).transpose(1, 0, 2), ((0, 0), (0, rows - n_steps), (0, 0)))
    return (kr.astype(BF16), rows_form(abr), rows_form(abi),
            cols_form(car[1:]), cols_form(-cai[1:]), sr, si)


def _ssm_expand(kr_ref, abr_ref, abi_ref, cmr_ref, cmi_ref, m_sc, b_sc, c_sc):
    t, c, gb, p = SSM_CHUNK, SSM_GROUP, GROUPS_PER_BLOCK, SSM_STATE
    shift = c.bit_length() - 1
    half = gb * p
    pairs = LANES // p

    def group_of(idx):
        return (idx >> shift) & (gb - 1)

    m_sc[...] = jnp.zeros_like(m_sc)
    col_g = group_of(lax.broadcasted_iota(jnp.int32, (c, LANES), 1))
    for k in range(t):
        kk = kr_ref[k]
        dk = jnp.concatenate([jnp.where(col_g == g, kk, jnp.zeros_like(kk)) for g in range(gb)], axis=0)
        for tau in range(t - k):
            m_sc[tau * LANES:(tau + 1) * LANES, (tau + k) * LANES:(tau + k + 1) * LANES] = dk

    row_g = group_of(lax.broadcasted_iota(jnp.int32, (t * LANES, LANES), 0))
    lane_half = lax.broadcasted_iota(jnp.int32, (t * LANES, LANES), 1) >> (p.bit_length() - 1)
    for part, ab_ref in enumerate((abr_ref, abi_ref)):
        ab = ab_ref[...]
        for q in range(gb // pairs):
            b_sc[:, part * half + q * LANES:part * half + (q + 1) * LANES] = jnp.where(
                row_g == pairs * q + lane_half, ab, jnp.zeros_like(ab))

    col_g2 = group_of(lax.broadcasted_iota(jnp.int32, (LANES, t * LANES), 1))
    row_half = lax.broadcasted_iota(jnp.int32, (LANES, t * LANES), 0) >> (p.bit_length() - 1)
    for part, cm_ref in enumerate((cmr_ref, cmi_ref)):
        cm = cm_ref[...]
        for q in range(gb // pairs):
            c_sc[part * half + q * LANES:part * half + (q + 1) * LANES, :] = jnp.where(
                col_g2 == pairs * q + row_half, cm, jnp.zeros_like(cm))


def _ssm_kernel(s_ref, kr_ref, abr_ref, abi_ref, cmr_ref, cmi_ref, sr_ref, si_ref, y_ref,
                m_sc, b_sc, c_sc):
    t, n_chunks, _ = s_ref.shape
    half = GROUPS_PER_BLOCK * SSM_STATE

    @pl.when(pl.program_id(1) == 0)
    def _():
        _ssm_expand(kr_ref, abr_ref, abi_ref, cmr_ref, cmi_ref, m_sc, b_sc, c_sc)

    u = jnp.concatenate([s_ref[tau].astype(BF16) for tau in range(t)], axis=1)
    y = _dot(u, m_sc[...])
    x = _dot(u, b_sc[...])
    hr, hi = x[:, :half], x[:, half:]

    row = lax.broadcasted_iota(jnp.int32, (n_chunks, half), 0)

    def shifted(h, k):
        return jnp.where(row >= k, pltpu.roll(h, k, 0), 0.0)

    step = 0
    while (1 << step) < n_chunks:
        k = 1 << step
        pr = sr_ref[step:step + 1, :]
        pi = si_ref[step:step + 1, :]
        zr, zi = shifted(hr, k), shifted(hi, k)
        hr, hi = hr + pr * zr - pi * zi, hi + pr * zi + pi * zr
        step += 1

    h_prev = jnp.concatenate([shifted(hr, 1), shifted(hi, 1)], axis=1).astype(BF16)
    y = jax.nn.gelu(y + _dot(h_prev, c_sc[...]))
    for tau in range(t):
        y_ref[tau] = y[:, tau * LANES:(tau + 1) * LANES]


def _ssm(z, s_col, weights):
    b, n_res, n_chunks, _ = z.shape
    nj = weights[0].shape[0]
    assert n_res == SSM_CHUNK
    assert SSM_GROUP & (SSM_GROUP - 1) == 0 and SSM_STATE & (SSM_STATE - 1) == 0

    def wspec(a):
        nd = a.ndim - 1
        return pl.BlockSpec((None,) + a.shape[1:], lambda j, bi: (j,) + (0,) * nd)

    s = n_res * n_chunks
    width = SSM_CHUNK * LANES
    state = 2 * GROUPS_PER_BLOCK * SSM_STATE
    scratch = (width * width + 2 * width * state) * 2
    wbytes = 2 * sum(math.prod(a.shape[1:]) * a.dtype.itemsize for a in weights)
    vmem = (4 * s * LANES * 4 + wbytes + scratch + n_chunks * (width * 10 + state * 12)) * 1.2
    return pl.pallas_call(
        _ssm_kernel,
        grid=(nj, b),
        in_specs=[pl.BlockSpec((None, n_res, n_chunks, LANES),
                               lambda j, bi: (bi, 0, 0, s_col + j))]
        + [wspec(a) for a in weights],
        out_specs=pl.BlockSpec((None, n_res, n_chunks, LANES), lambda j, bi: (bi, 0, 0, j)),
        out_shape=jax.ShapeDtypeStruct((b, n_res, n_chunks, nj * LANES), F32),
        scratch_shapes=[pltpu.VMEM((width, width), BF16), pltpu.VMEM((width, state), BF16),
                        pltpu.VMEM((state, width), BF16)],
        compiler_params=_params(("parallel", "arbitrary"), vmem),
        name="ssm",
    )(z, *weights)


def _mix_out_kernel(ya_ref, y_ref, h_ref, ga_ref, gb_ref, wglu_ref, bglu_ref,
                    wa_ref, wb_ref, o_ref):
    n_res, tl, _ = ya_ref.shape
    y = y_ref[...].reshape(n_res * tl, -1)
    gate = _dot(y.astype(BF16), wglu_ref[...]) + bglu_ref[...]
    yb = y * jax.nn.sigmoid(gate)
    na = _rms(ya_ref[...].reshape(n_res * tl, -1), ga_ref[...]).astype(BF16)
    nb = _rms(yb, gb_ref[...]).astype(BF16)
    perm = _residue_perm(n_res, tl, True)
    na = _dot(perm, na).astype(BF16)
    nb = _dot(perm, nb).astype(BF16)
    o_ref[...] = h_ref[...] + _dot(na, wa_ref[...]) + _dot(nb, wb_ref[...])


def _mix_out(ya, y, h, ga, gb, wglu, bglu, wa, wb, *, tm=512):
    m, d = h.shape
    batch, n_res, length, wa_w = ya.shape
    wb_w = y.shape[-1]
    tm = min(tm, n_res * length)
    tl = tm // n_res
    tiles = length // tl
    assert tl % 8 == 0 and length % tl == 0
    vmem = (2 * tm * (wa_w + wb_w + 2 * d) * 4 + (wglu.size + wa.size + wb.size) * 2
            + 8 * tm * d * 4) * 1.2
    row = pl.BlockSpec((tm, d), lambda i: (i, 0))
    res_major = lambda w: pl.BlockSpec((None, n_res, tl, w),
                                       lambda i: (i // tiles, 0, i % tiles, 0))
    return pl.pallas_call(
        _mix_out_kernel,
        grid=(m // tm,),
        in_specs=[res_major(wa_w), res_major(wb_w), row,
                  _const_spec((1, wa_w)), _const_spec((1, wb_w)),
                  _const_spec(wglu.shape), _const_spec((1, wb_w)),
                  _const_spec(wa.shape), _const_spec(wb.shape)],
        out_specs=row,
        out_shape=jax.ShapeDtypeStruct((m, d), F32),
        compiler_params=_params(("parallel",), vmem),
        name="mix_out",
    )(ya, y, h, ga.reshape(1, -1), gb.reshape(1, -1), wglu, bglu.reshape(1, -1), wa, wb)


def _ple_kernel(h_ref, p_ref, gn_ref, wg_ref, wp_ref, fn_ref, o_ref, *, final):
    h = h_ref[...]
    gate = jax.nn.sigmoid(_dot(_rms(h, gn_ref[...]).astype(BF16), wg_ref[...]))
    proj = _dot(p_ref[...].astype(BF16), wp_ref[...])
    out = h + gate * proj
    o_ref[...] = _rms(out, fn_ref[...]) if final else out


def _ple(h, p, gn, wg, wp, fn, *, final, tm=512):
    m, d = h.shape
    pd = p.shape[1]
    tm = min(tm, m)
    vmem = (2 * tm * (2 * d + pd) * 4 + (wg.size + wp.size) * 2 + 8 * tm * d * 4) * 1.2
    row = lambda w: pl.BlockSpec((tm, w), lambda i: (i, 0))
    return pl.pallas_call(
        functools.partial(_ple_kernel, final=final),
        grid=(m // tm,),
        in_specs=[row(d), row(pd), _const_spec((1, d)), _const_spec(wg.shape),
                  _const_spec(wp.shape), _const_spec((1, d))],
        out_specs=row(d),
        out_shape=jax.ShapeDtypeStruct((m, d), F32),
        compiler_params=_params(("parallel",), vmem),
        name="ple",
    )(h, p, gn.reshape(1, d), wg, wp, fn.reshape(1, d))


def kernel(x, p, ffn1_norm, ffn1_w_gate, ffn1_w_up, ffn1_w_down, mix_norm, w_in, attn_out_norm, ssm_lambda_re, ssm_lambda_im, ssm_log_dt, ssm_b_re, ssm_b_im, ssm_c_re, ssm_c_im, ssm_d, ssm_w_glu, ssm_b_glu, ssm_out_norm, w_out, ffn2_norm, ffn2_w_gate, ffn2_w_up, ffn2_w_down, ple_norm, ple_w_gate, ple_w_proj, final_norm):
    b, s, d = x.shape
    depth = p.shape[0]
    attn_w = attn_out_norm.shape[1]
    ssm_w = ssm_out_norm.shape[1]
    n_pairs = attn_w // LANES
    h = x.reshape(b * s, d)
    for i in range(depth):
        h = _ffn(h, ffn1_norm[i], _ff_tiles(ffn1_w_gate[i], 1), _ff_tiles(ffn1_w_up[i], 1),
                 _ff_tiles(ffn1_w_down[i], 0))
        z = _in_proj(h, mix_norm[i], w_in[i].astype(BF16), b)
        ya = _attention(z, n_pairs, q_col=0, k_col=n_pairs, v_col=2 * n_pairs)
        weights = _ssm_weights(ssm_lambda_re[i], ssm_lambda_im[i], ssm_log_dt[i],
                               ssm_b_re[i], ssm_b_im[i], ssm_c_re[i], ssm_c_im[i],
                               ssm_d[i], s // SSM_CHUNK)
        y = _ssm(z, 3 * n_pairs, weights)
        wo = w_out[i].astype(BF16)
        h = _mix_out(ya, y, h, attn_out_norm[i], ssm_out_norm[i], ssm_w_glu[i].astype(BF16),
                     ssm_b_glu[i], wo[:attn_w], wo[attn_w:])
        h = _ffn(h, ffn2_norm[i], _ff_tiles(ffn2_w_gate[i], 1), _ff_tiles(ffn2_w_up[i], 1),
                 _ff_tiles(ffn2_w_down[i], 0))
        h = _ple(h, p[i].reshape(b * s, -1), ple_norm[i], ple_w_gate[i].astype(BF16),
                 ple_w_proj[i].astype(BF16), final_norm, final=i == depth - 1)
    return h.reshape(b, s, d)
```

```python
import functools
import math

import jax
import jax.numpy as jnp
from jax import lax
from jax.experimental import pallas as pl
from jax.experimental.pallas import tpu as pltpu

NORM_EPS = 1e-6
MASK_VALUE = -1e30
HEAD_DIM = 64
ATTN_BLOCK = 128
ATTN_SPAN = 128
DILATIONS = (1, 4, 16)
ATTN_UNROLL = 16
SSM_GROUP = 16
SSM_STATE = 64
SSM_CHUNK = 16
TOKEN_RES = 16
LANES = 128
GROUPS_PER_BLOCK = LANES // SSM_GROUP
FF_TILE = 256
VMEM_CAP = 56 * 1024 * 1024

BF16 = jnp.bfloat16
F32 = jnp.float32


def _params(sem, vmem_bytes):
    return pltpu.CompilerParams(
        dimension_semantics=sem, vmem_limit_bytes=min(int(vmem_bytes), VMEM_CAP))


def _rms(x, g):
    ms = jnp.mean(x * x, axis=-1, keepdims=True)
    return x * lax.rsqrt(ms + NORM_EPS) * g


def _dot(a, b):
    return jnp.dot(a, b, preferred_element_type=F32)


def _const_spec(shape):
    return pl.BlockSpec(shape, lambda *_: (0,) * len(shape),
                        pipeline_mode=pl.Buffered(1))


def _ffn_kernel(x_ref, g_ref, wg_ref, wu_ref, wd_ref, o_ref, xn_ref):
    f = pl.program_id(1)

    @pl.when(f == 0)
    def _():
        xn_ref[...] = _rms(x_ref[...], g_ref[...]).astype(BF16)
        o_ref[...] = jnp.zeros_like(o_ref)

    xn = xn_ref[...]
    gate = _dot(xn, wg_ref[...])
    up = _dot(xn, wu_ref[...])
    act = (gate * jax.nn.sigmoid(gate) * up).astype(BF16)
    o_ref[...] += _dot(act, wd_ref[...])

    @pl.when(f == pl.num_programs(1) - 1)
    def _():
        o_ref[...] = x_ref[...] + 0.5 * o_ref[...]


def _ffn(h, g, wg, wu, wd, *, tm=1024):
    m, d = h.shape
    nf, _, tf = wg.shape
    tm = min(tm, m)
    vmem = (4 * tm * d * 4 + tm * d * 2 + 6 * d * tf * 2 + 6 * tm * tf * 4) * 1.25
    row = pl.BlockSpec((tm, d), lambda i, f: (i, 0))
    return pl.pallas_call(
        _ffn_kernel,
        grid=(m // tm, nf),
        in_specs=[
            row,
            pl.BlockSpec((1, d), lambda i, f: (0, 0)),
            pl.BlockSpec((None, d, tf), lambda i, f: (f, 0, 0)),
            pl.BlockSpec((None, d, tf), lambda i, f: (f, 0, 0)),
            pl.BlockSpec((None, tf, d), lambda i, f: (f, 0, 0)),
        ],
        out_specs=row,
        out_shape=jax.ShapeDtypeStruct((m, d), F32),
        scratch_shapes=[pltpu.VMEM((tm, d), BF16)],
        compiler_params=_params(("parallel", "arbitrary"), vmem),
        name="ffn",
    )(h, g.reshape(1, d), wg, wu, wd)


def _ff_tile_kernel(w_ref, o_ref, *, axis, size):
    w = w_ref[...]
    tile = w.shape[axis]
    pos = lax.broadcasted_iota(jnp.int32, w.shape, axis) + pl.program_id(0) * tile
    o_ref[...] = jnp.where(pos < size, w, 0.0).astype(o_ref.dtype)


def _ff_tiles(w, axis, tf=FF_TILE):
    size = w.shape[axis]
    nf = pl.cdiv(size, tf)
    block = list(w.shape)
    block[axis] = tf
    index = (lambda f: (0, f)) if axis == 1 else (lambda f: (f, 0))
    return pl.pallas_call(
        functools.partial(_ff_tile_kernel, axis=axis, size=size),
        grid=(nf,),
        in_specs=[pl.BlockSpec(tuple(block), index)],
        out_specs=pl.BlockSpec((None,) + tuple(block), lambda f: (f, 0, 0)),
        out_shape=jax.ShapeDtypeStruct((nf,) + tuple(block), BF16),
        compiler_params=_params(("parallel",), 6 * math.prod(block) * 4),
        name="ff_tiles",
    )(w)


def _residue_perm(n_res, tl, transpose):
    tm = n_res * tl
    assert tl & (tl - 1) == 0
    a = lax.broadcasted_iota(jnp.int32, (tm, tm), 1 if transpose else 0)
    b = lax.broadcasted_iota(jnp.int32, (tm, tm), 0 if transpose else 1)
    hit = b == n_res * (a & (tl - 1)) + (a >> (tl.bit_length() - 1))
    return jnp.where(hit, 1.0, 0.0).astype(BF16)


def _in_proj_kernel(x_ref, g_ref, w_ref, o_ref):
    n_res, tl, _ = o_ref.shape
    xn = _rms(x_ref[...], g_ref[...]).astype(BF16)
    xn = _dot(_residue_perm(n_res, tl, False), xn).astype(BF16)
    res = _dot(xn, w_ref[...])
    for r in range(n_res):
        o_ref[r] = res[r * tl:(r + 1) * tl]


def _in_proj(h, g, w, batch, *, tm=256):
    m, d = h.shape
    n = w.shape[1]
    seq = m // batch
    tm = min(tm, seq)
    tl = tm // TOKEN_RES
    tiles = seq // tm
    assert tl % 8 == 0 and seq % tm == 0
    vmem = (2 * tm * d * 4 + tm * d * 6 + d * n * 2 + 4 * tm * n * 4 + 4 * tm * tm) * 1.2
    return pl.pallas_call(
        _in_proj_kernel,
        grid=(m // tm,),
        in_specs=[
            pl.BlockSpec((tm, d), lambda i: (i, 0)),
            _const_spec((1, d)),
            _const_spec((d, n)),
        ],
        out_specs=pl.BlockSpec((None, TOKEN_RES, tl, n),
                               lambda i: (i // tiles, 0, i % tiles, 0)),
        out_shape=jax.ShapeDtypeStruct((batch, TOKEN_RES, seq // TOKEN_RES, n), F32),
        compiler_params=_params(("parallel",), vmem),
        name="in_proj",
    )(h, g.reshape(1, d), w)


def _attn_kernel(q_ref, k_ref, v_ref, o_ref, m_ref, l_ref, bias_ref, p_sc, mx_sc):
    n_res, length, _ = q_ref.shape
    blk = ATTN_BLOCK
    n_units = n_res * length // blk
    head0 = lax.broadcasted_iota(jnp.int32, (blk, LANES), 1) < HEAD_DIM
    kv_head0 = lax.broadcasted_iota(jnp.int32, (2 * blk, LANES), 1) < HEAD_DIM
    scale = HEAD_DIM ** -0.5 * math.log2(math.e)

    rho = lax.broadcasted_iota(jnp.int32, (blk, 2 * blk), 0)
    kap = lax.broadcasted_iota(jnp.int32, (blk, 2 * blk), 1)
    for pattern, d in enumerate(DILATIONS):
        n_sub = n_res // d
        c_rows = blk // n_sub
        sh = c_rows.bit_length() - 1
        iq, lq = rho >> sh, rho & (c_rows - 1)
        ik, lk = kap >> (sh + 1), kap & (2 * c_rows - 1)
        for shift in range(2):
            dist = n_sub * (lq - lk + shift * c_rows) + (iq - ik)
            bias_ref[2 * pattern + shift] = jnp.where(
                (dist >= 0) & (dist <= ATTN_SPAN), 0.0, MASK_VALUE)

    def probs(q, k, bias, sel):
        qh = jnp.where(sel, q, 0.0).astype(BF16)
        s = lax.dot_general(qh, k, (((1,), (1,)), ((), ())),
                            preferred_element_type=F32) + bias
        m = jnp.max(s, axis=-1, keepdims=True)
        return jnp.exp2(s - m).astype(BF16), m

    n_groups = n_units // ATTN_UNROLL
    for pattern, d in enumerate(DILATIONS):
        first = pattern == 0
        last = pattern == len(DILATIONS) - 1
        n_sub = n_res // d
        c_rows = blk // n_sub
        blocks_per_residue = n_units // d

        def gather(ref, res, start, size, d=d, n_sub=n_sub):
            return jnp.concatenate(
                [ref[d * i + res, pl.ds(start, size), :] for i in range(n_sub)], axis=0)

        def scatter(ref, res, start, val, d=d, n_sub=n_sub, c_rows=c_rows):
            for i in range(n_sub):
                ref[d * i + res, pl.ds(start, c_rows), :] = val[i * c_rows:(i + 1) * c_rows]

        def place(g, uu, c_rows=c_rows, blocks_per_residue=blocks_per_residue):
            u = g * ATTN_UNROLL + uu
            res = u // blocks_per_residue
            bq = u % blocks_per_residue
            kb = jnp.maximum(bq - 1, 0)
            return (res, pl.multiple_of(c_rows * bq, 8), pl.multiple_of(c_rows * kb, 8), bq - kb)

        def score_stage(g, slot, pattern=pattern, c_rows=c_rows, gather=gather, place=place):
            loaded = []
            for uu in range(ATTN_UNROLL):
                res, q0, k0, shift = place(g, uu)
                loaded.append((gather(q_ref, res, q0, c_rows) * scale,
                               gather(k_ref, res, k0, 2 * c_rows).astype(BF16),
                               bias_ref[2 * pattern + shift]))
            for uu, (q, k, bias) in enumerate(loaded):
                p0, m0 = probs(q, k, bias, head0)
                p1, m1 = probs(q, k, bias, jnp.logical_not(head0))
                p_sc[slot, uu, 0] = p0
                p_sc[slot, uu, 1] = p1
                mx_sc[slot, uu] = jnp.where(head0, m0, m1)

        def value_stage(g, slot, first=first, last=last, c_rows=c_rows,
                        gather=gather, scatter=scatter, place=place):
            loaded = []
            for uu in range(ATTN_UNROLL):
                res, q0, k0, _ = place(g, uu)
                old = None if first else tuple(
                    gather(ref, res, q0, c_rows) for ref in (o_ref, m_ref, l_ref))
                loaded.append((res, q0, gather(v_ref, res, k0, 2 * c_rows).astype(BF16),
                               p_sc[slot, uu, 0], p_sc[slot, uu, 1], mx_sc[slot, uu], old))
            results = []
            for res, q0, v, p0, p1, m_new, old in loaded:
                ones = jnp.ones_like(v)
                o0 = _dot(p0, jnp.where(kv_head0, v, ones))
                o1 = _dot(p1, jnp.where(kv_head0, ones, v))
                o_new = jnp.where(head0, o0, o1)
                l_new = pltpu.roll(jnp.where(head0, o1, o0), HEAD_DIM, 1)
                if not first:
                    o_old, m_old, l_old = old
                    m_tot = jnp.maximum(m_old, m_new)
                    w_old = jnp.exp2(m_old - m_tot)
                    w_new = jnp.exp2(m_new - m_tot)
                    o_new = w_old * o_old + w_new * o_new
                    l_new = w_old * l_old + w_new * l_new
                    m_new = m_tot
                results.append((res, q0, o_new, m_new, l_new))
            for res, q0, o_new, m_new, l_new in results:
                if last:
                    scatter(o_ref, res, q0, o_new / l_new)
                else:
                    scatter(o_ref, res, q0, o_new)
                    scatter(m_ref, res, q0, m_new)
                    scatter(l_ref, res, q0, l_new)

        def step(g, carry, score_stage=score_stage, value_stage=value_stage):
            slot = g & 1
            value_stage(g - 1, 1 - slot)
            score_stage(g, slot)
            return carry

        score_stage(0, 0)
        lax.fori_loop(1, n_groups, step, 0)
        value_stage(n_groups - 1, (n_groups - 1) & 1)


def _attention(z, n_pairs, *, q_col, k_col, v_col):
    b, n_res, length, _ = z.shape
    assert n_res == TOKEN_RES == DILATIONS[-1] and all(n_res % d == 0 for d in DILATIONS)
    assert length % ATTN_BLOCK == 0 and length >= 2 * ATTN_BLOCK and ATTN_BLOCK // n_res >= 8
    assert (n_res * length // ATTN_BLOCK) % ATTN_UNROLL == 0

    def spec(col):
        return pl.BlockSpec((None, n_res, length, LANES), lambda bi, h: (bi, 0, 0, col + h))

    slab = n_res * length * LANES * 4
    vmem = 10 * slab * 1.15 + (8 << 20)
    return pl.pallas_call(
        _attn_kernel,
        grid=(b, n_pairs),
        in_specs=[spec(q_col), spec(k_col), spec(v_col)],
        out_specs=spec(0),
        out_shape=jax.ShapeDtypeStruct((b, n_res, length, n_pairs * LANES), F32),
        scratch_shapes=[pltpu.VMEM((n_res, length, LANES), F32),
                        pltpu.VMEM((n_res, length, LANES), F32),
                        pltpu.VMEM((2 * len(DILATIONS), ATTN_BLOCK, 2 * ATTN_BLOCK), F32),
                        pltpu.VMEM((2, ATTN_UNROLL, 2, ATTN_BLOCK, 2 * ATTN_BLOCK), BF16),
                        pltpu.VMEM((2, ATTN_UNROLL, ATTN_BLOCK, LANES), F32)],
        compiler_params=_params(("parallel", "parallel"), vmem),
        name="attention",
    )(z, z, z)


def _ssm_weights(lam_re, lam_im, log_dt, b_re, b_im, c_re, c_im, d_skip, n_chunks):
    hp = lax.Precision.HIGHEST
    g, p = lam_re.shape
    c = b_re.shape[-1]
    t = SSM_CHUNK
    gb = GROUPS_PER_BLOCK
    nj = g // gb
    lr, li = lam_re.astype(F32), lam_im.astype(F32)
    dt = jnp.exp(log_dt.astype(F32))[:, None]
    mag = jnp.exp(lr * dt)
    ar = mag * jnp.cos(li * dt)
    ai = mag * jnp.sin(li * dt)
    nr, ni = ar - 1.0, ai
    den = lr * lr + li * li
    cr = (nr * lr + ni * li) / den
    ci = (ni * lr - nr * li) / den
    br, bi = b_re.astype(F32), b_im.astype(F32)
    bbr = cr[..., None] * br - ci[..., None] * bi
    bbi = cr[..., None] * bi + ci[..., None] * br

    def power(k):
        kk = k.astype(F32)[:, None, None]
        pm = jnp.exp(kk * (lr * dt))
        ang = kk * (li * dt)
        return pm * jnp.cos(ang), pm * jnp.sin(ang)

    pr, pi = power(jnp.arange(t + 1))
    cre, cim = c_re.astype(F32), c_im.astype(F32)
    car = cre[None] * pr[:, :, None, :] - cim[None] * pi[:, :, None, :]
    cai = cre[None] * pi[:, :, None, :] + cim[None] * pr[:, :, None, :]
    kern = jnp.einsum('kgdq,gqc->kgdc', jnp.concatenate([car[:t], -cai[:t]], axis=-1),
                      jnp.concatenate([bbr, bbi], axis=1), precision=hp)
    kern = kern.at[0].add(d_skip.astype(F32).reshape(g, c)[:, :, None] * jnp.eye(c, dtype=F32))

    kr = kern.reshape(t, nj, gb, c, c).transpose(1, 0, 4, 2, 3).reshape(nj, t, c, gb * c)

    rev_r, rev_i = pr[:t][::-1][:, :, None, :], pi[:t][::-1][:, :, None, :]
    bbr_t, bbi_t = bbr.transpose(0, 2, 1)[None], bbi.transpose(0, 2, 1)[None]
    abr = rev_r * bbr_t - rev_i * bbi_t
    abi = rev_r * bbi_t + rev_i * bbr_t

    def block_rows(x):
        return x.reshape(t, nj, gb, c, p).transpose(1, 0, 2, 3, 4).reshape(nj, t * gb * c, p)

    def rows_form(x):
        x = block_rows(x)
        return jnp.concatenate([x, x], axis=-1).astype(BF16)

    def cols_form(x):
        x = block_rows(x).transpose(0, 2, 1)
        return jnp.concatenate([x, x], axis=1).astype(BF16)

    n_steps = max(1, (n_chunks - 1).bit_length())
    sr, si = power(t * (2 ** jnp.arange(n_steps)))
    rows = -(-n_steps // 8) * 8
    sr = jnp.pad(sr.reshape(n_steps, nj, gb * p).transpose(1, 0, 2), ((0, 0), (0, rows - n_steps), (0, 0)))
    si = jnp.pad(si.reshape(n_steps, nj, gb * p).transpose(1, 0, 2), ((0, 0), (0, rows - n_steps), (0, 0)))
    return (kr.astype(BF16), rows_form(abr), rows_form(abi),
            cols_form(car[1:]), cols_form(-cai[1:]), sr, si)


def _ssm_expand(kr_ref, abr_ref, abi_ref, cmr_ref, cmi_ref, m_sc, b_sc, c_sc):
    t, c, gb, p = SSM_CHUNK, SSM_GROUP, GROUPS_PER_BLOCK, SSM_STATE
    shift = c.bit_length() - 1
    half = gb * p
    pairs = LANES // p

    def group_of(idx):
        return (idx >> shift) & (gb - 1)

    m_sc[...] = jnp.zeros_like(m_sc)
    col_g = group_of(lax.broadcasted_iota(jnp.int32, (c, LANES), 1))
    for k in range(t):
        kk = kr_ref[k]
        dk = jnp.concatenate([jnp.where(col_g == g, kk, jnp.zeros_like(kk)) for g in range(gb)], axis=0)
        for tau in range(t - k):
            m_sc[tau * LANES:(tau + 1) * LANES, (tau + k) * LANES:(tau + k + 1) * LANES] = dk

    row_g = group_of(lax.broadcasted_iota(jnp.int32, (t * LANES, LANES), 0))
    lane_half = lax.broadcasted_iota(jnp.int32, (t * LANES, LANES), 1) >> (p.bit_length() - 1)
    for part, ab_ref in enumerate((abr_ref, abi_ref)):
        ab = ab_ref[...]
        for q in range(gb // pairs):
            b_sc[:, part * half + q * LANES:part * half + (q + 1) * LANES] = jnp.where(
                row_g == pairs * q + lane_half, ab, jnp.zeros_like(ab))

    col_g2 = group_of(lax.broadcasted_iota(jnp.int32, (LANES, t * LANES), 1))
    row_half = lax.broadcasted_iota(jnp.int32, (LANES, t * LANES), 0) >> (p.bit_length() - 1)
    for part, cm_ref in enumerate((cmr_ref, cmi_ref)):
        cm = cm_ref[...]
        for q in range(gb // pairs):
            c_sc[part * half + q * LANES:part * half + (q + 1) * LANES, :] = jnp.where(
                col_g2 == pairs * q + row_half, cm, jnp.zeros_like(cm))


def _ssm_kernel(s_ref, kr_ref, abr_ref, abi_ref, cmr_ref, cmi_ref, sr_ref, si_ref, y_ref,
                m_sc, b_sc, c_sc):
    t, n_chunks, _ = s_ref.shape
    half = GROUPS_PER_BLOCK * SSM_STATE

    @pl.when(pl.program_id(1) == 0)
    def _():
        _ssm_expand(kr_ref, abr_ref, abi_ref, cmr_ref, cmi_ref, m_sc, b_sc, c_sc)

    u = jnp.concatenate([s_ref[tau].astype(BF16) for tau in range(t)], axis=1)
    y = _dot(u, m_sc[...])
    x = _dot(u, b_sc[...])
    hr, hi = x[:, :half], x[:, half:]

    row = lax.broadcasted_iota(jnp.int32, (n_chunks, half), 0)

    def shifted(h, k):
        return jnp.where(row >= k, pltpu.roll(h, k, 0), 0.0)

    step = 0
    while (1 << step) < n_chunks:
        k = 1 << step
        pr = sr_ref[step:step + 1, :]
        pi = si_ref[step:step + 1, :]
        zr, zi = shifted(hr, k), shifted(hi, k)
        hr, hi = hr + pr * zr - pi * zi, hi + pr * zi + pi * zr
        step += 1

    h_prev = jnp.concatenate([shifted(hr, 1), shifted(hi, 1)], axis=1).astype(BF16)
    y = jax.nn.gelu(y + _dot(h_prev, c_sc[...]))
    for tau in range(t):
        y_ref[tau] = y[:, tau * LANES:(tau + 1) * LANES]


def _ssm(z, s_col, weights):
    b, n_res, n_chunks, _ = z.shape
    nj = weights[0].shape[0]
    assert n_res == SSM_CHUNK
    assert SSM_GROUP & (SSM_GROUP - 1) == 0 and SSM_STATE & (SSM_STATE - 1) == 0

    def wspec(a):
        nd = a.ndim - 1
        return pl.BlockSpec((None,) + a.shape[1:], lambda j, bi: (j,) + (0,) * nd)

    s = n_res * n_chunks
    width = SSM_CHUNK * LANES
    state = 2 * GROUPS_PER_BLOCK * SSM_STATE
    scratch = (width * width + 2 * width * state) * 2
    wbytes = 2 * sum(math.prod(a.shape[1:]) * a.dtype.itemsize for a in weights)
    vmem = (4 * s * LANES * 4 + wbytes + scratch + n_chunks * (width * 10 + state * 12)) * 1.2
    return pl.pallas_call(
        _ssm_kernel,
        grid=(nj, b),
        in_specs=[pl.BlockSpec((None, n_res, n_chunks, LANES),
                               lambda j, bi: (bi, 0, 0, s_col + j))]
        + [wspec(a) for a in weights],
        out_specs=pl.BlockSpec((None, n_res, n_chunks, LANES), lambda j, bi: (bi, 0, 0, j)),
        out_shape=jax.ShapeDtypeStruct((b, n_res, n_chunks, nj * LANES), F32),
        scratch_shapes=[pltpu.VMEM((width, width), BF16), pltpu.VMEM((width, state), BF16),
                        pltpu.VMEM((state, width), BF16)],
        compiler_params=_params(("parallel", "arbitrary"), vmem),
        name="ssm",
    )(z, *weights)


def _mix_out_kernel(ya_ref, y_ref, h_ref, ga_ref, gb_ref, wglu_ref, bglu_ref,
                    wa_ref, wb_ref, o_ref):
    n_res, tl, _ = ya_ref.shape
    y = y_ref[...].reshape(n_res * tl, -1)
    gate = _dot(y.astype(BF16), wglu_ref[...]) + bglu_ref[...]
    yb = y * jax.nn.sigmoid(gate)
    na = _rms(ya_ref[...].reshape(n_res * tl, -1), ga_ref[...]).astype(BF16)
    nb = _rms(yb, gb_ref[...]).astype(BF16)
    perm = _residue_perm(n_res, tl, True)
    na = _dot(perm, na).astype(BF16)
    nb = _dot(perm, nb).astype(BF16)
    o_ref[...] = h_ref[...] + _dot(na, wa_ref[...]) + _dot(nb, wb_ref[...])


def _mix_out(ya, y, h, ga, gb, wglu, bglu, wa, wb, *, tm=512):
    m, d = h.shape
    batch, n_res, length, wa_w = ya.shape
    wb_w = y.shape[-1]
    tm = min(tm, n_res * length)
    tl = tm // n_res
    tiles = length // tl
    assert tl % 8 == 0 and length % tl == 0
    vmem = (2 * tm * (wa_w + wb_w + 2 * d) * 4 + (wglu.size + wa.size + wb.size) * 2
            + 8 * tm * d * 4) * 1.2
    row = pl.BlockSpec((tm, d), lambda i: (i, 0))
    res_major = lambda w: pl.BlockSpec((None, n_res, tl, w),
                                       lambda i: (i // tiles, 0, i % tiles, 0))
    return pl.pallas_call(
        _mix_out_kernel,
        grid=(m // tm,),
        in_specs=[res_major(wa_w), res_major(wb_w), row,
                  _const_spec((1, wa_w)), _const_spec((1, wb_w)),
                  _const_spec(wglu.shape), _const_spec((1, wb_w)),
                  _const_spec(wa.shape), _const_spec(wb.shape)],
        out_specs=row,
        out_shape=jax.ShapeDtypeStruct((m, d), F32),
        compiler_params=_params(("parallel",), vmem),
        name="mix_out",
    )(ya, y, h, ga.reshape(1, -1), gb.reshape(1, -1), wglu, bglu.reshape(1, -1), wa, wb)


def _ple_kernel(h_ref, p_ref, gn_ref, wg_ref, wp_ref, fn_ref, o_ref, *, final):
    h = h_ref[...]
    gate = jax.nn.sigmoid(_dot(_rms(h, gn_ref[...]).astype(BF16), wg_ref[...]))
    proj = _dot(p_ref[...].astype(BF16), wp_ref[...])
    out = h + gate * proj
    o_ref[...] = _rms(out, fn_ref[...]) if final else out


def _ple(h, p, gn, wg, wp, fn, *, final, tm=512):
    m, d = h.shape
    pd = p.shape[1]
    tm = min(tm, m)
    vmem = (2 * tm * (2 * d + pd) * 4 + (wg.size + wp.size) * 2 + 8 * tm * d * 4) * 1.2
    row = lambda w: pl.BlockSpec((tm, w), lambda i: (i, 0))
    return pl.pallas_call(
        functools.partial(_ple_kernel, final=final),
        grid=(m // tm,),
        in_specs=[row(d), row(pd), _const_spec((1, d)), _const_spec(wg.shape),
                  _const_spec(wp.shape), _const_spec((1, d))],
        out_specs=row(d),
        out_shape=jax.ShapeDtypeStruct((m, d), F32),
        compiler_params=_params(("parallel",), vmem),
        name="ple",
    )(h, p, gn.reshape(1, d), wg, wp, fn.reshape(1, d))


def kernel(x, p, ffn1_norm, ffn1_w_gate, ffn1_w_up, ffn1_w_down, mix_norm, w_in, attn_out_norm, ssm_lambda_re, ssm_lambda_im, ssm_log_dt, ssm_b_re, ssm_b_im, ssm_c_re, ssm_c_im, ssm_d, ssm_w_glu, ssm_b_glu, ssm_out_norm, w_out, ffn2_norm, ffn2_w_gate, ffn2_w_up, ffn2_w_down, ple_norm, ple_w_gate, ple_w_proj, final_norm):
    b, s, d = x.shape
    depth = p.shape[0]
    attn_w = attn_out_norm.shape[1]
    ssm_w = ssm_out_norm.shape[1]
    n_pairs = attn_w // LANES
    h = x.reshape(b * s, d)
    for i in range(depth):
        h = _ffn(h, ffn1_norm[i], _ff_tiles(ffn1_w_gate[i], 1), _ff_tiles(ffn1_w_up[i], 1),
                 _ff_tiles(ffn1_w_down[i], 0))
        z = _in_proj(h, mix_norm[i], w_in[i].astype(BF16), b)
        ya = _attention(z, n_pairs, q_col=0, k_col=n_pairs, v_col=2 * n_pairs)
        weights = _ssm_weights(ssm_lambda_re[i], ssm_lambda_im[i], ssm_log_dt[i],
                               ssm_b_re[i], ssm_b_im[i], ssm_c_re[i], ssm_c_im[i],
                               ssm_d[i], s // SSM_CHUNK)
        y = _ssm(z, 3 * n_pairs, weights)
        wo = w_out[i].astype(BF16)
        h = _mix_out(ya, y, h, attn_out_norm[i], ssm_out_norm[i], ssm_w_glu[i].astype(BF16),
                     ssm_b_glu[i], wo[:attn_w], wo[attn_w:])
        h = _ffn(h, ffn2_norm[i], _ff_tiles(ffn2_w_gate[i], 1), _ff_tiles(ffn2_w_up[i], 1),
                 _ff_tiles(ffn2_w_down[i], 0))
        h = _ple(h, p[i].reshape(b * s, -1), ple_norm[i], ple_w_gate[i].astype(BF16),
                 ple_w_proj[i].astype(BF16), final_norm, final=i == depth - 1)
    return h.reshape(b, s, d)
```

```python
import functools
import math

import jax
import jax.numpy as jnp
from jax import lax
from jax.experimental import pallas as pl
from jax.experimental.pallas import tpu as pltpu

NORM_EPS = 1e-6
MASK_VALUE = -1e30
HEAD_DIM = 64
ATTN_BLOCK = 128
ATTN_SPAN = 128
DILATIONS = (1, 4, 16)
ATTN_UNROLL = 16
SSM_GROUP = 16
SSM_STATE = 64
SSM_CHUNK = 16
SCAN_GROUP = 8
TOKEN_RES = 16
LANES = 128
GROUPS_PER_BLOCK = LANES // SSM_GROUP
FF_TILE = 256
FF_STAGE = 512
VMEM_CAP = 56 * 1024 * 1024

BF16 = jnp.bfloat16
F32 = jnp.float32


def _params(sem, vmem_bytes):
    return pltpu.CompilerParams(
        dimension_semantics=sem, vmem_limit_bytes=min(int(vmem_bytes), VMEM_CAP))


def _rms(x, g):
    ms = jnp.mean(x * x, axis=-1, keepdims=True)
    return x * lax.rsqrt(ms + NORM_EPS) * g


def _dot(a, b):
    return jnp.dot(a, b, preferred_element_type=F32)


def _const_spec(shape):
    return pl.BlockSpec(shape, lambda *_: (0,) * len(shape),
                        pipeline_mode=pl.Buffered(1))


def _ffn_kernel(x_ref, g_ref, wg_ref, wu_ref, wd_ref, o_ref, xn_ref):
    f = pl.program_id(1)

    @pl.when(f == 0)
    def _():
        xn_ref[...] = _rms(x_ref[...], g_ref[...]).astype(BF16)
        o_ref[...] = jnp.zeros_like(o_ref)

    xn = xn_ref[...]
    gate = _dot(xn, wg_ref[...])
    up = _dot(xn, wu_ref[...])
    act = (gate * jax.nn.sigmoid(gate) * up).astype(BF16)
    o_ref[...] += _dot(act, wd_ref[...])

    @pl.when(f == pl.num_programs(1) - 1)
    def _():
        o_ref[...] = x_ref[...] + 0.5 * o_ref[...]


def _ffn(h, g, wg, wu, wd, *, tm=1024):
    m, d = h.shape
    nf, _, tf = wg.shape
    tm = min(tm, m)
    vmem = (4 * tm * d * 4 + tm * d * 2 + 6 * d * tf * 2 + 6 * tm * tf * 4) * 1.25
    row = pl.BlockSpec((tm, d), lambda i, f: (i, 0))
    return pl.pallas_call(
        _ffn_kernel,
        grid=(m // tm, nf),
        in_specs=[
            row,
            pl.BlockSpec((1, d), lambda i, f: (0, 0)),
            pl.BlockSpec((None, d, tf), lambda i, f: (f, 0, 0)),
            pl.BlockSpec((None, d, tf), lambda i, f: (f, 0, 0)),
            pl.BlockSpec((None, tf, d), lambda i, f: (f, 0, 0)),
        ],
        out_specs=row,
        out_shape=jax.ShapeDtypeStruct((m, d), F32),
        scratch_shapes=[pltpu.VMEM((tm, d), BF16)],
        compiler_params=_params(("parallel", "arbitrary"), vmem),
        name="ffn",
    )(h, g.reshape(1, d), wg, wu, wd)


def _ff_tile_kernel(w_ref, o_ref, *, axis, size):
    w = w_ref[...]
    n_tiles = o_ref.shape[0]
    tile = o_ref.shape[1 + axis]
    pos = lax.broadcasted_iota(jnp.int32, w.shape, axis) + pl.program_id(0) * (n_tiles * tile)
    w = jnp.where(pos < size, w, 0.0).astype(o_ref.dtype)
    for t in range(n_tiles):
        o_ref[t] = w[:, t * tile:(t + 1) * tile] if axis == 1 else w[t * tile:(t + 1) * tile]


def _ff_tiles(w, layer, axis, tf=FF_TILE):
    size = w.shape[1 + axis]
    per_step = max(1, FF_STAGE // tf)
    nf = pl.cdiv(size, tf * per_step) * per_step
    tile = list(w.shape[1:])
    tile[axis] = tf
    stage = list(tile)
    stage[axis] = tf * per_step
    index = (lambda f: (layer, 0, f)) if axis == 1 else (lambda f: (layer, f, 0))
    return pl.pallas_call(
        functools.partial(_ff_tile_kernel, axis=axis, size=size),
        grid=(nf // per_step,),
        in_specs=[pl.BlockSpec((None,) + tuple(stage), index)],
        out_specs=pl.BlockSpec((per_step,) + tuple(tile), lambda f: (f, 0, 0)),
        out_shape=jax.ShapeDtypeStruct((nf,) + tuple(tile), BF16),
        compiler_params=_params(("parallel",), 6 * math.prod(stage) * 4),
        name="ff_tiles",
    )(w)


def _residue_perm(n_res, tl, transpose):
    tm = n_res * tl
    assert tl & (tl - 1) == 0
    a = lax.broadcasted_iota(jnp.int32, (tm, tm), 1 if transpose else 0)
    b = lax.broadcasted_iota(jnp.int32, (tm, tm), 0 if transpose else 1)
    hit = b == n_res * (a & (tl - 1)) + (a >> (tl.bit_length() - 1))
    return jnp.where(hit, 1.0, 0.0).astype(BF16)


def _in_proj_kernel(x_ref, g_ref, w_ref, o_ref):
    n_res, tl, _ = o_ref.shape
    xn = _rms(x_ref[...], g_ref[...]).astype(BF16)
    xn = _dot(_residue_perm(n_res, tl, False), xn).astype(BF16)
    res = _dot(xn, w_ref[...])
    for r in range(n_res):
        o_ref[r] = res[r * tl:(r + 1) * tl]


def _in_proj(h, g, w, batch, *, tm=256):
    m, d = h.shape
    n = w.shape[1]
    seq = m // batch
    tm = min(tm, seq)
    tl = tm // TOKEN_RES
    tiles = seq // tm
    assert tl % 8 == 0 and seq % tm == 0
    vmem = (2 * tm * d * 4 + tm * d * 6 + d * n * 2 + 4 * tm * n * 4 + 4 * tm * tm) * 1.2
    return pl.pallas_call(
        _in_proj_kernel,
        grid=(m // tm,),
        in_specs=[
            pl.BlockSpec((tm, d), lambda i: (i, 0)),
            _const_spec((1, d)),
            _const_spec((d, n)),
        ],
        out_specs=pl.BlockSpec((None, TOKEN_RES, tl, n),
                               lambda i: (i // tiles, 0, i % tiles, 0)),
        out_shape=jax.ShapeDtypeStruct((batch, TOKEN_RES, seq // TOKEN_RES, n), F32),
        compiler_params=_params(("parallel",), vmem),
        name="in_proj",
    )(h, g.reshape(1, d), w)


def _attn_kernel(q_ref, k_ref, v_ref, o_ref, m_ref, l_ref, bias_ref, p_sc, mx_sc):
    n_res, length, _ = q_ref.shape
    blk = ATTN_BLOCK
    n_units = n_res * length // blk
    head0 = lax.broadcasted_iota(jnp.int32, (blk, LANES), 1) < HEAD_DIM
    kv_head0 = lax.broadcasted_iota(jnp.int32, (2 * blk, LANES), 1) < HEAD_DIM
    scale = HEAD_DIM ** -0.5 * math.log2(math.e)

    rho = lax.broadcasted_iota(jnp.int32, (blk, 2 * blk), 0)
    kap = lax.broadcasted_iota(jnp.int32, (blk, 2 * blk), 1)
    for pattern, d in enumerate(DILATIONS):
        n_sub = n_res // d
        c_rows = blk // n_sub
        sh = c_rows.bit_length() - 1
        iq, lq = rho >> sh, rho & (c_rows - 1)
        ik, lk = kap >> (sh + 1), kap & (2 * c_rows - 1)
        for shift in range(2):
            dist = n_sub * (lq - lk + shift * c_rows) + (iq - ik)
            bias_ref[2 * pattern + shift] = jnp.where(
                (dist >= 0) & (dist <= ATTN_SPAN), 0.0, MASK_VALUE)

    def probs(q, k, bias, sel):
        qh = jnp.where(sel, q, 0.0).astype(BF16)
        s = lax.dot_general(qh, k, (((1,), (1,)), ((), ())),
                            preferred_element_type=F32) + bias
        m = jnp.max(s, axis=-1, keepdims=True)
        return jnp.exp2(s - m).astype(BF16), m

    n_groups = n_units // ATTN_UNROLL
    for pattern, d in enumerate(DILATIONS):
        first = pattern == 0
        last = pattern == len(DILATIONS) - 1
        n_sub = n_res // d
        c_rows = blk // n_sub
        blocks_per_residue = n_units // d

        def gather(ref, res, start, size, d=d, n_sub=n_sub):
            return jnp.concatenate(
                [ref[d * i + res, pl.ds(start, size), :] for i in range(n_sub)], axis=0)

        def scatter(ref, res, start, val, d=d, n_sub=n_sub, c_rows=c_rows):
            for i in range(n_sub):
                ref[d * i + res, pl.ds(start, c_rows), :] = val[i * c_rows:(i + 1) * c_rows]

        def place(g, uu, c_rows=c_rows, blocks_per_residue=blocks_per_residue):
            u = g * ATTN_UNROLL + uu
            res = u // blocks_per_residue
            bq = u % blocks_per_residue
            kb = jnp.maximum(bq - 1, 0)
            return (res, pl.multiple_of(c_rows * bq, 8), pl.multiple_of(c_rows * kb, 8), bq - kb)

        def score_stage(g, slot, pattern=pattern, c_rows=c_rows, gather=gather, place=place):
            loaded = []
            for uu in range(ATTN_UNROLL):
                res, q0, k0, shift = place(g, uu)
                loaded.append((gather(q_ref, res, q0, c_rows) * scale,
                               gather(k_ref, res, k0, 2 * c_rows).astype(BF16),
                               bias_ref[2 * pattern + shift]))
            for uu, (q, k, bias) in enumerate(loaded):
                p0, m0 = probs(q, k, bias, head0)
                p1, m1 = probs(q, k, bias, jnp.logical_not(head0))
                p_sc[slot, uu, 0] = p0
                p_sc[slot, uu, 1] = p1
                mx_sc[slot, uu] = jnp.where(head0, m0, m1)

        def value_stage(g, slot, first=first, last=last, c_rows=c_rows,
                        gather=gather, scatter=scatter, place=place):
            loaded = []
            for uu in range(ATTN_UNROLL):
                res, q0, k0, _ = place(g, uu)
                old = None if first else tuple(
                    gather(ref, res, q0, c_rows) for ref in (o_ref, m_ref, l_ref))
                loaded.append((res, q0, gather(v_ref, res, k0, 2 * c_rows).astype(BF16),
                               p_sc[slot, uu, 0], p_sc[slot, uu, 1], mx_sc[slot, uu], old))
            results = []
            for res, q0, v, p0, p1, m_new, old in loaded:
                ones = jnp.ones_like(v)
                o0 = _dot(p0, jnp.where(kv_head0, v, ones))
                o1 = _dot(p1, jnp.where(kv_head0, ones, v))
                o_new = jnp.where(head0, o0, o1)
                l_new = pltpu.roll(jnp.where(head0, o1, o0), HEAD_DIM, 1)
                if not first:
                    o_old, m_old, l_old = old
                    m_tot = jnp.maximum(m_old, m_new)
                    w_old = jnp.exp2(m_old - m_tot)
                    w_new = jnp.exp2(m_new - m_tot)
                    o_new = w_old * o_old + w_new * o_new
                    l_new = w_old * l_old + w_new * l_new
                    m_new = m_tot
                results.append((res, q0, o_new, m_new, l_new))
            for res, q0, o_new, m_new, l_new in results:
                if last:
                    scatter(o_ref, res, q0, o_new / l_new)
                else:
                    scatter(o_ref, res, q0, o_new)
                    scatter(m_ref, res, q0, m_new)
                    scatter(l_ref, res, q0, l_new)

        def step(g, carry, score_stage=score_stage, value_stage=value_stage):
            slot = g & 1
            value_stage(g - 1, 1 - slot)
            score_stage(g, slot)
            return carry

        score_stage(0, 0)
        lax.fori_loop(1, n_groups, step, 0)
        value_stage(n_groups - 1, (n_groups - 1) & 1)


def _attention(z, n_pairs, *, q_col, k_col, v_col):
    b, n_res, length, _ = z.shape
    assert n_res == TOKEN_RES == DILATIONS[-1] and all(n_res % d == 0 for d in DILATIONS)
    assert length % ATTN_BLOCK == 0 and length >= 2 * ATTN_BLOCK and ATTN_BLOCK // n_res >= 8
    assert (n_res * length // ATTN_BLOCK) % ATTN_UNROLL == 0

    def spec(col):
        return pl.BlockSpec((None, n_res, length, LANES), lambda bi, h: (bi, 0, 0, col + h))

    slab = n_res * length * LANES * 4
    vmem = 10 * slab * 1.15 + (8 << 20)
    return pl.pallas_call(
        _attn_kernel,
        grid=(b, n_pairs),
        in_specs=[spec(q_col), spec(k_col), spec(v_col)],
        out_specs=spec(0),
        out_shape=jax.ShapeDtypeStruct((b, n_res, length, n_pairs * LANES), F32),
        scratch_shapes=[pltpu.VMEM((n_res, length, LANES), F32),
                        pltpu.VMEM((n_res, length, LANES), F32),
                        pltpu.VMEM((2 * len(DILATIONS), ATTN_BLOCK, 2 * ATTN_BLOCK), F32),
                        pltpu.VMEM((2, ATTN_UNROLL, 2, ATTN_BLOCK, 2 * ATTN_BLOCK), BF16),
                        pltpu.VMEM((2, ATTN_UNROLL, ATTN_BLOCK, LANES), F32)],
        compiler_params=_params(("parallel", "parallel"), vmem),
        name="attention",
    )(z, z, z)


def _ssm_weights(lam_re, lam_im, log_dt, b_re, b_im, c_re, c_im, d_skip, n_chunks):
    hp = lax.Precision.HIGHEST
    g, p = lam_re.shape
    c = b_re.shape[-1]
    t = SSM_CHUNK
    gb = GROUPS_PER_BLOCK
    nj = g // gb
    lr, li = lam_re.astype(F32), lam_im.astype(F32)
    dt = jnp.exp(log_dt.astype(F32))[:, None]
    mag = jnp.exp(lr * dt)
    ar = mag * jnp.cos(li * dt)
    ai = mag * jnp.sin(li * dt)
    nr, ni = ar - 1.0, ai
    den = lr * lr + li * li
    cr = (nr * lr + ni * li) / den
    ci = (ni * lr - nr * li) / den
    br, bi = b_re.astype(F32), b_im.astype(F32)
    bbr = cr[..., None] * br - ci[..., None] * bi
    bbi = cr[..., None] * bi + ci[..., None] * br

    def power(k):
        kk = k.astype(F32)[:, None, None]
        pm = jnp.exp(kk * (lr * dt))
        ang = kk * (li * dt)
        return pm * jnp.cos(ang), pm * jnp.sin(ang)

    pr, pi = power(jnp.arange(t + 1))
    cre, cim = c_re.astype(F32), c_im.astype(F32)
    car = cre[None] * pr[:, :, None, :] - cim[None] * pi[:, :, None, :]
    cai = cre[None] * pi[:, :, None, :] + cim[None] * pr[:, :, None, :]
    kern = jnp.einsum('kgdq,gqc->kgdc', jnp.concatenate([car[:t], -cai[:t]], axis=-1),
                      jnp.concatenate([bbr, bbi], axis=1), precision=hp)
    kern = kern.at[0].add(d_skip.astype(F32).reshape(g, c)[:, :, None] * jnp.eye(c, dtype=F32))

    kr = kern.reshape(t, nj, gb, c, c).transpose(1, 0, 4, 2, 3).reshape(nj, t, c, gb * c)

    rev_r, rev_i = pr[:t][::-1][:, :, None, :], pi[:t][::-1][:, :, None, :]
    bbr_t, bbi_t = bbr.transpose(0, 2, 1)[None], bbi.transpose(0, 2, 1)[None]
    abr = rev_r * bbr_t - rev_i * bbi_t
    abi = rev_r * bbi_t + rev_i * bbr_t

    def block_rows(x):
        return x.reshape(t, nj, gb, c, p).transpose(1, 0, 2, 3, 4).reshape(nj, t * gb * c, p)

    def rows_form(x):
        x = block_rows(x)
        return jnp.concatenate([x, x], axis=-1).astype(BF16)

    def cols_form(x):
        x = block_rows(x).transpose(0, 2, 1)
        return jnp.concatenate([x, x], axis=1).astype(BF16)

    n_steps = max(1, (n_chunks - 1).bit_length())
    sr, si = power(t * (2 ** jnp.arange(n_steps)))
    rows = -(-n_steps // 8) * 8
    sr = jnp.pad(sr.reshape(n_steps, nj, gb * p).transpose(1, 0, 2), ((0, 0), (0, rows - n_steps), (0, 0)))
    si = jnp.pad(si.reshape(n_steps, nj, gb * p).transpose(1, 0, 2), ((0, 0), (0, rows - n_steps), (0, 0)))
    gr, gi = power(t * (jnp.arange(SCAN_GROUP) + 1))
    gr = gr.reshape(SCAN_GROUP, nj, gb * p).transpose(1, 0, 2)
    gi = gi.reshape(SCAN_GROUP, nj, gb * p).transpose(1, 0, 2)
    return (kr.astype(BF16), rows_form(abr), rows_form(abi),
            cols_form(car[1:]), cols_form(-cai[1:]), sr, si, gr, gi)


def _ssm_expand(kr_ref, abr_ref, abi_ref, cmr_ref, cmi_ref, m_sc, b_sc, c_sc):
    t, c, gb, p = SSM_CHUNK, SSM_GROUP, GROUPS_PER_BLOCK, SSM_STATE
    shift = c.bit_length() - 1
    half = gb * p
    pairs = LANES // p

    def group_of(idx):
        return (idx >> shift) & (gb - 1)

    m_sc[...] = jnp.zeros_like(m_sc)
    col_g = group_of(lax.broadcasted_iota(jnp.int32, (c, LANES), 1))
    for k in range(t):
        kk = kr_ref[k]
        dk = jnp.concatenate([jnp.where(col_g == g, kk, jnp.zeros_like(kk)) for g in range(gb)], axis=0)
        for tau in range(t - k):
            m_sc[tau * LANES:(tau + 1) * LANES, (tau + k) * LANES:(tau + k + 1) * LANES] = dk

    row_g = group_of(lax.broadcasted_iota(jnp.int32, (t * LANES, LANES), 0))
    lane_half = lax.broadcasted_iota(jnp.int32, (t * LANES, LANES), 1) >> (p.bit_length() - 1)
    for part, ab_ref in enumerate((abr_ref, abi_ref)):
        ab = ab_ref[...]
        for q in range(gb // pairs):
            b_sc[:, part * half + q * LANES:part * half + (q + 1) * LANES] = jnp.where(
                row_g == pairs * q + lane_half, ab, jnp.zeros_like(ab))

    col_g2 = group_of(lax.broadcasted_iota(jnp.int32, (LANES, t * LANES), 1))
    row_half = lax.broadcasted_iota(jnp.int32, (LANES, t * LANES), 0) >> (p.bit_length() - 1)
    for part, cm_ref in enumerate((cmr_ref, cmi_ref)):
        cm = cm_ref[...]
        for q in range(gb // pairs):
            c_sc[part * half + q * LANES:part * half + (q + 1) * LANES, :] = jnp.where(
                col_g2 == pairs * q + row_half, cm, jnp.zeros_like(cm))


def _ssm_kernel(s_ref, kr_ref, abr_ref, abi_ref, cmr_ref, cmi_ref, sr_ref, si_ref,
                gr_ref, gi_ref, y_ref, m_sc, b_sc, c_sc):
    t, n_chunks, _ = s_ref.shape
    half = GROUPS_PER_BLOCK * SSM_STATE

    @pl.when(pl.program_id(1) == 0)
    def _():
        _ssm_expand(kr_ref, abr_ref, abi_ref, cmr_ref, cmi_ref, m_sc, b_sc, c_sc)

    u = jnp.concatenate([s_ref[tau].astype(BF16) for tau in range(t)], axis=1)
    y = _dot(u, m_sc[...])
    x = _dot(u, b_sc[...])
    hr, hi = x[:, :half], x[:, half:]

    grp = SCAN_GROUP
    n_groups = n_chunks // grp
    row = lax.broadcasted_iota(jnp.int32, (n_chunks, half), 0)
    grow = lax.broadcasted_iota(jnp.int32, (n_groups, half), 0)

    def shifted(h, k, pos):
        return jnp.where(pos >= k, pltpu.roll(h, k, 0), 0.0)

    def scan_step(hr, hi, step, k, pos):
        pr = sr_ref[step:step + 1, :]
        pi = si_ref[step:step + 1, :]
        zr, zi = shifted(hr, k, pos), shifted(hi, k, pos)
        return hr + pr * zr - pi * zi, hi + pr * zi + pi * zr

    in_group = row & (grp - 1)
    step = 0
    while (1 << step) < grp:
        hr, hi = scan_step(hr, hi, step, 1 << step, in_group)
        step += 1
    er = hr.reshape(n_groups, grp, half)[:, grp - 1, :]
    ei = hi.reshape(n_groups, grp, half)[:, grp - 1, :]
    while (grp << (step - grp.bit_length() + 1)) < n_chunks:
        er, ei = scan_step(er, ei, step, 1 << (step - grp.bit_length() + 1), grow)
        step += 1

    def spread(e):
        e = shifted(e, 1, grow)
        return jnp.broadcast_to(e[:, None, :], (n_groups, grp, half)).reshape(n_chunks, half)

    cr, ci = spread(er), spread(ei)
    gr = jnp.tile(gr_ref[...], (n_groups, 1))
    gi = jnp.tile(gi_ref[...], (n_groups, 1))
    hr, hi = hr + gr * cr - gi * ci, hi + gr * ci + gi * cr

    h_prev = jnp.concatenate([shifted(hr, 1, row), shifted(hi, 1, row)], axis=1).astype(BF16)
    y = jax.nn.gelu(y + _dot(h_prev, c_sc[...]))
    for tau in range(t):
        y_ref[tau] = y[:, tau * LANES:(tau + 1) * LANES]


def _ssm(z, s_col, weights):
    b, n_res, n_chunks, _ = z.shape
    nj = weights[0].shape[0]
    assert n_res == SSM_CHUNK
    assert SSM_GROUP & (SSM_GROUP - 1) == 0 and SSM_STATE & (SSM_STATE - 1) == 0

    def wspec(a):
        nd = a.ndim - 1
        return pl.BlockSpec((None,) + a.shape[1:], lambda j, bi: (j,) + (0,) * nd)

    s = n_res * n_chunks
    width = SSM_CHUNK * LANES
    state = 2 * GROUPS_PER_BLOCK * SSM_STATE
    scratch = (width * width + 2 * width * state) * 2
    wbytes = 2 * sum(math.prod(a.shape[1:]) * a.dtype.itemsize for a in weights)
    vmem = (4 * s * LANES * 4 + wbytes + scratch + n_chunks * (width * 10 + state * 12)) * 1.2
    return pl.pallas_call(
        _ssm_kernel,
        grid=(nj, b),
        in_specs=[pl.BlockSpec((None, n_res, n_chunks, LANES),
                               lambda j, bi: (bi, 0, 0, s_col + j))]
        + [wspec(a) for a in weights],
        out_specs=pl.BlockSpec((None, n_res, n_chunks, LANES), lambda j, bi: (bi, 0, 0, j)),
        out_shape=jax.ShapeDtypeStruct((b, n_res, n_chunks, nj * LANES), F32),
        scratch_shapes=[pltpu.VMEM((width, width), BF16), pltpu.VMEM((width, state), BF16),
                        pltpu.VMEM((state, width), BF16)],
        compiler_params=_params(("parallel", "arbitrary"), vmem),
        name="ssm",
    )(z, *weights)


def _mix_out_kernel(ya_ref, y_ref, h_ref, ga_ref, gb_ref, wglu_ref, bglu_ref,
                    wa_ref, wb_ref, o_ref):
    n_res, tl, _ = ya_ref.shape
    y = y_ref[...].reshape(n_res * tl, -1)
    gate = _dot(y.astype(BF16), wglu_ref[...]) + bglu_ref[...]
    yb = y * jax.nn.sigmoid(gate)
    na = _rms(ya_ref[...].reshape(n_res * tl, -1), ga_ref[...]).astype(BF16)
    nb = _rms(yb, gb_ref[...]).astype(BF16)
    perm = _residue_perm(n_res, tl, True)
    na = _dot(perm, na).astype(BF16)
    nb = _dot(perm, nb).astype(BF16)
    o_ref[...] = h_ref[...] + _dot(na, wa_ref[...]) + _dot(nb, wb_ref[...])


def _mix_out(ya, y, h, ga, gb, wglu, bglu, wa, wb, *, tm=512):
    m, d = h.shape
    batch, n_res, length, wa_w = ya.shape
    wb_w = y.shape[-1]
    tm = min(tm, n_res * length)
    tl = tm // n_res
    tiles = length // tl
    assert tl % 8 == 0 and length % tl == 0
    vmem = (2 * tm * (wa_w + wb_w + 2 * d) * 4 + (wglu.size + wa.size + wb.size) * 2
            + 8 * tm * d * 4) * 1.2
    row = pl.BlockSpec((tm, d), lambda i: (i, 0))
    res_major = lambda w: pl.BlockSpec((None, n_res, tl, w),
                                       lambda i: (i // tiles, 0, i % tiles, 0))
    return pl.pallas_call(
        _mix_out_kernel,
        grid=(m // tm,),
        in_specs=[res_major(wa_w), res_major(wb_w), row,
                  _const_spec((1, wa_w)), _const_spec((1, wb_w)),
                  _const_spec(wglu.shape), _const_spec((1, wb_w)),
                  _const_spec(wa.shape), _const_spec(wb.shape)],
        out_specs=row,
        out_shape=jax.ShapeDtypeStruct((m, d), F32),
        compiler_params=_params(("parallel",), vmem),
        name="mix_out",
    )(ya, y, h, ga.reshape(1, -1), gb.reshape(1, -1), wglu, bglu.reshape(1, -1), wa, wb)


def _ple_kernel(h_ref, p_ref, gn_ref, wg_ref, wp_ref, fn_ref, o_ref, *, final):
    h = h_ref[...]
    gate = jax.nn.sigmoid(_dot(_rms(h, gn_ref[...]).astype(BF16), wg_ref[...]))
    proj = _dot(p_ref[...].astype(BF16), wp_ref[...])
    out = h + gate * proj
    o_ref[...] = _rms(out, fn_ref[...]) if final else out


def _ple(h, p, layer, gn, wg, wp, fn, *, final, tm=512):
    m, d = h.shape
    pd = p.shape[2]
    tm = min(tm, m)
    vmem = (2 * tm * (2 * d + pd) * 4 + (wg.size + wp.size) * 2 + 8 * tm * d * 4) * 1.2
    row = lambda w: pl.BlockSpec((tm, w), lambda i: (i, 0))
    return pl.pallas_call(
        functools.partial(_ple_kernel, final=final),
        grid=(m // tm,),
        in_specs=[row(d), pl.BlockSpec((None, tm, pd), lambda i: (layer, i, 0)),
                  _const_spec((1, d)), _const_spec(wg.shape),
                  _const_spec(wp.shape), _const_spec((1, d))],
        out_specs=row(d),
        out_shape=jax.ShapeDtypeStruct((m, d), F32),
        compiler_params=_params(("parallel",), vmem),
        name="ple",
    )(h, p, gn.reshape(1, d), wg, wp, fn.reshape(1, d))


def kernel(x, p, ffn1_norm, ffn1_w_gate, ffn1_w_up, ffn1_w_down, mix_norm, w_in, attn_out_norm, ssm_lambda_re, ssm_lambda_im, ssm_log_dt, ssm_b_re, ssm_b_im, ssm_c_re, ssm_c_im, ssm_d, ssm_w_glu, ssm_b_glu, ssm_out_norm, w_out, ffn2_norm, ffn2_w_gate, ffn2_w_up, ffn2_w_down, ple_norm, ple_w_gate, ple_w_proj, final_norm):
    b, s, d = x.shape
    depth = p.shape[0]
    attn_w = attn_out_norm.shape[1]
    ssm_w = ssm_out_norm.shape[1]
    n_pairs = attn_w // LANES
    h = x.reshape(b * s, d)
    for i in range(depth):
        h = _ffn(h, ffn1_norm[i], _ff_tiles(ffn1_w_gate, i, 1), _ff_tiles(ffn1_w_up, i, 1),
                 _ff_tiles(ffn1_w_down, i, 0))
        z = _in_proj(h, mix_norm[i], w_in[i].astype(BF16), b)
        ya = _attention(z, n_pairs, q_col=0, k_col=n_pairs, v_col=2 * n_pairs)
        weights = _ssm_weights(ssm_lambda_re[i], ssm_lambda_im[i], ssm_log_dt[i],
                               ssm_b_re[i], ssm_b_im[i], ssm_c_re[i], ssm_c_im[i],
                               ssm_d[i], s // SSM_CHUNK)
        y = _ssm(z, 3 * n_pairs, weights)
        wo = w_out[i].astype(BF16)
        h = _mix_out(ya, y, h, attn_out_norm[i], ssm_out_norm[i], ssm_w_glu[i].astype(BF16),
                     ssm_b_glu[i], wo[:attn_w], wo[attn_w:])
        h = _ffn(h, ffn2_norm[i], _ff_tiles(ffn2_w_gate, i, 1), _ff_tiles(ffn2_w_up, i, 1),
                 _ff_tiles(ffn2_w_down, i, 0))
        h = _ple(h, p.reshape(depth, b * s, -1), i, ple_norm[i], ple_w_gate[i].astype(BF16),
                 ple_w_proj[i].astype(BF16), final_norm, final=i == depth - 1)
    return h.reshape(b, s, d)
```

```python
import functools
import math

import jax
import jax.numpy as jnp
from jax import lax
from jax.experimental import pallas as pl
from jax.experimental.pallas import tpu as pltpu

NORM_EPS = 1e-6
MASK_VALUE = -1e30
HEAD_DIM = 64
ATTN_BLOCK = 128
ATTN_SPAN = 128
DILATIONS = (1, 4, 16)
ATTN_UNROLL = 16
SSM_GROUP = 16
SSM_STATE = 64
SSM_CHUNK = 16
TOKEN_RES = 16
LANES = 128
GROUPS_PER_BLOCK = LANES // SSM_GROUP
FF_TILE = 256
FF_STAGE = 512
VMEM_CAP = 56 * 1024 * 1024

BF16 = jnp.bfloat16
F32 = jnp.float32


def _params(sem, vmem_bytes):
    return pltpu.CompilerParams(
        dimension_semantics=sem, vmem_limit_bytes=min(int(vmem_bytes), VMEM_CAP))


def _rms(x, g):
    ms = jnp.mean(x * x, axis=-1, keepdims=True)
    return x * lax.rsqrt(ms + NORM_EPS) * g


def _dot(a, b):
    return jnp.dot(a, b, preferred_element_type=F32)


def _const_spec(shape):
    return pl.BlockSpec(shape, lambda *_: (0,) * len(shape),
                        pipeline_mode=pl.Buffered(1))


def _ffn_kernel(x_ref, g_ref, wg_ref, wu_ref, wd_ref, o_ref, xn_ref):
    f = pl.program_id(1)

    @pl.when(f == 0)
    def _():
        xn_ref[...] = _rms(x_ref[...], g_ref[...]).astype(BF16)
        o_ref[...] = jnp.zeros_like(o_ref)

    xn = xn_ref[...]
    gate = _dot(xn, wg_ref[...])
    up = _dot(xn, wu_ref[...])
    act = (gate * jax.nn.sigmoid(gate) * up).astype(BF16)
    o_ref[...] += _dot(act, wd_ref[...])

    @pl.when(f == pl.num_programs(1) - 1)
    def _():
        o_ref[...] = x_ref[...] + 0.5 * o_ref[...]


def _ffn(h, g, wg, wu, wd, *, tm=1024):
    m, d = h.shape
    tf = wg.shape[1]
    nf = wg.shape[0] // d
    tm = min(tm, m)
    vmem = (4 * tm * d * 4 + tm * d * 2 + 6 * d * tf * 2 + 6 * tm * tf * 4) * 1.25
    row = pl.BlockSpec((tm, d), lambda i, f: (i, 0))
    return pl.pallas_call(
        _ffn_kernel,
        grid=(m // tm, nf),
        in_specs=[
            row,
            pl.BlockSpec((1, d), lambda i, f: (0, 0)),
            pl.BlockSpec((d, tf), lambda i, f: (f, 0)),
            pl.BlockSpec((d, tf), lambda i, f: (f, 0)),
            pl.BlockSpec((tf, d), lambda i, f: (f, 0)),
        ],
        out_specs=row,
        out_shape=jax.ShapeDtypeStruct((m, d), F32),
        scratch_shapes=[pltpu.VMEM((tm, d), BF16)],
        compiler_params=_params(("parallel", "arbitrary"), vmem),
        name="ffn",
    )(h, g.reshape(1, d), wg, wu, wd)


def _ff_tile_kernel(w_ref, o_ref, *, axis, size):
    w = w_ref[...]
    pos = lax.broadcasted_iota(jnp.int32, w.shape, axis) + pl.program_id(0) * w.shape[axis]
    w = jnp.where(pos < size, w, 0.0).astype(o_ref.dtype)
    if axis == 0:
        o_ref[...] = w
    else:
        rows, tile = w.shape[0], o_ref.shape[1]
        for t in range(w.shape[1] // tile):
            o_ref[t * rows:(t + 1) * rows, :] = w[:, t * tile:(t + 1) * tile]


def _ff_tiles(w, layer, axis, tf=FF_TILE):
    rows, cols = w.shape[1:]
    size = w.shape[1 + axis]
    per_step = max(1, FF_STAGE // tf)
    steps = pl.cdiv(size, tf * per_step)
    nf = steps * per_step
    if axis == 0:
        in_spec = pl.BlockSpec((None, tf * per_step, cols), lambda f: (layer, f, 0))
        out_spec = pl.BlockSpec((tf * per_step, cols), lambda f: (f, 0))
        out_shape = (nf * tf, cols)
    else:
        in_spec = pl.BlockSpec((None, rows, tf * per_step), lambda f: (layer, 0, f))
        out_spec = pl.BlockSpec((per_step * rows, tf), lambda f: (f, 0))
        out_shape = (nf * rows, tf)
    return pl.pallas_call(
        functools.partial(_ff_tile_kernel, axis=axis, size=size),
        grid=(steps,),
        in_specs=[in_spec],
        out_specs=out_spec,
        out_shape=jax.ShapeDtypeStruct(out_shape, BF16),
        compiler_params=_params(("parallel",), 6 * rows * cols // steps * 4 + (4 << 20)),
        name="ff_tiles",
    )(w)


def _residue_perm(n_res, tl, transpose):
    tm = n_res * tl
    assert tl & (tl - 1) == 0
    a = lax.broadcasted_iota(jnp.int32, (tm, tm), 1 if transpose else 0)
    b = lax.broadcasted_iota(jnp.int32, (tm, tm), 0 if transpose else 1)
    hit = b == n_res * (a & (tl - 1)) + (a >> (tl.bit_length() - 1))
    return jnp.where(hit, 1.0, 0.0).astype(BF16)


def _in_proj_kernel(x_ref, g_ref, w_ref, o_ref):
    n_res, tl, _ = o_ref.shape
    xn = _rms(x_ref[...], g_ref[...]).astype(BF16)
    xn = _dot(_residue_perm(n_res, tl, False), xn).astype(BF16)
    res = _dot(xn, w_ref[...])
    for r in range(n_res):
        o_ref[r] = res[r * tl:(r + 1) * tl]


def _in_proj(h, g, w, batch, *, tm=256):
    m, d = h.shape
    n = w.shape[1]
    seq = m // batch
    tm = min(tm, seq)
    tl = tm // TOKEN_RES
    tiles = seq // tm
    assert tl % 8 == 0 and seq % tm == 0
    vmem = (2 * tm * d * 4 + tm * d * 6 + d * n * 2 + 4 * tm * n * 4 + 4 * tm * tm) * 1.2
    return pl.pallas_call(
        _in_proj_kernel,
        grid=(m // tm,),
        in_specs=[
            pl.BlockSpec((tm, d), lambda i: (i, 0)),
            _const_spec((1, d)),
            _const_spec((d, n)),
        ],
        out_specs=pl.BlockSpec((None, TOKEN_RES, tl, n),
                               lambda i: (i // tiles, 0, i % tiles, 0)),
        out_shape=jax.ShapeDtypeStruct((batch, TOKEN_RES, seq // TOKEN_RES, n), F32),
        compiler_params=_params(("parallel",), vmem),
        name="in_proj",
    )(h, g.reshape(1, d), w)


def _attn_kernel(q_ref, k_ref, v_ref, o_ref, m_ref, l_ref, bias_ref, p_sc, mx_sc):
    n_res, length, _ = q_ref.shape
    blk = ATTN_BLOCK
    n_units = n_res * length // blk
    head0 = lax.broadcasted_iota(jnp.int32, (blk, LANES), 1) < HEAD_DIM
    kv_head0 = lax.broadcasted_iota(jnp.int32, (2 * blk, LANES), 1) < HEAD_DIM
    scale = HEAD_DIM ** -0.5 * math.log2(math.e)

    rho = lax.broadcasted_iota(jnp.int32, (blk, 2 * blk), 0)
    kap = lax.broadcasted_iota(jnp.int32, (blk, 2 * blk), 1)
    for pattern, d in enumerate(DILATIONS):
        n_sub = n_res // d
        c_rows = blk // n_sub
        sh = c_rows.bit_length() - 1
        iq, lq = rho >> sh, rho & (c_rows - 1)
        ik, lk = kap >> (sh + 1), kap & (2 * c_rows - 1)
        for shift in range(2):
            dist = n_sub * (lq - lk + shift * c_rows) + (iq - ik)
            bias_ref[2 * pattern + shift] = jnp.where(
                (dist >= 0) & (dist <= ATTN_SPAN), 0.0, MASK_VALUE)

    def probs(q, k, bias, sel):
        qh = jnp.where(sel, q, 0.0).astype(BF16)
        s = lax.dot_general(qh, k, (((1,), (1,)), ((), ())),
                            preferred_element_type=F32) + bias
        m = jnp.max(s, axis=-1, keepdims=True)
        return jnp.exp2(s - m).astype(BF16), m

    n_groups = n_units // ATTN_UNROLL
    for pattern, d in enumerate(DILATIONS):
        first = pattern == 0
        last = pattern == len(DILATIONS) - 1
        n_sub = n_res // d
        c_rows = blk // n_sub
        blocks_per_residue = n_units // d

        def gather(ref, res, start, size, d=d, n_sub=n_sub):
            return jnp.concatenate(
                [ref[d * i + res, pl.ds(start, size), :] for i in range(n_sub)], axis=0)

        def scatter(ref, res, start, val, d=d, n_sub=n_sub, c_rows=c_rows):
            for i in range(n_sub):
                ref[d * i + res, pl.ds(start, c_rows), :] = val[i * c_rows:(i + 1) * c_rows]

        def place(g, uu, c_rows=c_rows, blocks_per_residue=blocks_per_residue):
            u = g * ATTN_UNROLL + uu
            res = u // blocks_per_residue
            bq = u % blocks_per_residue
            kb = jnp.maximum(bq - 1, 0)
            return (res, pl.multiple_of(c_rows * bq, 8), pl.multiple_of(c_rows * kb, 8), bq - kb)

        def score_stage(g, slot, pattern=pattern, c_rows=c_rows, gather=gather, place=place):
            loaded = []
            for uu in range(ATTN_UNROLL):
                res, q0, k0, shift = place(g, uu)
                loaded.append((gather(q_ref, res, q0, c_rows) * scale,
                               gather(k_ref, res, k0, 2 * c_rows).astype(BF16),
                               bias_ref[2 * pattern + shift]))
            for uu, (q, k, bias) in enumerate(loaded):
                p0, m0 = probs(q, k, bias, head0)
                p1, m1 = probs(q, k, bias, jnp.logical_not(head0))
                p_sc[slot, uu, 0] = p0
                p_sc[slot, uu, 1] = p1
                mx_sc[slot, uu] = jnp.where(head0, m0, m1)

        def value_stage(g, slot, first=first, last=last, c_rows=c_rows,
                        gather=gather, scatter=scatter, place=place):
            loaded = []
            for uu in range(ATTN_UNROLL):
                res, q0, k0, _ = place(g, uu)
                old = None if first else tuple(
                    gather(ref, res, q0, c_rows) for ref in (o_ref, m_ref, l_ref))
                loaded.append((res, q0, gather(v_ref, res, k0, 2 * c_rows).astype(BF16),
                               p_sc[slot, uu, 0], p_sc[slot, uu, 1], mx_sc[slot, uu], old))
            results = []
            for res, q0, v, p0, p1, m_new, old in loaded:
                ones = jnp.ones_like(v)
                o0 = _dot(p0, jnp.where(kv_head0, v, ones))
                o1 = _dot(p1, jnp.where(kv_head0, ones, v))
                o_new = jnp.where(head0, o0, o1)
                l_new = pltpu.roll(jnp.where(head0, o1, o0), HEAD_DIM, 1)
                if not first:
                    o_old, m_old, l_old = old
                    m_tot = jnp.maximum(m_old, m_new)
                    w_old = jnp.exp2(m_old - m_tot)
                    w_new = jnp.exp2(m_new - m_tot)
                    o_new = w_old * o_old + w_new * o_new
                    l_new = w_old * l_old + w_new * l_new
                    m_new = m_tot
                results.append((res, q0, o_new, m_new, l_new))
            for res, q0, o_new, m_new, l_new in results:
                if last:
                    scatter(o_ref, res, q0, o_new / l_new)
                else:
                    scatter(o_ref, res, q0, o_new)
                    scatter(m_ref, res, q0, m_new)
                    scatter(l_ref, res, q0, l_new)

        def step(g, carry, score_stage=score_stage, value_stage=value_stage):
            slot = g & 1
            value_stage(g - 1, 1 - slot)
            score_stage(g, slot)
            return carry

        score_stage(0, 0)
        lax.fori_loop(1, n_groups, step, 0)
        value_stage(n_groups - 1, (n_groups - 1) & 1)


def _attention(z, n_pairs, *, q_col, k_col, v_col):
    b, n_res, length, _ = z.shape
    assert n_res == TOKEN_RES == DILATIONS[-1] and all(n_res % d == 0 for d in DILATIONS)
    assert length % ATTN_BLOCK == 0 and length >= 2 * ATTN_BLOCK and ATTN_BLOCK // n_res >= 8
    assert (n_res * length // ATTN_BLOCK) % ATTN_UNROLL == 0

    def spec(col):
        return pl.BlockSpec((None, n_res, length, LANES), lambda bi, h: (bi, 0, 0, col + h))

    slab = n_res * length * LANES * 4
    vmem = 10 * slab * 1.15 + (8 << 20)
    return pl.pallas_call(
        _attn_kernel,
        grid=(b, n_pairs),
        in_specs=[spec(q_col), spec(k_col), spec(v_col)],
        out_specs=spec(0),
        out_shape=jax.ShapeDtypeStruct((b, n_res, length, n_pairs * LANES), F32),
        scratch_shapes=[pltpu.VMEM((n_res, length, LANES), F32),
                        pltpu.VMEM((n_res, length, LANES), F32),
                        pltpu.VMEM((2 * len(DILATIONS), ATTN_BLOCK, 2 * ATTN_BLOCK), F32),
                        pltpu.VMEM((2, ATTN_UNROLL, 2, ATTN_BLOCK, 2 * ATTN_BLOCK), BF16),
                        pltpu.VMEM((2, ATTN_UNROLL, ATTN_BLOCK, LANES), F32)],
        compiler_params=_params(("parallel", "parallel"), vmem),
        name="attention",
    )(z, z, z)


def _ssm_weights(lam_re, lam_im, log_dt, b_re, b_im, c_re, c_im, d_skip, n_chunks):
    hp = lax.Precision.HIGHEST
    g, p = lam_re.shape
    c = b_re.shape[-1]
    t = SSM_CHUNK
    gb = GROUPS_PER_BLOCK
    nj = g // gb
    lr, li = lam_re.astype(F32), lam_im.astype(F32)
    dt = jnp.exp(log_dt.astype(F32))[:, None]
    mag = jnp.exp(lr * dt)
    ar = mag * jnp.cos(li * dt)
    ai = mag * jnp.sin(li * dt)
    nr, ni = ar - 1.0, ai
    den = lr * lr + li * li
    cr = (nr * lr + ni * li) / den
    ci = (ni * lr - nr * li) / den
    br, bi = b_re.astype(F32), b_im.astype(F32)
    bbr = cr[..., None] * br - ci[..., None] * bi
    bbi = cr[..., None] * bi + ci[..., None] * br

    def power(k):
        kk = k.astype(F32)[:, None, None]
        pm = jnp.exp(kk * (lr * dt))
        ang = kk * (li * dt)
        return pm * jnp.cos(ang), pm * jnp.sin(ang)

    pr, pi = power(jnp.arange(t + 1))
    cre, cim = c_re.astype(F32), c_im.astype(F32)
    car = cre[None] * pr[:, :, None, :] - cim[None] * pi[:, :, None, :]
    cai = cre[None] * pi[:, :, None, :] + cim[None] * pr[:, :, None, :]
    kern = jnp.einsum('kgdq,gqc->kgdc', jnp.concatenate([car[:t], -cai[:t]], axis=-1),
                      jnp.concatenate([bbr, bbi], axis=1), precision=hp)
    kern = kern.at[0].add(d_skip.astype(F32).reshape(g, c)[:, :, None] * jnp.eye(c, dtype=F32))

    kr = kern.reshape(t, nj, gb, c, c).transpose(1, 0, 4, 2, 3).reshape(nj, t, c, gb * c)

    rev_r, rev_i = pr[:t][::-1][:, :, None, :], pi[:t][::-1][:, :, None, :]
    bbr_t, bbi_t = bbr.transpose(0, 2, 1)[None], bbi.transpose(0, 2, 1)[None]
    abr = rev_r * bbr_t - rev_i * bbi_t
    abi = rev_r * bbi_t + rev_i * bbr_t

    def block_rows(x):
        return x.reshape(t, nj, gb, c, p).transpose(1, 0, 2, 3, 4).reshape(nj, t * gb * c, p)

    def rows_form(x):
        x = block_rows(x)
        return jnp.concatenate([x, x], axis=-1).astype(BF16)

    def cols_form(x):
        x = block_rows(x).transpose(0, 2, 1)
        return jnp.concatenate([x, x], axis=1).astype(BF16)

    n_steps = max(1, (n_chunks - 1).bit_length())
    sr, si = power(t * (2 ** jnp.arange(n_steps)))
    rows = -(-n_steps // 8) * 8
    sr = jnp.pad(sr.reshape(n_steps, nj, gb * p).transpose(1, 0, 2), ((0, 0), (0, rows - n_steps), (0, 0)))
    si = jnp.pad(si.reshape(n_steps, nj, gb * p).transpose(1, 0, 2), ((0, 0), (0, rows - n_steps), (0, 0)))
    return (kr.astype(BF16), rows_form(abr), rows_form(abi),
            cols_form(car[1:]), cols_form(-cai[1:]), sr, si)


def _ssm_expand(kr_ref, abr_ref, abi_ref, cmr_ref, cmi_ref, m_sc, b_sc, c_sc):
    t, c, gb, p = SSM_CHUNK, SSM_GROUP, GROUPS_PER_BLOCK, SSM_STATE
    shift = c.bit_length() - 1
    half = gb * p
    pairs = LANES // p

    def group_of(idx):
        return (idx >> shift) & (gb - 1)

    m_sc[...] = jnp.zeros_like(m_sc)
    col_g = group_of(lax.broadcasted_iota(jnp.int32, (c, LANES), 1))
    for k in range(t):
        kk = kr_ref[k]
        dk = jnp.concatenate([jnp.where(col_g == g, kk, jnp.zeros_like(kk)) for g in range(gb)], axis=0)
        for tau in range(t - k):
            m_sc[tau * LANES:(tau + 1) * LANES, (tau + k) * LANES:(tau + k + 1) * LANES] = dk

    row_g = group_of(lax.broadcasted_iota(jnp.int32, (t * LANES, LANES), 0))
    lane_half = lax.broadcasted_iota(jnp.int32, (t * LANES, LANES), 1) >> (p.bit_length() - 1)
    for part, ab_ref in enumerate((abr_ref, abi_ref)):
        ab = ab_ref[...]
        for q in range(gb // pairs):
            b_sc[:, part * half + q * LANES:part * half + (q + 1) * LANES] = jnp.where(
                row_g == pairs * q + lane_half, ab, jnp.zeros_like(ab))

    col_g2 = group_of(lax.broadcasted_iota(jnp.int32, (LANES, t * LANES), 1))
    row_half = lax.broadcasted_iota(jnp.int32, (LANES, t * LANES), 0) >> (p.bit_length() - 1)
    for part, cm_ref in enumerate((cmr_ref, cmi_ref)):
        cm = cm_ref[...]
        for q in range(gb // pairs):
            c_sc[part * half + q * LANES:part * half + (q + 1) * LANES, :] = jnp.where(
                col_g2 == pairs * q + row_half, cm, jnp.zeros_like(cm))


def _ssm_kernel(s_ref, kr_ref, abr_ref, abi_ref, cmr_ref, cmi_ref, sr_ref, si_ref, y_ref,
                m_sc, b_sc, c_sc):
    t, n_chunks, _ = s_ref.shape
    half = GROUPS_PER_BLOCK * SSM_STATE

    @pl.when(pl.program_id(1) == 0)
    def _():
        _ssm_expand(kr_ref, abr_ref, abi_ref, cmr_ref, cmi_ref, m_sc, b_sc, c_sc)

    u = jnp.concatenate([s_ref[tau].astype(BF16) for tau in range(t)], axis=1)
    y = _dot(u, m_sc[...])
    x = _dot(u, b_sc[...])
    hr, hi = x[:, :half], x[:, half:]

    row = lax.broadcasted_iota(jnp.int32, (n_chunks, half), 0)

    def shifted(h, k):
        return jnp.where(row >= k, pltpu.roll(h, k, 0), 0.0)

    step = 0
    while (1 << step) < n_chunks:
        k = 1 << step
        pr = sr_ref[step:step + 1, :]
        pi = si_ref[step:step + 1, :]
        zr, zi = shifted(hr, k), shifted(hi, k)
        hr, hi = hr + pr * zr - pi * zi, hi + pr * zi + pi * zr
        step += 1

    h_prev = jnp.concatenate([shifted(hr, 1), shifted(hi, 1)], axis=1).astype(BF16)
    y = jax.nn.gelu(y + _dot(h_prev, c_sc[...]))
    for tau in range(t):
        y_ref[tau] = y[:, tau * LANES:(tau + 1) * LANES]


def _ssm(z, s_col, weights):
    b, n_res, n_chunks, _ = z.shape
    nj = weights[0].shape[0]
    assert n_res == SSM_CHUNK
    assert SSM_GROUP & (SSM_GROUP - 1) == 0 and SSM_STATE & (SSM_STATE - 1) == 0

    def wspec(a):
        nd = a.ndim - 1
        return pl.BlockSpec((None,) + a.shape[1:], lambda j, bi: (j,) + (0,) * nd)

    s = n_res * n_chunks
    width = SSM_CHUNK * LANES
    state = 2 * GROUPS_PER_BLOCK * SSM_STATE
    scratch = (width * width + 2 * width * state) * 2
    wbytes = 2 * sum(math.prod(a.shape[1:]) * a.dtype.itemsize for a in weights)
    vmem = (4 * s * LANES * 4 + wbytes + scratch + n_chunks * (width * 10 + state * 12)) * 1.2
    return pl.pallas_call(
        _ssm_kernel,
        grid=(nj, b),
        in_specs=[pl.BlockSpec((None, n_res, n_chunks, LANES),
                               lambda j, bi: (bi, 0, 0, s_col + j))]
        + [wspec(a) for a in weights],
        out_specs=pl.BlockSpec((None, n_res, n_chunks, LANES), lambda j, bi: (bi, 0, 0, j)),
        out_shape=jax.ShapeDtypeStruct((b, n_res, n_chunks, nj * LANES), F32),
        scratch_shapes=[pltpu.VMEM((width, width), BF16), pltpu.VMEM((width, state), BF16),
                        pltpu.VMEM((state, width), BF16)],
        compiler_params=_params(("parallel", "arbitrary"), vmem),
        name="ssm",
    )(z, *weights)


def _mix_out_kernel(ya_ref, y_ref, h_ref, ga_ref, gb_ref, wglu_ref, bglu_ref,
                    wa_ref, wb_ref, o_ref):
    n_res, tl, _ = ya_ref.shape
    y = y_ref[...].reshape(n_res * tl, -1)
    gate = _dot(y.astype(BF16), wglu_ref[...]) + bglu_ref[...]
    yb = y * jax.nn.sigmoid(gate)
    na = _rms(ya_ref[...].reshape(n_res * tl, -1), ga_ref[...]).astype(BF16)
    nb = _rms(yb, gb_ref[...]).astype(BF16)
    perm = _residue_perm(n_res, tl, True)
    na = _dot(perm, na).astype(BF16)
    nb = _dot(perm, nb).astype(BF16)
    o_ref[...] = h_ref[...] + _dot(na, wa_ref[...]) + _dot(nb, wb_ref[...])


def _mix_out(ya, y, h, ga, gb, wglu, bglu, wa, wb, *, tm=512):
    m, d = h.shape
    batch, n_res, length, wa_w = ya.shape
    wb_w = y.shape[-1]
    tm = min(tm, n_res * length)
    tl = tm // n_res
    tiles = length // tl
    assert tl % 8 == 0 and length % tl == 0
    vmem = (2 * tm * (wa_w + wb_w + 2 * d) * 4 + (wglu.size + wa.size + wb.size) * 2
            + 8 * tm * d * 4) * 1.2
    row = pl.BlockSpec((tm, d), lambda i: (i, 0))
    res_major = lambda w: pl.BlockSpec((None, n_res, tl, w),
                                       lambda i: (i // tiles, 0, i % tiles, 0))
    return pl.pallas_call(
        _mix_out_kernel,
        grid=(m // tm,),
        in_specs=[res_major(wa_w), res_major(wb_w), row,
                  _const_spec((1, wa_w)), _const_spec((1, wb_w)),
                  _const_spec(wglu.shape), _const_spec((1, wb_w)),
                  _const_spec(wa.shape), _const_spec(wb.shape)],
        out_specs=row,
        out_shape=jax.ShapeDtypeStruct((m, d), F32),
        compiler_params=_params(("parallel",), vmem),
        name="mix_out",
    )(ya, y, h, ga.reshape(1, -1), gb.reshape(1, -1), wglu, bglu.reshape(1, -1), wa, wb)


def _ple_kernel(h_ref, p_ref, gn_ref, wg_ref, wp_ref, fn_ref, o_ref, *, final):
    h = h_ref[...]
    gate = jax.nn.sigmoid(_dot(_rms(h, gn_ref[...]).astype(BF16), wg_ref[...]))
    proj = _dot(p_ref[...].astype(BF16), wp_ref[...])
    out = h + gate * proj
    o_ref[...] = _rms(out, fn_ref[...]) if final else out


def _ple(h, p, layer, gn, wg, wp, fn, *, final, tm=512):
    m, d = h.shape
    pd = p.shape[2]
    tm = min(tm, m)
    vmem = (2 * tm * (2 * d + pd) * 4 + (wg.size + wp.size) * 2 + 8 * tm * d * 4) * 1.2
    row = lambda w: pl.BlockSpec((tm, w), lambda i: (i, 0))
    return pl.pallas_call(
        functools.partial(_ple_kernel, final=final),
        grid=(m // tm,),
        in_specs=[row(d), pl.BlockSpec((None, tm, pd), lambda i: (layer, i, 0)),
                  _const_spec((1, d)), _const_spec(wg.shape),
                  _const_spec(wp.shape), _const_spec((1, d))],
        out_specs=row(d),
        out_shape=jax.ShapeDtypeStruct((m, d), F32),
        compiler_params=_params(("parallel",), vmem),
        name="ple",
    )(h, p, gn.reshape(1, d), wg, wp, fn.reshape(1, d))


def kernel(x, p, ffn1_norm, ffn1_w_gate, ffn1_w_up, ffn1_w_down, mix_norm, w_in, attn_out_norm, ssm_lambda_re, ssm_lambda_im, ssm_log_dt, ssm_b_re, ssm_b_im, ssm_c_re, ssm_c_im, ssm_d, ssm_w_glu, ssm_b_glu, ssm_out_norm, w_out, ffn2_norm, ffn2_w_gate, ffn2_w_up, ffn2_w_down, ple_norm, ple_w_gate, ple_w_proj, final_norm):
    b, s, d = x.shape
    depth = p.shape[0]
    attn_w = attn_out_norm.shape[1]
    ssm_w = ssm_out_norm.shape[1]
    n_pairs = attn_w // LANES
    h = x.reshape(b * s, d)
    for i in range(depth):
        h = _ffn(h, ffn1_norm[i], _ff_tiles(ffn1_w_gate, i, 1), _ff_tiles(ffn1_w_up, i, 1),
                 _ff_tiles(ffn1_w_down, i, 0))
        z = _in_proj(h, mix_norm[i], w_in[i].astype(BF16), b)
        ya = _attention(z, n_pairs, q_col=0, k_col=n_pairs, v_col=2 * n_pairs)
        weights = _ssm_weights(ssm_lambda_re[i], ssm_lambda_im[i], ssm_log_dt[i],
                               ssm_b_re[i], ssm_b_im[i], ssm_c_re[i], ssm_c_im[i],
                               ssm_d[i], s // SSM_CHUNK)
        y = _ssm(z, 3 * n_pairs, weights)
        wo = w_out[i].astype(BF16)
        h = _mix_out(ya, y, h, attn_out_norm[i], ssm_out_norm[i], ssm_w_glu[i].astype(BF16),
                     ssm_b_glu[i], wo[:attn_w], wo[attn_w:])
        h = _ffn(h, ffn2_norm[i], _ff_tiles(ffn2_w_gate, i, 1), _ff_tiles(ffn2_w_up, i, 1),
                 _ff_tiles(ffn2_w_down, i, 0))
        h = _ple(h, p.reshape(depth, b * s, -1), i, ple_norm[i], ple_w_gate[i].astype(BF16),
                 ple_w_proj[i].astype(BF16), final_norm, final=i == depth - 1)
    return h.reshape(b, s, d)
```

```python
import functools
import math

import jax
import jax.numpy as jnp
from jax import lax
from jax.experimental import pallas as pl
from jax.experimental.pallas import tpu as pltpu

NORM_EPS = 1e-6
MASK_VALUE = -1e30
HEAD_DIM = 64
ATTN_BLOCK = 128
ATTN_SPAN = 128
DILATIONS = (1, 4, 16)
ATTN_UNROLL = 16
SSM_GROUP = 16
SSM_STATE = 64
SSM_CHUNK = 16
TOKEN_RES = 16
LANES = 128
GROUPS_PER_BLOCK = LANES // SSM_GROUP
FF_TILE = 256
FF_STAGE = 512
VMEM_CAP = 56 * 1024 * 1024

BF16 = jnp.bfloat16
F32 = jnp.float32


def _params(sem, vmem_bytes):
    return pltpu.CompilerParams(
        dimension_semantics=sem, vmem_limit_bytes=min(int(vmem_bytes), VMEM_CAP))


def _rms(x, g):
    ms = jnp.mean(x * x, axis=-1, keepdims=True)
    return x * lax.rsqrt(ms + NORM_EPS) * g


def _dot(a, b):
    return jnp.dot(a, b, preferred_element_type=F32)


def _const_spec(shape):
    return pl.BlockSpec(shape, lambda *_: (0,) * len(shape),
                        pipeline_mode=pl.Buffered(1))


def _ffn_kernel(x_ref, g_ref, wg_ref, wu_ref, wd_ref, o_ref, xn_ref):
    f = pl.program_id(1)

    @pl.when(f == 0)
    def _():
        xn_ref[...] = _rms(x_ref[...], g_ref[...]).astype(BF16)
        o_ref[...] = jnp.zeros_like(o_ref)

    xn = xn_ref[...]
    gate = _dot(xn, wg_ref[...])
    up = _dot(xn, wu_ref[...])
    act = (gate * jax.nn.sigmoid(gate) * up).astype(BF16)
    o_ref[...] += _dot(act, wd_ref[...])

    @pl.when(f == pl.num_programs(1) - 1)
    def _():
        o_ref[...] = x_ref[...] + 0.5 * o_ref[...]


def _ffn(h, g, wg, wu, wd, *, tm=1024):
    m, d = h.shape
    nf, _, tf = wg.shape
    tm = min(tm, m)
    vmem = (4 * tm * d * 4 + tm * d * 2 + 6 * d * tf * 2 + 6 * tm * tf * 4) * 1.25
    row = pl.BlockSpec((tm, d), lambda i, f: (i, 0))
    return pl.pallas_call(
        _ffn_kernel,
        grid=(m // tm, nf),
        in_specs=[
            row,
            pl.BlockSpec((1, d), lambda i, f: (0, 0)),
            pl.BlockSpec((None, d, tf), lambda i, f: (f, 0, 0)),
            pl.BlockSpec((None, d, tf), lambda i, f: (f, 0, 0)),
            pl.BlockSpec((None, tf, d), lambda i, f: (f, 0, 0)),
        ],
        out_specs=row,
        out_shape=jax.ShapeDtypeStruct((m, d), F32),
        scratch_shapes=[pltpu.VMEM((tm, d), BF16)],
        compiler_params=_params(("parallel", "arbitrary"), vmem),
        name="ffn",
    )(h, g.reshape(1, d), wg, wu, wd)


def _ff_tile_kernel(w_ref, o_ref, *, axis, size):
    w = w_ref[...]
    n_tiles = o_ref.shape[0]
    tile = o_ref.shape[1 + axis]
    pos = lax.broadcasted_iota(jnp.int32, w.shape, axis) + pl.program_id(0) * (n_tiles * tile)
    w = jnp.where(pos < size, w, 0.0).astype(o_ref.dtype)
    for t in range(n_tiles):
        o_ref[t] = w[:, t * tile:(t + 1) * tile] if axis == 1 else w[t * tile:(t + 1) * tile]


def _ff_tiles(w, layer, axis, tf=FF_TILE):
    size = w.shape[1 + axis]
    per_step = max(1, FF_STAGE // tf)
    nf = pl.cdiv(size, tf * per_step) * per_step
    tile = list(w.shape[1:])
    tile[axis] = tf
    stage = list(tile)
    stage[axis] = tf * per_step
    index = (lambda f: (layer, 0, f)) if axis == 1 else (lambda f: (layer, f, 0))
    return pl.pallas_call(
        functools.partial(_ff_tile_kernel, axis=axis, size=size),
        grid=(nf // per_step,),
        in_specs=[pl.BlockSpec((None,) + tuple(stage), index)],
        out_specs=pl.BlockSpec((per_step,) + tuple(tile), lambda f: (f, 0, 0)),
        out_shape=jax.ShapeDtypeStruct((nf,) + tuple(tile), BF16),
        compiler_params=_params(("parallel",), 6 * math.prod(stage) * 4),
        name="ff_tiles",
    )(w)


def _residue_perm(n_res, tl, transpose):
    tm = n_res * tl
    assert tl & (tl - 1) == 0
    a = lax.broadcasted_iota(jnp.int32, (tm, tm), 1 if transpose else 0)
    b = lax.broadcasted_iota(jnp.int32, (tm, tm), 0 if transpose else 1)
    hit = b == n_res * (a & (tl - 1)) + (a >> (tl.bit_length() - 1))
    return jnp.where(hit, 1.0, 0.0).astype(BF16)


def _in_proj_kernel(x_ref, g_ref, w_ref, o_ref):
    n_res, tl, _ = o_ref.shape
    xn = _rms(x_ref[...], g_ref[...]).astype(BF16)
    xn = _dot(_residue_perm(n_res, tl, False), xn).astype(BF16)
    res = _dot(xn, w_ref[...])
    for r in range(n_res):
        o_ref[r] = res[r * tl:(r + 1) * tl]


def _in_proj(h, g, w, batch, *, tm=256):
    m, d = h.shape
    n = w.shape[1]
    seq = m // batch
    tm = min(tm, seq)
    tl = tm // TOKEN_RES
    tiles = seq // tm
    assert tl % 8 == 0 and seq % tm == 0
    vmem = (2 * tm * d * 4 + tm * d * 6 + d * n * 2 + 4 * tm * n * 4 + 4 * tm * tm) * 1.2
    return pl.pallas_call(
        _in_proj_kernel,
        grid=(m // tm,),
        in_specs=[
            pl.BlockSpec((tm, d), lambda i: (i, 0)),
            _const_spec((1, d)),
            _const_spec((d, n)),
        ],
        out_specs=pl.BlockSpec((None, TOKEN_RES, tl, n),
                               lambda i: (i // tiles, 0, i % tiles, 0)),
        out_shape=jax.ShapeDtypeStruct((batch, TOKEN_RES, seq // TOKEN_RES, n), F32),
        compiler_params=_params(("parallel",), vmem),
        name="in_proj",
    )(h, g.reshape(1, d), w)


def _attn_kernel(q_ref, k_ref, v_ref, o_ref, m_ref, l_ref, bias_ref, p_sc, mx_sc):
    n_res, length, _ = q_ref.shape
    blk = ATTN_BLOCK
    n_units = n_res * length // blk
    head0 = lax.broadcasted_iota(jnp.int32, (blk, LANES), 1) < HEAD_DIM
    kv_head0 = lax.broadcasted_iota(jnp.int32, (2 * blk, LANES), 1) < HEAD_DIM
    scale = HEAD_DIM ** -0.5 * math.log2(math.e)

    rho = lax.broadcasted_iota(jnp.int32, (blk, 2 * blk), 0)
    kap = lax.broadcasted_iota(jnp.int32, (blk, 2 * blk), 1)
    for pattern, d in enumerate(DILATIONS):
        n_sub = n_res // d
        c_rows = blk // n_sub
        sh = c_rows.bit_length() - 1
        iq, lq = rho >> sh, rho & (c_rows - 1)
        ik, lk = kap >> (sh + 1), kap & (2 * c_rows - 1)
        for shift in range(2):
            dist = n_sub * (lq - lk + shift * c_rows) + (iq - ik)
            bias_ref[2 * pattern + shift] = jnp.where(
                (dist >= 0) & (dist <= ATTN_SPAN), 0.0, MASK_VALUE)

    def probs(q, k, bias, sel):
        qh = jnp.where(sel, q, 0.0).astype(BF16)
        s = lax.dot_general(qh, k, (((1,), (1,)), ((), ())),
                            preferred_element_type=F32) + bias
        m = jnp.max(s, axis=-1, keepdims=True)
        return jnp.exp2(s - m).astype(BF16), m

    n_groups = n_units // ATTN_UNROLL
    for pattern, d in enumerate(DILATIONS):
        first = pattern == 0
        last = pattern == len(DILATIONS) - 1
        n_sub = n_res // d
        c_rows = blk // n_sub
        blocks_per_residue = n_units // d

        def gather(ref, res, start, size, d=d, n_sub=n_sub):
            return jnp.concatenate(
                [ref[d * i + res, pl.ds(start, size), :] for i in range(n_sub)], axis=0)

        def scatter(ref, res, start, val, d=d, n_sub=n_sub, c_rows=c_rows):
            for i in range(n_sub):
                ref[d * i + res, pl.ds(start, c_rows), :] = val[i * c_rows:(i + 1) * c_rows]

        def place(g, uu, c_rows=c_rows, blocks_per_residue=blocks_per_residue):
            u = g * ATTN_UNROLL + uu
            res = u // blocks_per_residue
            bq = u % blocks_per_residue
            kb = jnp.maximum(bq - 1, 0)
            return (res, pl.multiple_of(c_rows * bq, 8), pl.multiple_of(c_rows * kb, 8), bq - kb)

        def score_stage(g, slot, pattern=pattern, c_rows=c_rows, gather=gather, place=place):
            loaded = []
            for uu in range(ATTN_UNROLL):
                res, q0, k0, shift = place(g, uu)
                loaded.append((gather(q_ref, res, q0, c_rows) * scale,
                               gather(k_ref, res, k0, 2 * c_rows).astype(BF16),
                               bias_ref[2 * pattern + shift]))
            for uu, (q, k, bias) in enumerate(loaded):
                p0, m0 = probs(q, k, bias, head0)
                p1, m1 = probs(q, k, bias, jnp.logical_not(head0))
                p_sc[slot, uu, 0] = p0
                p_sc[slot, uu, 1] = p1
                mx_sc[slot, uu] = jnp.where(head0, m0, m1)

        def value_stage(g, slot, first=first, last=last, c_rows=c_rows,
                        gather=gather, scatter=scatter, place=place):
            loaded = []
            for uu in range(ATTN_UNROLL):
                res, q0, k0, _ = place(g, uu)
                old = None if first else tuple(
                    gather(ref, res, q0, c_rows) for ref in (o_ref, m_ref, l_ref))
                loaded.append((res, q0, gather(v_ref, res, k0, 2 * c_rows).astype(BF16),
                               p_sc[slot, uu, 0], p_sc[slot, uu, 1], mx_sc[slot, uu], old))
            results = []
            for res, q0, v, p0, p1, m_new, old in loaded:
                ones = jnp.ones_like(v)
                o0 = _dot(p0, jnp.where(kv_head0, v, ones))
                o1 = _dot(p1, jnp.where(kv_head0, ones, v))
                o_new = jnp.where(head0, o0, o1)
                l_new = pltpu.roll(jnp.where(head0, o1, o0), HEAD_DIM, 1)
                if not first:
                    o_old, m_old, l_old = old
                    m_tot = jnp.maximum(m_old, m_new)
                    w_old = jnp.exp2(m_old - m_tot)
                    w_new = jnp.exp2(m_new - m_tot)
                    o_new = w_old * o_old + w_new * o_new
                    l_new = w_old * l_old + w_new * l_new
                    m_new = m_tot
                results.append((res, q0, o_new, m_new, l_new))
            for res, q0, o_new, m_new, l_new in results:
                if last:
                    scatter(o_ref, res, q0, o_new / l_new)
                else:
                    scatter(o_ref, res, q0, o_new)
                    scatter(m_ref, res, q0, m_new)
                    scatter(l_ref, res, q0, l_new)

        def step(g, carry, score_stage=score_stage, value_stage=value_stage):
            slot = g & 1
            value_stage(g - 1, 1 - slot)
            score_stage(g, slot)
            return carry

        score_stage(0, 0)
        lax.fori_loop(1, n_groups, step, 0)
        value_stage(n_groups - 1, (n_groups - 1) & 1)


def _attention(z, n_pairs, *, q_col, k_col, v_col):
    b, n_res, length, _ = z.shape
    assert n_res == TOKEN_RES == DILATIONS[-1] and all(n_res % d == 0 for d in DILATIONS)
    assert length % ATTN_BLOCK == 0 and length >= 2 * ATTN_BLOCK and ATTN_BLOCK // n_res >= 8
    assert (n_res * length // ATTN_BLOCK) % ATTN_UNROLL == 0

    def spec(col):
        return pl.BlockSpec((None, n_res, length, LANES), lambda bi, h: (bi, 0, 0, col + h))

    slab = n_res * length * LANES * 4
    vmem = 10 * slab * 1.15 + (8 << 20)
    return pl.pallas_call(
        _attn_kernel,
        grid=(b, n_pairs),
        in_specs=[spec(q_col), spec(k_col), spec(v_col)],
        out_specs=spec(0),
        out_shape=jax.ShapeDtypeStruct((b, n_res, length, n_pairs * LANES), F32),
        scratch_shapes=[pltpu.VMEM((n_res, length, LANES), F32),
                        pltpu.VMEM((n_res, length, LANES), F32),
                        pltpu.VMEM((2 * len(DILATIONS), ATTN_BLOCK, 2 * ATTN_BLOCK), F32),
                        pltpu.VMEM((2, ATTN_UNROLL, 2, ATTN_BLOCK, 2 * ATTN_BLOCK), BF16),
                        pltpu.VMEM((2, ATTN_UNROLL, ATTN_BLOCK, LANES), F32)],
        compiler_params=_params(("parallel", "parallel"), vmem),
        name="attention",
    )(z, z, z)


def _ssm_weights(lam_re, lam_im, log_dt, b_re, b_im, c_re, c_im, d_skip, n_chunks):
    hp = lax.Precision.HIGHEST
    g, p = lam_re.shape
    c = b_re.shape[-1]
    t = SSM_CHUNK
    gb = GROUPS_PER_BLOCK
    nj = g // gb
    lr, li = lam_re.astype(F32), lam_im.astype(F32)
    dt = jnp.exp(log_dt.astype(F32))[:, None]
    mag = jnp.exp(lr * dt)
    ar = mag * jnp.cos(li * dt)
    ai = mag * jnp.sin(li * dt)
    nr, ni = ar - 1.0, ai
    den = lr * lr + li * li
    cr = (nr * lr + ni * li) / den
    ci = (ni * lr - nr * li) / den
    br, bi = b_re.astype(F32), b_im.astype(F32)
    bbr = cr[..., None] * br - ci[..., None] * bi
    bbi = cr[..., None] * bi + ci[..., None] * br

    def power(k):
        kk = k.astype(F32)[:, None, None]
        pm = jnp.exp(kk * (lr * dt))
        ang = kk * (li * dt)
        return pm * jnp.cos(ang), pm * jnp.sin(ang)

    pr, pi = power(jnp.arange(t + 1))
    cre, cim = c_re.astype(F32), c_im.astype(F32)
    car = cre[None] * pr[:, :, None, :] - cim[None] * pi[:, :, None, :]
    cai = cre[None] * pi[:, :, None, :] + cim[None] * pr[:, :, None, :]
    kern = jnp.einsum('kgdq,gqc->kgdc', jnp.concatenate([car[:t], -cai[:t]], axis=-1),
                      jnp.concatenate([bbr, bbi], axis=1), precision=hp)
    kern = kern.at[0].add(d_skip.astype(F32).reshape(g, c)[:, :, None] * jnp.eye(c, dtype=F32))

    kr = kern.reshape(t, nj, gb, c, c).transpose(1, 0, 4, 2, 3).reshape(nj, t, c, gb * c)

    rev_r, rev_i = pr[:t][::-1][:, :, None, :], pi[:t][::-1][:, :, None, :]
    bbr_t, bbi_t = bbr.transpose(0, 2, 1)[None], bbi.transpose(0, 2, 1)[None]
    abr = rev_r * bbr_t - rev_i * bbi_t
    abi = rev_r * bbi_t + rev_i * bbr_t

    def block_rows(x):
        return x.reshape(t, nj, gb, c, p).transpose(1, 0, 2, 3, 4).reshape(nj, t * gb * c, p)

    def rows_form(x):
        x = block_rows(x)
        return jnp.concatenate([x, x], axis=-1).astype(BF16)

    def cols_form(x):
        x = block_rows(x).transpose(0, 2, 1)
        return jnp.concatenate([x, x], axis=1).astype(BF16)

    n_steps = max(1, (n_chunks - 1).bit_length())
    sr, si = power(t * (2 ** jnp.arange(n_steps)))
    rows = -(-n_steps // 8) * 8
    sr = jnp.pad(sr.reshape(n_steps, nj, gb * p).transpose(1, 0, 2), ((0, 0), (0, rows - n_steps), (0, 0)))
    si = jnp.pad(si.reshape(n_steps, nj, gb * p).transpose(1, 0, 2), ((0, 0), (0, rows - n_steps), (0, 0)))
    return (kr.astype(BF16), rows_form(abr), rows_form(abi),
            cols_form(car[1:]), cols_form(-cai[1:]), sr, si)


def _ssm_expand(kr_ref, abr_ref, abi_ref, cmr_ref, cmi_ref, m_sc, b_sc, c_sc):
    t, c, gb, p = SSM_CHUNK, SSM_GROUP, GROUPS_PER_BLOCK, SSM_STATE
    shift = c.bit_length() - 1
    half = gb * p
    pairs = LANES // p

    def group_of(idx):
        return (idx >> shift) & (gb - 1)

    m_sc[...] = jnp.zeros_like(m_sc)
    col_g = group_of(lax.broadcasted_iota(jnp.int32, (c, LANES), 1))
    for k in range(t):
        kk = kr_ref[k]
        dk = jnp.concatenate([jnp.where(col_g == g, kk, jnp.zeros_like(kk)) for g in range(gb)], axis=0)
        for tau in range(t - k):
            m_sc[tau * LANES:(tau + 1) * LANES, (tau + k) * LANES:(tau + k + 1) * LANES] = dk

    row_g = group_of(lax.broadcasted_iota(jnp.int32, (t * LANES, LANES), 0))
    lane_half = lax.broadcasted_iota(jnp.int32, (t * LANES, LANES), 1) >> (p.bit_length() - 1)
    for part, ab_ref in enumerate((abr_ref, abi_ref)):
        ab = ab_ref[...]
        for q in range(gb // pairs):
            b_sc[:, part * half + q * LANES:part * half + (q + 1) * LANES] = jnp.where(
                row_g == pairs * q + lane_half, ab, jnp.zeros_like(ab))

    col_g2 = group_of(lax.broadcasted_iota(jnp.int32, (LANES, t * LANES), 1))
    row_half = lax.broadcasted_iota(jnp.int32, (LANES, t * LANES), 0) >> (p.bit_length() - 1)
    for part, cm_ref in enumerate((cmr_ref, cmi_ref)):
        cm = cm_ref[...]
        for q in range(gb // pairs):
            c_sc[part * half + q * LANES:part * half + (q + 1) * LANES, :] = jnp.where(
                col_g2 == pairs * q + row_half, cm, jnp.zeros_like(cm))


def _ssm_kernel(s_ref, kr_ref, abr_ref, abi_ref, cmr_ref, cmi_ref, sr_ref, si_ref, y_ref,
                m_sc, b_sc, c_sc):
    t, n_chunks, _ = s_ref.shape
    half = GROUPS_PER_BLOCK * SSM_STATE

    @pl.when(pl.program_id(1) == 0)
    def _():
        _ssm_expand(kr_ref, abr_ref, abi_ref, cmr_ref, cmi_ref, m_sc, b_sc, c_sc)

    u = jnp.concatenate([s_ref[tau].astype(BF16) for tau in range(t)], axis=1)
    y = _dot(u, m_sc[...])
    x = _dot(u, b_sc[...])
    hr, hi = x[:, :half], x[:, half:]

    row = lax.broadcasted_iota(jnp.int32, (n_chunks, half), 0)

    def shifted(h, k):
        return jnp.where(row >= k, pltpu.roll(h, k, 0), 0.0)

    step = 0
    while (1 << step) < n_chunks:
        k = 1 << step
        pr = sr_ref[step:step + 1, :]
        pi = si_ref[step:step + 1, :]
        zr, zi = shifted(hr, k), shifted(hi, k)
        hr, hi = hr + pr * zr - pi * zi, hi + pr * zi + pi * zr
        step += 1

    h_prev = jnp.concatenate([shifted(hr, 1), shifted(hi, 1)], axis=1).astype(BF16)
    y = jax.nn.gelu(y + _dot(h_prev, c_sc[...]))
    for tau in range(t):
        y_ref[tau] = y[:, tau * LANES:(tau + 1) * LANES]


def _ssm(z, s_col, weights):
    b, n_res, n_chunks, _ = z.shape
    nj = weights[0].shape[0]
    assert n_res == SSM_CHUNK
    assert SSM_GROUP & (SSM_GROUP - 1) == 0 and SSM_STATE & (SSM_STATE - 1) == 0

    def wspec(a):
        nd = a.ndim - 1
        return pl.BlockSpec((None,) + a.shape[1:], lambda j, bi: (j,) + (0,) * nd)

    s = n_res * n_chunks
    width = SSM_CHUNK * LANES
    state = 2 * GROUPS_PER_BLOCK * SSM_STATE
    scratch = (width * width + 2 * width * state) * 2
    wbytes = 2 * sum(math.prod(a.shape[1:]) * a.dtype.itemsize for a in weights)
    vmem = (4 * s * LANES * 4 + wbytes + scratch + n_chunks * (width * 10 + state * 12)) * 1.2
    return pl.pallas_call(
        _ssm_kernel,
        grid=(nj, b),
        in_specs=[pl.BlockSpec((None, n_res, n_chunks, LANES),
                               lambda j, bi: (bi, 0, 0, s_col + j))]
        + [wspec(a) for a in weights],
        out_specs=pl.BlockSpec((None, n_res, n_chunks, LANES), lambda j, bi: (bi, 0, 0, j)),
        out_shape=jax.ShapeDtypeStruct((b, n_res, n_chunks, nj * LANES), F32),
        scratch_shapes=[pltpu.VMEM((width, width), BF16), pltpu.VMEM((width, state), BF16),
                        pltpu.VMEM((state, width), BF16)],
        compiler_params=_params(("parallel", "arbitrary"), vmem),
        name="ssm",
    )(z, *weights)


def _mix_out_kernel(ya_ref, y_ref, h_ref, ga_ref, gb_ref, wglu_ref, bglu_ref,
                    wa_ref, wb_ref, o_ref):
    n_res, tl, _ = ya_ref.shape
    y = y_ref[...].reshape(n_res * tl, -1)
    gate = _dot(y.astype(BF16), wglu_ref[...]) + bglu_ref[...]
    yb = y * jax.nn.sigmoid(gate)
    na = _rms(ya_ref[...].reshape(n_res * tl, -1), ga_ref[...]).astype(BF16)
    nb = _rms(yb, gb_ref[...]).astype(BF16)
    perm = _residue_perm(n_res, tl, True)
    na = _dot(perm, na).astype(BF16)
    nb = _dot(perm, nb).astype(BF16)
    o_ref[...] = h_ref[...] + _dot(na, wa_ref[...]) + _dot(nb, wb_ref[...])


def _mix_out(ya, y, h, ga, gb, wglu, bglu, wa, wb, *, tm=512):
    m, d = h.shape
    batch, n_res, length, wa_w = ya.shape
    wb_w = y.shape[-1]
    tm = min(tm, n_res * length)
    tl = tm // n_res
    tiles = length // tl
    assert tl % 8 == 0 and length % tl == 0
    vmem = (2 * tm * (wa_w + wb_w + 2 * d) * 4 + (wglu.size + wa.size + wb.size) * 2
            + 8 * tm * d * 4) * 1.2
    row = pl.BlockSpec((tm, d), lambda i: (i, 0))
    res_major = lambda w: pl.BlockSpec((None, n_res, tl, w),
                                       lambda i: (i // tiles, 0, i % tiles, 0))
    return pl.pallas_call(
        _mix_out_kernel,
        grid=(m // tm,),
        in_specs=[res_major(wa_w), res_major(wb_w), row,
                  _const_spec((1, wa_w)), _const_spec((1, wb_w)),
                  _const_spec(wglu.shape), _const_spec((1, wb_w)),
                  _const_spec(wa.shape), _const_spec(wb.shape)],
        out_specs=row,
        out_shape=jax.ShapeDtypeStruct((m, d), F32),
        compiler_params=_params(("parallel",), vmem),
        name="mix_out",
    )(ya, y, h, ga.reshape(1, -1), gb.reshape(1, -1), wglu, bglu.reshape(1, -1), wa, wb)


def _ple_kernel(h_ref, p_ref, gn_ref, wg_ref, wp_ref, fn_ref, o_ref, *, final):
    h = h_ref[...]
    gate = jax.nn.sigmoid(_dot(_rms(h, gn_ref[...]).astype(BF16), wg_ref[...]))
    proj = _dot(p_ref[...].astype(BF16), wp_ref[...])
    out = h + gate * proj
    o_ref[...] = _rms(out, fn_ref[...]) if final else out


def _ple(h, p, layer, gn, wg, wp, fn, *, final, tm=512):
    m, d = h.shape
    pd = p.shape[2]
    tm = min(tm, m)
    vmem = (2 * tm * (2 * d + pd) * 4 + (wg.size + wp.size) * 2 + 8 * tm * d * 4) * 1.2
    row = lambda w: pl.BlockSpec((tm, w), lambda i: (i, 0))
    return pl.pallas_call(
        functools.partial(_ple_kernel, final=final),
        grid=(m // tm,),
        in_specs=[row(d), pl.BlockSpec((None, tm, pd), lambda i: (layer, i, 0)),
                  _const_spec((1, d)), _const_spec(wg.shape),
                  _const_spec(wp.shape), _const_spec((1, d))],
        out_specs=row(d),
        out_shape=jax.ShapeDtypeStruct((m, d), F32),
        compiler_params=_params(("parallel",), vmem),
        name="ple",
    )(h, p, gn.reshape(1, d), wg, wp, fn.reshape(1, d))


def kernel(x, p, ffn1_norm, ffn1_w_gate, ffn1_w_up, ffn1_w_down, mix_norm, w_in, attn_out_norm, ssm_lambda_re, ssm_lambda_im, ssm_log_dt, ssm_b_re, ssm_b_im, ssm_c_re, ssm_c_im, ssm_d, ssm_w_glu, ssm_b_glu, ssm_out_norm, w_out, ffn2_norm, ffn2_w_gate, ffn2_w_up, ffn2_w_down, ple_norm, ple_w_gate, ple_w_proj, final_norm):
    b, s, d = x.shape
    depth = p.shape[0]
    attn_w = attn_out_norm.shape[1]
    ssm_w = ssm_out_norm.shape[1]
    n_pairs = attn_w // LANES
    h = x.reshape(b * s, d)
    for i in range(depth):
        h = _ffn(h, ffn1_norm[i], _ff_tiles(ffn1_w_gate, i, 1), _ff_tiles(ffn1_w_up, i, 1),
                 _ff_tiles(ffn1_w_down, i, 0))
        z = _in_proj(h, mix_norm[i], w_in[i].astype(BF16), b)
        ya = _attention(z, n_pairs, q_col=0, k_col=n_pairs, v_col=2 * n_pairs)
        weights = _ssm_weights(ssm_lambda_re[i], ssm_lambda_im[i], ssm_log_dt[i],
                               ssm_b_re[i], ssm_b_im[i], ssm_c_re[i], ssm_c_im[i],
                               ssm_d[i], s // SSM_CHUNK)
        y = _ssm(z, 3 * n_pairs, weights)
        wo = w_out[i].astype(BF16)
        h = _mix_out(ya, y, h, attn_out_norm[i], ssm_out_norm[i], ssm_w_glu[i].astype(BF16),
                     ssm_b_glu[i], wo[:attn_w], wo[attn_w:])
        h = _ffn(h, ffn2_norm[i], _ff_tiles(ffn2_w_gate, i, 1), _ff_tiles(ffn2_w_up, i, 1),
                 _ff_tiles(ffn2_w_down, i, 0))
        h = _ple(h, p.reshape(depth, b * s, -1), i, ple_norm[i], ple_w_gate[i].astype(BF16),
                 ple_w_proj[i].astype(BF16), final_norm, final=i == depth - 1)
    return h.reshape(b, s, d)
```

```python
import functools
import math

import jax
import jax.numpy as jnp
from jax import lax
from jax.experimental import pallas as pl
from jax.experimental.pallas import tpu as pltpu

NORM_EPS = 1e-6
MASK_VALUE = -1e30
HEAD_DIM = 64
ATTN_BLOCK = 128
ATTN_SPAN = 128
DILATIONS = (1, 4, 16)
ATTN_UNROLL = 16
SSM_GROUP = 16
SSM_STATE = 64
SSM_CHUNK = 16
TOKEN_RES = 16
LANES = 128
GROUPS_PER_BLOCK = LANES // SSM_GROUP
FF_TILE = 256
FF_STAGE = 512
VMEM_CAP = 56 * 1024 * 1024

BF16 = jnp.bfloat16
F32 = jnp.float32


def _params(sem, vmem_bytes):
    return pltpu.CompilerParams(
        dimension_semantics=sem, vmem_limit_bytes=min(int(vmem_bytes), VMEM_CAP))


def _rms(x, g):
    ms = jnp.mean(x * x, axis=-1, keepdims=True)
    return x * lax.rsqrt(ms + NORM_EPS) * g


def _dot(a, b):
    return jnp.dot(a, b, preferred_element_type=F32)


def _const_spec(shape):
    return pl.BlockSpec(shape, lambda *_: (0,) * len(shape),
                        pipeline_mode=pl.Buffered(1))


def _ffn_kernel(x_ref, g_ref, wg_ref, wu_ref, wd_ref, o_ref, xn_ref):
    f = pl.program_id(1)

    @pl.when(f == 0)
    def _():
        xn_ref[...] = _rms(x_ref[...], g_ref[...]).astype(BF16)
        o_ref[...] = jnp.zeros_like(o_ref)

    xn = xn_ref[...]
    gate = _dot(xn, wg_ref[...])
    up = _dot(xn, wu_ref[...])
    act = (gate * jax.nn.sigmoid(gate) * up).astype(BF16)
    o_ref[...] += _dot(act, wd_ref[...])

    @pl.when(f == pl.num_programs(1) - 1)
    def _():
        o_ref[...] = x_ref[...] + 0.5 * o_ref[...]


def _ffn(h, g, wg, wu, wd, *, tm=1024):
    m, d = h.shape
    nf, _, tf = wg.shape
    tm = min(tm, m)
    vmem = (4 * tm * d * 4 + tm * d * 2 + 6 * d * tf * 2 + 6 * tm * tf * 4) * 1.25
    row = pl.BlockSpec((tm, d), lambda i, f: (i, 0))
    return pl.pallas_call(
        _ffn_kernel,
        grid=(m // tm, nf),
        in_specs=[
            row,
            pl.BlockSpec((1, d), lambda i, f: (0, 0)),
            pl.BlockSpec((None, d, tf), lambda i, f: (f, 0, 0)),
            pl.BlockSpec((None, d, tf), lambda i, f: (f, 0, 0)),
            pl.BlockSpec((None, tf, d), lambda i, f: (f, 0, 0)),
        ],
        out_specs=row,
        out_shape=jax.ShapeDtypeStruct((m, d), F32),
        scratch_shapes=[pltpu.VMEM((tm, d), BF16)],
        compiler_params=_params(("parallel", "arbitrary"), vmem),
        name="ffn",
    )(h, g.reshape(1, d), wg, wu, wd)


def _ff_tile_kernel(w_ref, o_ref, *, axis, size):
    w = w_ref[...]
    n_tiles = o_ref.shape[0]
    tile = o_ref.shape[1 + axis]
    pos = lax.broadcasted_iota(jnp.int32, w.shape, axis) + pl.program_id(0) * (n_tiles * tile)
    w = jnp.where(pos < size, w, 0.0).astype(o_ref.dtype)
    for t in range(n_tiles):
        o_ref[t] = w[:, t * tile:(t + 1) * tile] if axis == 1 else w[t * tile:(t + 1) * tile]


def _ff_tiles(w, layer, axis, tf=FF_TILE):
    size = w.shape[1 + axis]
    per_step = max(1, FF_STAGE // tf)
    nf = pl.cdiv(size, tf * per_step) * per_step
    tile = list(w.shape[1:])
    tile[axis] = tf
    stage = list(tile)
    stage[axis] = tf * per_step
    index = (lambda f: (layer, 0, f)) if axis == 1 else (lambda f: (layer, f, 0))
    return pl.pallas_call(
        functools.partial(_ff_tile_kernel, axis=axis, size=size),
        grid=(nf // per_step,),
        in_specs=[pl.BlockSpec((None,) + tuple(stage), index)],
        out_specs=pl.BlockSpec((per_step,) + tuple(tile), lambda f: (f, 0, 0)),
        out_shape=jax.ShapeDtypeStruct((nf,) + tuple(tile), BF16),
        compiler_params=_params(("parallel",), 6 * math.prod(stage) * 4),
        name="ff_tiles",
    )(w)


def _residue_perm(n_res, tl, transpose):
    tm = n_res * tl
    assert tl & (tl - 1) == 0
    a = lax.broadcasted_iota(jnp.int32, (tm, tm), 1 if transpose else 0)
    b = lax.broadcasted_iota(jnp.int32, (tm, tm), 0 if transpose else 1)
    hit = b == n_res * (a & (tl - 1)) + (a >> (tl.bit_length() - 1))
    return jnp.where(hit, 1.0, 0.0).astype(BF16)


def _in_proj_kernel(x_ref, g_ref, w_ref, o_ref, tok_sc):
    n_res, tl, _ = o_ref.shape
    xn = _rms(x_ref[...], g_ref[...])
    blocks = xn.shape[1] // LANES
    for c in range(blocks):
        tok_sc[c] = xn[:, c * LANES:(c + 1) * LANES]
    xn = jnp.concatenate(
        [jnp.concatenate([tok_sc[c, pl.ds(r, tl, stride=n_res), :] for c in range(blocks)], axis=1)
         for r in range(n_res)], axis=0).astype(BF16)
    res = _dot(xn, w_ref[...])
    for r in range(n_res):
        o_ref[r] = res[r * tl:(r + 1) * tl]


def _in_proj(h, g, w, batch, *, tm=256):
    m, d = h.shape
    n = w.shape[1]
    seq = m // batch
    tm = min(tm, seq)
    tl = tm // TOKEN_RES
    tiles = seq // tm
    assert tl % 8 == 0 and seq % tm == 0
    vmem = (2 * tm * d * 4 + tm * d * 6 + d * n * 2 + 4 * tm * n * 4 + 4 * tm * tm) * 1.2
    return pl.pallas_call(
        _in_proj_kernel,
        grid=(m // tm,),
        in_specs=[
            pl.BlockSpec((tm, d), lambda i: (i, 0)),
            _const_spec((1, d)),
            _const_spec((d, n)),
        ],
        out_specs=pl.BlockSpec((None, TOKEN_RES, tl, n),
                               lambda i: (i // tiles, 0, i % tiles, 0)),
        out_shape=jax.ShapeDtypeStruct((batch, TOKEN_RES, seq // TOKEN_RES, n), F32),
        scratch_shapes=[pltpu.VMEM((d // LANES, tm, LANES), F32)],
        compiler_params=_params(("parallel",), vmem),
        name="in_proj",
    )(h, g.reshape(1, d), w)


def _attn_kernel(q_ref, k_ref, v_ref, o_ref, m_ref, l_ref, bias_ref, p_sc, mx_sc):
    n_res, length, _ = q_ref.shape
    blk = ATTN_BLOCK
    n_units = n_res * length // blk
    head0 = lax.broadcasted_iota(jnp.int32, (blk, LANES), 1) < HEAD_DIM
    kv_head0 = lax.broadcasted_iota(jnp.int32, (2 * blk, LANES), 1) < HEAD_DIM
    scale = HEAD_DIM ** -0.5 * math.log2(math.e)

    rho = lax.broadcasted_iota(jnp.int32, (blk, 2 * blk), 0)
    kap = lax.broadcasted_iota(jnp.int32, (blk, 2 * blk), 1)
    for pattern, d in enumerate(DILATIONS):
        n_sub = n_res // d
        c_rows = blk // n_sub
        sh = c_rows.bit_length() - 1
        iq, lq = rho >> sh, rho & (c_rows - 1)
        ik, lk = kap >> (sh + 1), kap & (2 * c_rows - 1)
        for shift in range(2):
            dist = n_sub * (lq - lk + shift * c_rows) + (iq - ik)
            bias_ref[2 * pattern + shift] = jnp.where(
                (dist >= 0) & (dist <= ATTN_SPAN), 0.0, MASK_VALUE)

    def probs(q, k, bias, sel):
        qh = jnp.where(sel, q, 0.0).astype(BF16)
        s = lax.dot_general(qh, k, (((1,), (1,)), ((), ())),
                            preferred_element_type=F32) + bias
        m = jnp.max(s, axis=-1, keepdims=True)
        return jnp.exp2(s - m).astype(BF16), m

    n_groups = n_units // ATTN_UNROLL
    for pattern, d in enumerate(DILATIONS):
        first = pattern == 0
        last = pattern == len(DILATIONS) - 1
        n_sub = n_res // d
        c_rows = blk // n_sub
        blocks_per_residue = n_units // d

        def gather(ref, res, start, size, d=d, n_sub=n_sub):
            return jnp.concatenate(
                [ref[d * i + res, pl.ds(start, size), :] for i in range(n_sub)], axis=0)

        def scatter(ref, res, start, val, d=d, n_sub=n_sub, c_rows=c_rows):
            for i in range(n_sub):
                ref[d * i + res, pl.ds(start, c_rows), :] = val[i * c_rows:(i + 1) * c_rows]

        def place(g, uu, c_rows=c_rows, blocks_per_residue=blocks_per_residue):
            u = g * ATTN_UNROLL + uu
            res = u // blocks_per_residue
            bq = u % blocks_per_residue
            kb = jnp.maximum(bq - 1, 0)
            return (res, pl.multiple_of(c_rows * bq, 8), pl.multiple_of(c_rows * kb, 8), bq - kb)

        def score_stage(g, slot, pattern=pattern, c_rows=c_rows, gather=gather, place=place):
            loaded = []
            for uu in range(ATTN_UNROLL):
                res, q0, k0, shift = place(g, uu)
                loaded.append((gather(q_ref, res, q0, c_rows) * scale,
                               gather(k_ref, res, k0, 2 * c_rows).astype(BF16),
                               bias_ref[2 * pattern + shift]))
            for uu, (q, k, bias) in enumerate(loaded):
                p0, m0 = probs(q, k, bias, head0)
                p1, m1 = probs(q, k, bias, jnp.logical_not(head0))
                p_sc[slot, uu, 0] = p0
                p_sc[slot, uu, 1] = p1
                mx_sc[slot, uu] = jnp.where(head0, m0, m1)

        def value_stage(g, slot, first=first, last=last, c_rows=c_rows,
                        gather=gather, scatter=scatter, place=place):
            loaded = []
            for uu in range(ATTN_UNROLL):
                res, q0, k0, _ = place(g, uu)
                old = None if first else tuple(
                    gather(ref, res, q0, c_rows) for ref in (o_ref, m_ref, l_ref))
                loaded.append((res, q0, gather(v_ref, res, k0, 2 * c_rows).astype(BF16),
                               p_sc[slot, uu, 0], p_sc[slot, uu, 1], mx_sc[slot, uu], old))
            results = []
            for res, q0, v, p0, p1, m_new, old in loaded:
                ones = jnp.ones_like(v)
                o0 = _dot(p0, jnp.where(kv_head0, v, ones))
                o1 = _dot(p1, jnp.where(kv_head0, ones, v))
                o_new = jnp.where(head0, o0, o1)
                l_new = pltpu.roll(jnp.where(head0, o1, o0), HEAD_DIM, 1)
                if not first:
                    o_old, m_old, l_old = old
                    m_tot = jnp.maximum(m_old, m_new)
                    w_old = jnp.exp2(m_old - m_tot)
                    w_new = jnp.exp2(m_new - m_tot)
                    o_new = w_old * o_old + w_new * o_new
                    l_new = w_old * l_old + w_new * l_new
                    m_new = m_tot
                results.append((res, q0, o_new, m_new, l_new))
            for res, q0, o_new, m_new, l_new in results:
                if last:
                    scatter(o_ref, res, q0, o_new / l_new)
                else:
                    scatter(o_ref, res, q0, o_new)
                    scatter(m_ref, res, q0, m_new)
                    scatter(l_ref, res, q0, l_new)

        def step(g, carry, score_stage=score_stage, value_stage=value_stage):
            slot = g & 1
            value_stage(g - 1, 1 - slot)
            score_stage(g, slot)
            return carry

        score_stage(0, 0)
        lax.fori_loop(1, n_groups, step, 0)
        value_stage(n_groups - 1, (n_groups - 1) & 1)


def _attention(z, n_pairs, *, q_col, k_col, v_col):
    b, n_res, length, _ = z.shape
    assert n_res == TOKEN_RES == DILATIONS[-1] and all(n_res % d == 0 for d in DILATIONS)
    assert length % ATTN_BLOCK == 0 and length >= 2 * ATTN_BLOCK and ATTN_BLOCK // n_res >= 8
    assert (n_res * length // ATTN_BLOCK) % ATTN_UNROLL == 0

    def spec(col):
        return pl.BlockSpec((None, n_res, length, LANES), lambda bi, h: (bi, 0, 0, col + h))

    slab = n_res * length * LANES * 4
    vmem = 10 * slab * 1.15 + (8 << 20)
    return pl.pallas_call(
        _attn_kernel,
        grid=(b, n_pairs),
        in_specs=[spec(q_col), spec(k_col), spec(v_col)],
        out_specs=spec(0),
        out_shape=jax.ShapeDtypeStruct((b, n_res, length, n_pairs * LANES), F32),
        scratch_shapes=[pltpu.VMEM((n_res, length, LANES), F32),
                        pltpu.VMEM((n_res, length, LANES), F32),
                        pltpu.VMEM((2 * len(DILATIONS), ATTN_BLOCK, 2 * ATTN_BLOCK), F32),
                        pltpu.VMEM((2, ATTN_UNROLL, 2, ATTN_BLOCK, 2 * ATTN_BLOCK), BF16),
                        pltpu.VMEM((2, ATTN_UNROLL, ATTN_BLOCK, LANES), F32)],
        compiler_params=_params(("parallel", "parallel"), vmem),
        name="attention",
    )(z, z, z)


def _ssm_weights(lam_re, lam_im, log_dt, b_re, b_im, c_re, c_im, d_skip, n_chunks):
    hp = lax.Precision.HIGHEST
    g, p = lam_re.shape
    c = b_re.shape[-1]
    t = SSM_CHUNK
    gb = GROUPS_PER_BLOCK
    nj = g // gb
    lr, li = lam_re.astype(F32), lam_im.astype(F32)
    dt = jnp.exp(log_dt.astype(F32))[:, None]
    mag = jnp.exp(lr * dt)
    ar = mag * jnp.cos(li * dt)
    ai = mag * jnp.sin(li * dt)
    nr, ni = ar - 1.0, ai
    den = lr * lr + li * li
    cr = (nr * lr + ni * li) / den
    ci = (ni * lr - nr * li) / den
    br, bi = b_re.astype(F32), b_im.astype(F32)
    bbr = cr[..., None] * br - ci[..., None] * bi
    bbi = cr[..., None] * bi + ci[..., None] * br

    def power(k):
        kk = k.astype(F32)[:, None, None]
        pm = jnp.exp(kk * (lr * dt))
        ang = kk * (li * dt)
        return pm * jnp.cos(ang), pm * jnp.sin(ang)

    pr, pi = power(jnp.arange(t + 1))
    cre, cim = c_re.astype(F32), c_im.astype(F32)
    car = cre[None] * pr[:, :, None, :] - cim[None] * pi[:, :, None, :]
    cai = cre[None] * pi[:, :, None, :] + cim[None] * pr[:, :, None, :]
    kern = jnp.einsum('kgdq,gqc->kgdc', jnp.concatenate([car[:t], -cai[:t]], axis=-1),
                      jnp.concatenate([bbr, bbi], axis=1), precision=hp)
    kern = kern.at[0].add(d_skip.astype(F32).reshape(g, c)[:, :, None] * jnp.eye(c, dtype=F32))

    kr = kern.reshape(t, nj, gb, c, c).transpose(1, 0, 4, 2, 3).reshape(nj, t, c, gb * c)

    rev_r, rev_i = pr[:t][::-1][:, :, None, :], pi[:t][::-1][:, :, None, :]
    bbr_t, bbi_t = bbr.transpose(0, 2, 1)[None], bbi.transpose(0, 2, 1)[None]
    abr = rev_r * bbr_t - rev_i * bbi_t
    abi = rev_r * bbi_t + rev_i * bbr_t

    def block_rows(x):
        return x.reshape(t, nj, gb, c, p).transpose(1, 0, 2, 3, 4).reshape(nj, t * gb * c, p)

    def rows_form(x):
        x = block_rows(x)
        return jnp.concatenate([x, x], axis=-1).astype(BF16)

    def cols_form(x):
        x = block_rows(x).transpose(0, 2, 1)
        return jnp.concatenate([x, x], axis=1).astype(BF16)

    n_steps = max(1, (n_chunks - 1).bit_length())
    sr, si = power(t * (2 ** jnp.arange(n_steps)))
    rows = -(-n_steps // 8) * 8
    sr = jnp.pad(sr.reshape(n_steps, nj, gb * p).transpose(1, 0, 2), ((0, 0), (0, rows - n_steps), (0, 0)))
    si = jnp.pad(si.reshape(n_steps, nj, gb * p).transpose(1, 0, 2), ((0, 0), (0, rows - n_steps), (0, 0)))
    return (kr.astype(BF16), rows_form(abr), rows_form(abi),
            cols_form(car[1:]), cols_form(-cai[1:]), sr, si)


def _ssm_expand(kr_ref, abr_ref, abi_ref, cmr_ref, cmi_ref, m_sc, b_sc, c_sc):
    t, c, gb, p = SSM_CHUNK, SSM_GROUP, GROUPS_PER_BLOCK, SSM_STATE
    shift = c.bit_length() - 1
    half = gb * p
    pairs = LANES // p

    def group_of(idx):
        return (idx >> shift) & (gb - 1)

    m_sc[...] = jnp.zeros_like(m_sc)
    col_g = group_of(lax.broadcasted_iota(jnp.int32, (c, LANES), 1))
    for k in range(t):
        kk = kr_ref[k]
        dk = jnp.concatenate([jnp.where(col_g == g, kk, jnp.zeros_like(kk)) for g in range(gb)], axis=0)
        for tau in range(t - k):
            m_sc[tau * LANES:(tau + 1) * LANES, (tau + k) * LANES:(tau + k + 1) * LANES] = dk

    row_g = group_of(lax.broadcasted_iota(jnp.int32, (t * LANES, LANES), 0))
    lane_half = lax.broadcasted_iota(jnp.int32, (t * LANES, LANES), 1) >> (p.bit_length() - 1)
    for part, ab_ref in enumerate((abr_ref, abi_ref)):
        ab = ab_ref[...]
        for q in range(gb // pairs):
            b_sc[:, part * half + q * LANES:part * half + (q + 1) * LANES] = jnp.where(
                row_g == pairs * q + lane_half, ab, jnp.zeros_like(ab))

    col_g2 = group_of(lax.broadcasted_iota(jnp.int32, (LANES, t * LANES), 1))
    row_half = lax.broadcasted_iota(jnp.int32, (LANES, t * LANES), 0) >> (p.bit_length() - 1)
    for part, cm_ref in enumerate((cmr_ref, cmi_ref)):
        cm = cm_ref[...]
        for q in range(gb // pairs):
            c_sc[part * half + q * LANES:part * half + (q + 1) * LANES, :] = jnp.where(
                col_g2 == pairs * q + row_half, cm, jnp.zeros_like(cm))


def _ssm_kernel(s_ref, kr_ref, abr_ref, abi_ref, cmr_ref, cmi_ref, sr_ref, si_ref, y_ref,
                m_sc, b_sc, c_sc):
    t, n_chunks, _ = s_ref.shape
    half = GROUPS_PER_BLOCK * SSM_STATE

    @pl.when(pl.program_id(1) == 0)
    def _():
        _ssm_expand(kr_ref, abr_ref, abi_ref, cmr_ref, cmi_ref, m_sc, b_sc, c_sc)

    u = jnp.concatenate([s_ref[tau].astype(BF16) for tau in range(t)], axis=1)
    y = _dot(u, m_sc[...])
    x = _dot(u, b_sc[...])
    hr, hi = x[:, :half], x[:, half:]

    row = lax.broadcasted_iota(jnp.int32, (n_chunks, half), 0)

    def shifted(h, k):
        return jnp.where(row >= k, pltpu.roll(h, k, 0), 0.0)

    step = 0
    while (1 << step) < n_chunks:
        k = 1 << step
        pr = sr_ref[step:step + 1, :]
        pi = si_ref[step:step + 1, :]
        zr, zi = shifted(hr, k), shifted(hi, k)
        hr, hi = hr + pr * zr - pi * zi, hi + pr * zi + pi * zr
        step += 1

    h_prev = jnp.concatenate([shifted(hr, 1), shifted(hi, 1)], axis=1).astype(BF16)
    y = jax.nn.gelu(y + _dot(h_prev, c_sc[...]))
    for tau in range(t):
        y_ref[tau] = y[:, tau * LANES:(tau + 1) * LANES]


def _ssm(z, s_col, weights):
    b, n_res, n_chunks, _ = z.shape
    nj = weights[0].shape[0]
    assert n_res == SSM_CHUNK
    assert SSM_GROUP & (SSM_GROUP - 1) == 0 and SSM_STATE & (SSM_STATE - 1) == 0

    def wspec(a):
        nd = a.ndim - 1
        return pl.BlockSpec((None,) + a.shape[1:], lambda j, bi: (j,) + (0,) * nd)

    s = n_res * n_chunks
    width = SSM_CHUNK * LANES
    state = 2 * GROUPS_PER_BLOCK * SSM_STATE
    scratch = (width * width + 2 * width * state) * 2
    wbytes = 2 * sum(math.prod(a.shape[1:]) * a.dtype.itemsize for a in weights)
    vmem = (4 * s * LANES * 4 + wbytes + scratch + n_chunks * (width * 10 + state * 12)) * 1.2
    return pl.pallas_call(
        _ssm_kernel,
        grid=(nj, b),
        in_specs=[pl.BlockSpec((None, n_res, n_chunks, LANES),
                               lambda j, bi: (bi, 0, 0, s_col + j))]
        + [wspec(a) for a in weights],
        out_specs=pl.BlockSpec((None, n_res, n_chunks, LANES), lambda j, bi: (bi, 0, 0, j)),
        out_shape=jax.ShapeDtypeStruct((b, n_res, n_chunks, nj * LANES), F32),
        scratch_shapes=[pltpu.VMEM((width, width), BF16), pltpu.VMEM((width, state), BF16),
                        pltpu.VMEM((state, width), BF16)],
        compiler_params=_params(("parallel", "arbitrary"), vmem),
        name="ssm",
    )(z, *weights)


def _mix_out_kernel(ya_ref, y_ref, h_ref, ga_ref, gb_ref, wglu_ref, bglu_ref,
                    wa_ref, wb_ref, o_ref, tok_sc):
    n_res, tl, _ = ya_ref.shape
    y = y_ref[...].reshape(n_res * tl, -1)
    gate = _dot(y.astype(BF16), wglu_ref[...]) + bglu_ref[...]
    yb = y * jax.nn.sigmoid(gate)
    na = _rms(ya_ref[...].reshape(n_res * tl, -1), ga_ref[...])
    nb = _rms(yb, gb_ref[...])

    def token_order(x, base):
        blocks = x.shape[1] // LANES
        for c in range(blocks):
            for r in range(n_res):
                tok_sc[base + c, pl.ds(r, tl, stride=n_res), :] = (
                    x[r * tl:(r + 1) * tl, c * LANES:(c + 1) * LANES])
        return jnp.concatenate([tok_sc[base + c] for c in range(blocks)], axis=1).astype(BF16)

    na = token_order(na, 0)
    nb = token_order(nb, na.shape[1] // LANES)
    o_ref[...] = h_ref[...] + _dot(na, wa_ref[...]) + _dot(nb, wb_ref[...])


def _mix_out(ya, y, h, ga, gb, wglu, bglu, wa, wb, *, tm=512):
    m, d = h.shape
    batch, n_res, length, wa_w = ya.shape
    wb_w = y.shape[-1]
    tm = min(tm, n_res * length)
    tl = tm // n_res
    tiles = length // tl
    assert tl % 8 == 0 and length % tl == 0
    vmem = (2 * tm * (wa_w + wb_w + 2 * d) * 4 + (wglu.size + wa.size + wb.size) * 2
            + 8 * tm * d * 4) * 1.2
    row = pl.BlockSpec((tm, d), lambda i: (i, 0))
    res_major = lambda w: pl.BlockSpec((None, n_res, tl, w),
                                       lambda i: (i // tiles, 0, i % tiles, 0))
    return pl.pallas_call(
        _mix_out_kernel,
        grid=(m // tm,),
        in_specs=[res_major(wa_w), res_major(wb_w), row,
                  _const_spec((1, wa_w)), _const_spec((1, wb_w)),
                  _const_spec(wglu.shape), _const_spec((1, wb_w)),
                  _const_spec(wa.shape), _const_spec(wb.shape)],
        out_specs=row,
        out_shape=jax.ShapeDtypeStruct((m, d), F32),
        scratch_shapes=[pltpu.VMEM(((wa_w + wb_w) // LANES, tm, LANES), F32)],
        compiler_params=_params(("parallel",), vmem),
        name="mix_out",
    )(ya, y, h, ga.reshape(1, -1), gb.reshape(1, -1), wglu, bglu.reshape(1, -1), wa, wb)


def _ple_kernel(h_ref, p_ref, gn_ref, wg_ref, wp_ref, fn_ref, o_ref, *, final):
    h = h_ref[...]
    gate = jax.nn.sigmoid(_dot(_rms(h, gn_ref[...]).astype(BF16), wg_ref[...]))
    proj = _dot(p_ref[...].astype(BF16), wp_ref[...])
    out = h + gate * proj
    o_ref[...] = _rms(out, fn_ref[...]) if final else out


def _ple(h, p, layer, gn, wg, wp, fn, *, final, tm=512):
    m, d = h.shape
    pd = p.shape[2]
    tm = min(tm, m)
    vmem = (2 * tm * (2 * d + pd) * 4 + (wg.size + wp.size) * 2 + 8 * tm * d * 4) * 1.2
    row = lambda w: pl.BlockSpec((tm, w), lambda i: (i, 0))
    return pl.pallas_call(
        functools.partial(_ple_kernel, final=final),
        grid=(m // tm,),
        in_specs=[row(d), pl.BlockSpec((None, tm, pd), lambda i: (layer, i, 0)),
                  _const_spec((1, d)), _const_spec(wg.shape),
                  _const_spec(wp.shape), _const_spec((1, d))],
        out_specs=row(d),
        out_shape=jax.ShapeDtypeStruct((m, d), F32),
        compiler_params=_params(("parallel",), vmem),
        name="ple",
    )(h, p, gn.reshape(1, d), wg, wp, fn.reshape(1, d))


def kernel(x, p, ffn1_norm, ffn1_w_gate, ffn1_w_up, ffn1_w_down, mix_norm, w_in, attn_out_norm, ssm_lambda_re, ssm_lambda_im, ssm_log_dt, ssm_b_re, ssm_b_im, ssm_c_re, ssm_c_im, ssm_d, ssm_w_glu, ssm_b_glu, ssm_out_norm, w_out, ffn2_norm, ffn2_w_gate, ffn2_w_up, ffn2_w_down, ple_norm, ple_w_gate, ple_w_proj, final_norm):
    b, s, d = x.shape
    depth = p.shape[0]
    attn_w = attn_out_norm.shape[1]
    ssm_w = ssm_out_norm.shape[1]
    n_pairs = attn_w // LANES
    h = x.reshape(b * s, d)
    for i in range(depth):
        h = _ffn(h, ffn1_norm[i], _ff_tiles(ffn1_w_gate, i, 1), _ff_tiles(ffn1_w_up, i, 1),
                 _ff_tiles(ffn1_w_down, i, 0))
        z = _in_proj(h, mix_norm[i], w_in[i].astype(BF16), b)
        ya = _attention(z, n_pairs, q_col=0, k_col=n_pairs, v_col=2 * n_pairs)
        weights = _ssm_weights(ssm_lambda_re[i], ssm_lambda_im[i], ssm_log_dt[i],
                               ssm_b_re[i], ssm_b_im[i], ssm_c_re[i], ssm_c_im[i],
                               ssm_d[i], s // SSM_CHUNK)
        y = _ssm(z, 3 * n_pairs, weights)
        wo = w_out[i].astype(BF16)
        h = _mix_out(ya, y, h, attn_out_norm[i], ssm_out_norm[i], ssm_w_glu[i].astype(BF16),
                     ssm_b_glu[i], wo[:attn_w], wo[attn_w:])
        h = _ffn(h, ffn2_norm[i], _ff_tiles(ffn2_w_gate, i, 1), _ff_tiles(ffn2_w_up, i, 1),
                 _ff_tiles(ffn2_w_down, i, 0))
        h = _ple(h, p.reshape(depth, b * s, -1), i, ple_norm[i], ple_w_gate[i].astype(BF16),
                 ple_w_proj[i].astype(BF16), final_norm, final=i == depth - 1)
    return h.reshape(b, s, d)
```

```python
import functools
import math

import jax
import jax.numpy as jnp
from jax import lax
from jax.experimental import pallas as pl
from jax.experimental.pallas import tpu as pltpu

NORM_EPS = 1e-6
MASK_VALUE = -1e30
HEAD_DIM = 64
ATTN_BLOCK = 128
ATTN_SPAN = 128
DILATIONS = (1, 4, 16)
ATTN_UNROLL = 16
SSM_GROUP = 16
SSM_STATE = 64
SSM_CHUNK = 16
TOKEN_RES = 16
LANES = 128
GROUPS_PER_BLOCK = LANES // SSM_GROUP
FF_TILE = 256
FF_STAGE = 512
MIX_SPLIT = 2
VMEM_CAP = 56 * 1024 * 1024

BF16 = jnp.bfloat16
F32 = jnp.float32


def _params(sem, vmem_bytes):
    return pltpu.CompilerParams(
        dimension_semantics=sem, vmem_limit_bytes=min(int(vmem_bytes), VMEM_CAP))


def _rms(x, g):
    ms = jnp.mean(x * x, axis=-1, keepdims=True)
    return x * lax.rsqrt(ms + NORM_EPS) * g


def _dot(a, b):
    return jnp.dot(a, b, preferred_element_type=F32)


def _const_spec(shape):
    return pl.BlockSpec(shape, lambda *_: (0,) * len(shape),
                        pipeline_mode=pl.Buffered(1))


def _ffn_kernel(x_ref, g_ref, wg_ref, wu_ref, wd_ref, o_ref, xn_ref):
    f = pl.program_id(1)

    @pl.when(f == 0)
    def _():
        xn_ref[...] = _rms(x_ref[...], g_ref[...]).astype(BF16)
        o_ref[...] = jnp.zeros_like(o_ref)

    xn = xn_ref[...]
    gate = _dot(xn, wg_ref[...])
    up = _dot(xn, wu_ref[...])
    act = (gate * jax.nn.sigmoid(gate) * up).astype(BF16)
    o_ref[...] += _dot(act, wd_ref[...])

    @pl.when(f == pl.num_programs(1) - 1)
    def _():
        o_ref[...] = x_ref[...] + 0.5 * o_ref[...]


def _ffn(h, g, wg, wu, wd, *, tm=1024):
    m, d = h.shape
    nf, _, tf = wg.shape
    tm = min(tm, m)
    vmem = (4 * tm * d * 4 + tm * d * 2 + 6 * d * tf * 2 + 6 * tm * tf * 4) * 1.25
    row = pl.BlockSpec((tm, d), lambda i, f: (i, 0))
    return pl.pallas_call(
        _ffn_kernel,
        grid=(m // tm, nf),
        in_specs=[
            row,
            pl.BlockSpec((1, d), lambda i, f: (0, 0)),
            pl.BlockSpec((None, d, tf), lambda i, f: (f, 0, 0)),
            pl.BlockSpec((None, d, tf), lambda i, f: (f, 0, 0)),
            pl.BlockSpec((None, tf, d), lambda i, f: (f, 0, 0)),
        ],
        out_specs=row,
        out_shape=jax.ShapeDtypeStruct((m, d), F32),
        scratch_shapes=[pltpu.VMEM((tm, d), BF16)],
        compiler_params=_params(("parallel", "arbitrary"), vmem),
        name="ffn",
    )(h, g.reshape(1, d), wg, wu, wd)


def _ff_tile_kernel(w_ref, o_ref, *, axis, size):
    w = w_ref[...]
    n_tiles = o_ref.shape[0]
    tile = o_ref.shape[1 + axis]
    pos = lax.broadcasted_iota(jnp.int32, w.shape, axis) + pl.program_id(0) * (n_tiles * tile)
    w = jnp.where(pos < size, w, 0.0).astype(o_ref.dtype)
    for t in range(n_tiles):
        o_ref[t] = w[:, t * tile:(t + 1) * tile] if axis == 1 else w[t * tile:(t + 1) * tile]


def _ff_tiles(w, layer, axis, tf=FF_TILE):
    size = w.shape[1 + axis]
    per_step = max(1, FF_STAGE // tf)
    nf = pl.cdiv(size, tf * per_step) * per_step
    tile = list(w.shape[1:])
    tile[axis] = tf
    stage = list(tile)
    stage[axis] = tf * per_step
    index = (lambda f: (layer, 0, f)) if axis == 1 else (lambda f: (layer, f, 0))
    return pl.pallas_call(
        functools.partial(_ff_tile_kernel, axis=axis, size=size),
        grid=(nf // per_step,),
        in_specs=[pl.BlockSpec((None,) + tuple(stage), index)],
        out_specs=pl.BlockSpec((per_step,) + tuple(tile), lambda f: (f, 0, 0)),
        out_shape=jax.ShapeDtypeStruct((nf,) + tuple(tile), BF16),
        compiler_params=_params(("parallel",), 6 * math.prod(stage) * 4),
        name="ff_tiles",
    )(w)


def _residue_perm(n_res, tl, transpose):
    tm = n_res * tl
    assert tl & (tl - 1) == 0
    a = lax.broadcasted_iota(jnp.int32, (tm, tm), 1 if transpose else 0)
    b = lax.broadcasted_iota(jnp.int32, (tm, tm), 0 if transpose else 1)
    hit = b == n_res * (a & (tl - 1)) + (a >> (tl.bit_length() - 1))
    return jnp.where(hit, 1.0, 0.0).astype(BF16)


def _in_proj_kernel(x_ref, g_ref, w_ref, o_ref):
    n_res, tl, _ = o_ref.shape
    xn = _rms(x_ref[...], g_ref[...]).astype(BF16)
    xn = _dot(_residue_perm(n_res, tl, False), xn).astype(BF16)
    res = _dot(xn, w_ref[...])
    for r in range(n_res):
        o_ref[r] = res[r * tl:(r + 1) * tl]


def _in_proj(h, g, w, batch, *, tm=256):
    m, d = h.shape
    n = w.shape[1]
    seq = m // batch
    tm = min(tm, seq)
    tl = tm // TOKEN_RES
    tiles = seq // tm
    assert tl % 8 == 0 and seq % tm == 0
    vmem = (2 * tm * d * 4 + tm * d * 6 + d * n * 2 + 4 * tm * n * 4 + 4 * tm * tm) * 1.2
    return pl.pallas_call(
        _in_proj_kernel,
        grid=(m // tm,),
        in_specs=[
            pl.BlockSpec((tm, d), lambda i: (i, 0)),
            _const_spec((1, d)),
            _const_spec((d, n)),
        ],
        out_specs=pl.BlockSpec((None, TOKEN_RES, tl, n),
                               lambda i: (i // tiles, 0, i % tiles, 0)),
        out_shape=jax.ShapeDtypeStruct((batch, TOKEN_RES, seq // TOKEN_RES, n), F32),
        compiler_params=_params(("parallel",), vmem),
        name="in_proj",
    )(h, g.reshape(1, d), w)


def _attn_kernel(q_ref, k_ref, v_ref, o_ref, m_ref, l_ref, bias_ref, p_sc, mx_sc):
    n_res, length, _ = q_ref.shape
    blk = ATTN_BLOCK
    n_units = n_res * length // blk
    head0 = lax.broadcasted_iota(jnp.int32, (blk, LANES), 1) < HEAD_DIM
    kv_head0 = lax.broadcasted_iota(jnp.int32, (2 * blk, LANES), 1) < HEAD_DIM
    scale = HEAD_DIM ** -0.5 * math.log2(math.e)

    rho = lax.broadcasted_iota(jnp.int32, (blk, 2 * blk), 0)
    kap = lax.broadcasted_iota(jnp.int32, (blk, 2 * blk), 1)
    for pattern, d in enumerate(DILATIONS):
        n_sub = n_res // d
        c_rows = blk // n_sub
        sh = c_rows.bit_length() - 1
        iq, lq = rho >> sh, rho & (c_rows - 1)
        ik, lk = kap >> (sh + 1), kap & (2 * c_rows - 1)
        for shift in range(2):
            dist = n_sub * (lq - lk + shift * c_rows) + (iq - ik)
            bias_ref[2 * pattern + shift] = jnp.where(
                (dist >= 0) & (dist <= ATTN_SPAN), 0.0, MASK_VALUE)

    def probs(q, k, bias, sel):
        qh = jnp.where(sel, q, 0.0).astype(BF16)
        s = lax.dot_general(qh, k, (((1,), (1,)), ((), ())),
                            preferred_element_type=F32) + bias
        m = jnp.max(s, axis=-1, keepdims=True)
        return jnp.exp2(s - m).astype(BF16), m

    n_groups = n_units // ATTN_UNROLL
    for pattern, d in enumerate(DILATIONS):
        first = pattern == 0
        last = pattern == len(DILATIONS) - 1
        n_sub = n_res // d
        c_rows = blk // n_sub
        blocks_per_residue = n_units // d

        def gather(ref, res, start, size, d=d, n_sub=n_sub):
            return jnp.concatenate(
                [ref[d * i + res, pl.ds(start, size), :] for i in range(n_sub)], axis=0)

        def scatter(ref, res, start, val, d=d, n_sub=n_sub, c_rows=c_rows):
            for i in range(n_sub):
                ref[d * i + res, pl.ds(start, c_rows), :] = val[i * c_rows:(i + 1) * c_rows]

        def place(g, uu, c_rows=c_rows, blocks_per_residue=blocks_per_residue):
            u = g * ATTN_UNROLL + uu
            res = u // blocks_per_residue
            bq = u % blocks_per_residue
            kb = jnp.maximum(bq - 1, 0)
            return (res, pl.multiple_of(c_rows * bq, 8), pl.multiple_of(c_rows * kb, 8), bq - kb)

        def score_stage(g, slot, pattern=pattern, c_rows=c_rows, gather=gather, place=place):
            loaded = []
            for uu in range(ATTN_UNROLL):
                res, q0, k0, shift = place(g, uu)
                loaded.append((gather(q_ref, res, q0, c_rows) * scale,
                               gather(k_ref, res, k0, 2 * c_rows).astype(BF16),
                               bias_ref[2 * pattern + shift]))
            for uu, (q, k, bias) in enumerate(loaded):
                p0, m0 = probs(q, k, bias, head0)
                p1, m1 = probs(q, k, bias, jnp.logical_not(head0))
                p_sc[slot, uu, 0] = p0
                p_sc[slot, uu, 1] = p1
                mx_sc[slot, uu] = jnp.where(head0, m0, m1)

        def value_stage(g, slot, first=first, last=last, c_rows=c_rows,
                        gather=gather, scatter=scatter, place=place):
            loaded = []
            for uu in range(ATTN_UNROLL):
                res, q0, k0, _ = place(g, uu)
                old = None if first else tuple(
                    gather(ref, res, q0, c_rows) for ref in (o_ref, m_ref, l_ref))
                loaded.append((res, q0, gather(v_ref, res, k0, 2 * c_rows).astype(BF16),
                               p_sc[slot, uu, 0], p_sc[slot, uu, 1], mx_sc[slot, uu], old))
            results = []
            for res, q0, v, p0, p1, m_new, old in loaded:
                ones = jnp.ones_like(v)
                o0 = _dot(p0, jnp.where(kv_head0, v, ones))
                o1 = _dot(p1, jnp.where(kv_head0, ones, v))
                o_new = jnp.where(head0, o0, o1)
                l_new = pltpu.roll(jnp.where(head0, o1, o0), HEAD_DIM, 1)
                if not first:
                    o_old, m_old, l_old = old
                    m_tot = jnp.maximum(m_old, m_new)
                    w_old = jnp.exp2(m_old - m_tot)
                    w_new = jnp.exp2(m_new - m_tot)
                    o_new = w_old * o_old + w_new * o_new
                    l_new = w_old * l_old + w_new * l_new
                    m_new = m_tot
                results.append((res, q0, o_new, m_new, l_new))
            for res, q0, o_new, m_new, l_new in results:
                if last:
                    scatter(o_ref, res, q0, o_new / l_new)
                else:
                    scatter(o_ref, res, q0, o_new)
                    scatter(m_ref, res, q0, m_new)
                    scatter(l_ref, res, q0, l_new)

        def step(g, carry, score_stage=score_stage, value_stage=value_stage):
            slot = g & 1
            value_stage(g - 1, 1 - slot)
            score_stage(g, slot)
            return carry

        score_stage(0, 0)
        lax.fori_loop(1, n_groups, step, 0)
        value_stage(n_groups - 1, (n_groups - 1) & 1)


def _attention(z, n_pairs, *, q_col, k_col, v_col):
    b, n_res, length, _ = z.shape
    assert n_res == TOKEN_RES == DILATIONS[-1] and all(n_res % d == 0 for d in DILATIONS)
    assert length % ATTN_BLOCK == 0 and length >= 2 * ATTN_BLOCK and ATTN_BLOCK // n_res >= 8
    assert (n_res * length // ATTN_BLOCK) % ATTN_UNROLL == 0

    def spec(col):
        return pl.BlockSpec((None, n_res, length, LANES), lambda bi, h: (bi, 0, 0, col + h))

    slab = n_res * length * LANES * 4
    vmem = 10 * slab * 1.15 + (8 << 20)
    return pl.pallas_call(
        _attn_kernel,
        grid=(b, n_pairs),
        in_specs=[spec(q_col), spec(k_col), spec(v_col)],
        out_specs=spec(0),
        out_shape=jax.ShapeDtypeStruct((b, n_res, length, n_pairs * LANES), F32),
        scratch_shapes=[pltpu.VMEM((n_res, length, LANES), F32),
                        pltpu.VMEM((n_res, length, LANES), F32),
                        pltpu.VMEM((2 * len(DILATIONS), ATTN_BLOCK, 2 * ATTN_BLOCK), F32),
                        pltpu.VMEM((2, ATTN_UNROLL, 2, ATTN_BLOCK, 2 * ATTN_BLOCK), BF16),
                        pltpu.VMEM((2, ATTN_UNROLL, ATTN_BLOCK, LANES), F32)],
        compiler_params=_params(("parallel", "parallel"), vmem),
        name="attention",
    )(z, z, z)


def _ssm_weights(lam_re, lam_im, log_dt, b_re, b_im, c_re, c_im, d_skip, n_chunks):
    hp = lax.Precision.HIGHEST
    g, p = lam_re.shape
    c = b_re.shape[-1]
    t = SSM_CHUNK
    gb = GROUPS_PER_BLOCK
    nj = g // gb
    lr, li = lam_re.astype(F32), lam_im.astype(F32)
    dt = jnp.exp(log_dt.astype(F32))[:, None]
    mag = jnp.exp(lr * dt)
    ar = mag * jnp.cos(li * dt)
    ai = mag * jnp.sin(li * dt)
    nr, ni = ar - 1.0, ai
    den = lr * lr + li * li
    cr = (nr * lr + ni * li) / den
    ci = (ni * lr - nr * li) / den
    br, bi = b_re.astype(F32), b_im.astype(F32)
    bbr = cr[..., None] * br - ci[..., None] * bi
    bbi = cr[..., None] * bi + ci[..., None] * br

    def power(k):
        kk = k.astype(F32)[:, None, None]
        pm = jnp.exp(kk * (lr * dt))
        ang = kk * (li * dt)
        return pm * jnp.cos(ang), pm * jnp.sin(ang)

    pr, pi = power(jnp.arange(t + 1))
    cre, cim = c_re.astype(F32), c_im.astype(F32)
    car = cre[None] * pr[:, :, None, :] - cim[None] * pi[:, :, None, :]
    cai = cre[None] * pi[:, :, None, :] + cim[None] * pr[:, :, None, :]
    kern = jnp.einsum('kgdq,gqc->kgdc', jnp.concatenate([car[:t], -cai[:t]], axis=-1),
                      jnp.concatenate([bbr, bbi], axis=1), precision=hp)
    kern = kern.at[0].add(d_skip.astype(F32).reshape(g, c)[:, :, None] * jnp.eye(c, dtype=F32))

    kr = kern.reshape(t, nj, gb, c, c).transpose(1, 0, 4, 2, 3).reshape(nj, t, c, gb * c)

    rev_r, rev_i = pr[:t][::-1][:, :, None, :], pi[:t][::-1][:, :, None, :]
    bbr_t, bbi_t = bbr.transpose(0, 2, 1)[None], bbi.transpose(0, 2, 1)[None]
    abr = rev_r * bbr_t - rev_i * bbi_t
    abi = rev_r * bbi_t + rev_i * bbr_t

    def block_rows(x):
        return x.reshape(t, nj, gb, c, p).transpose(1, 0, 2, 3, 4).reshape(nj, t * gb * c, p)

    def rows_form(x):
        x = block_rows(x)
        return jnp.concatenate([x, x], axis=-1).astype(BF16)

    def cols_form(x):
        x = block_rows(x).transpose(0, 2, 1)
        return jnp.concatenate([x, x], axis=1).astype(BF16)

    n_steps = max(1, (n_chunks - 1).bit_length())
    sr, si = power(t * (2 ** jnp.arange(n_steps)))
    rows = -(-n_steps // 8) * 8
    sr = jnp.pad(sr.reshape(n_steps, nj, gb * p).transpose(1, 0, 2), ((0, 0), (0, rows - n_steps), (0, 0)))
    si = jnp.pad(si.reshape(n_steps, nj, gb * p).transpose(1, 0, 2), ((0, 0), (0, rows - n_steps), (0, 0)))
    return (kr.astype(BF16), rows_form(abr), rows_form(abi),
            cols_form(car[1:]), cols_form(-cai[1:]), sr, si)


def _ssm_expand(kr_ref, abr_ref, abi_ref, cmr_ref, cmi_ref, m_sc, b_sc, c_sc):
    t, c, gb, p = SSM_CHUNK, SSM_GROUP, GROUPS_PER_BLOCK, SSM_STATE
    shift = c.bit_length() - 1
    half = gb * p
    pairs = LANES // p

    def group_of(idx):
        return (idx >> shift) & (gb - 1)

    m_sc[...] = jnp.zeros_like(m_sc)
    col_g = group_of(lax.broadcasted_iota(jnp.int32, (c, LANES), 1))
    for k in range(t):
        kk = kr_ref[k]
        dk = jnp.concatenate([jnp.where(col_g == g, kk, jnp.zeros_like(kk)) for g in range(gb)], axis=0)
        for tau in range(t - k):
            m_sc[tau * LANES:(tau + 1) * LANES, (tau + k) * LANES:(tau + k + 1) * LANES] = dk

    row_g = group_of(lax.broadcasted_iota(jnp.int32, (t * LANES, LANES), 0))
    lane_half = lax.broadcasted_iota(jnp.int32, (t * LANES, LANES), 1) >> (p.bit_length() - 1)
    for part, ab_ref in enumerate((abr_ref, abi_ref)):
        ab = ab_ref[...]
        for q in range(gb // pairs):
            b_sc[:, part * half + q * LANES:part * half + (q + 1) * LANES] = jnp.where(
                row_g == pairs * q + lane_half, ab, jnp.zeros_like(ab))

    col_g2 = group_of(lax.broadcasted_iota(jnp.int32, (LANES, t * LANES), 1))
    row_half = lax.broadcasted_iota(jnp.int32, (LANES, t * LANES), 0) >> (p.bit_length() - 1)
    for part, cm_ref in enumerate((cmr_ref, cmi_ref)):
        cm = cm_ref[...]
        for q in range(gb // pairs):
            c_sc[part * half + q * LANES:part * half + (q + 1) * LANES, :] = jnp.where(
                col_g2 == pairs * q + row_half, cm, jnp.zeros_like(cm))


def _ssm_kernel(s_ref, kr_ref, abr_ref, abi_ref, cmr_ref, cmi_ref, sr_ref, si_ref, y_ref,
                m_sc, b_sc, c_sc):
    t, n_chunks, _ = s_ref.shape
    half = GROUPS_PER_BLOCK * SSM_STATE

    @pl.when(pl.program_id(1) == 0)
    def _():
        _ssm_expand(kr_ref, abr_ref, abi_ref, cmr_ref, cmi_ref, m_sc, b_sc, c_sc)

    u = jnp.concatenate([s_ref[tau].astype(BF16) for tau in range(t)], axis=1)
    y = _dot(u, m_sc[...])
    x = _dot(u, b_sc[...])
    hr, hi = x[:, :half], x[:, half:]

    row = lax.broadcasted_iota(jnp.int32, (n_chunks, half), 0)

    def shifted(h, k):
        return jnp.where(row >= k, pltpu.roll(h, k, 0), 0.0)

    step = 0
    while (1 << step) < n_chunks:
        k = 1 << step
        pr = sr_ref[step:step + 1, :]
        pi = si_ref[step:step + 1, :]
        zr, zi = shifted(hr, k), shifted(hi, k)
        hr, hi = hr + pr * zr - pi * zi, hi + pr * zi + pi * zr
        step += 1

    h_prev = jnp.concatenate([shifted(hr, 1), shifted(hi, 1)], axis=1).astype(BF16)
    y = jax.nn.gelu(y + _dot(h_prev, c_sc[...]))
    for tau in range(t):
        y_ref[tau] = y[:, tau * LANES:(tau + 1) * LANES]


def _ssm(z, s_col, weights):
    b, n_res, n_chunks, _ = z.shape
    nj = weights[0].shape[0]
    assert n_res == SSM_CHUNK
    assert SSM_GROUP & (SSM_GROUP - 1) == 0 and SSM_STATE & (SSM_STATE - 1) == 0

    def wspec(a):
        nd = a.ndim - 1
        return pl.BlockSpec((None,) + a.shape[1:], lambda j, bi: (j,) + (0,) * nd)

    s = n_res * n_chunks
    width = SSM_CHUNK * LANES
    state = 2 * GROUPS_PER_BLOCK * SSM_STATE
    scratch = (width * width + 2 * width * state) * 2
    wbytes = 2 * sum(math.prod(a.shape[1:]) * a.dtype.itemsize for a in weights)
    vmem = (4 * s * LANES * 4 + wbytes + scratch + n_chunks * (width * 10 + state * 12)) * 1.2
    return pl.pallas_call(
        _ssm_kernel,
        grid=(nj, b),
        in_specs=[pl.BlockSpec((None, n_res, n_chunks, LANES),
                               lambda j, bi: (bi, 0, 0, s_col + j))]
        + [wspec(a) for a in weights],
        out_specs=pl.BlockSpec((None, n_res, n_chunks, LANES), lambda j, bi: (bi, 0, 0, j)),
        out_shape=jax.ShapeDtypeStruct((b, n_res, n_chunks, nj * LANES), F32),
        scratch_shapes=[pltpu.VMEM((width, width), BF16), pltpu.VMEM((width, state), BF16),
                        pltpu.VMEM((state, width), BF16)],
        compiler_params=_params(("parallel", "arbitrary"), vmem),
        name="ssm",
    )(z, *weights)


def _mix_out_kernel(ya_ref, y_ref, h_ref, ga_ref, gb_ref, wglu_ref, bglu_ref,
                    wa_ref, wb_ref, o_ref):
    n_res, tl, _ = ya_ref.shape
    sub = tl // MIX_SPLIT
    rows = n_res * sub
    perm = _residue_perm(n_res, sub, True)
    for s in range(MIX_SPLIT):
        ls = slice(s * sub, (s + 1) * sub)
        y = y_ref[:, ls, :].reshape(rows, -1)
        gate = _dot(y.astype(BF16), wglu_ref[...]) + bglu_ref[...]
        yb = y * jax.nn.sigmoid(gate)
        na = _rms(ya_ref[:, ls, :].reshape(rows, -1), ga_ref[...]).astype(BF16)
        nb = _rms(yb, gb_ref[...]).astype(BF16)
        na = _dot(perm, na).astype(BF16)
        nb = _dot(perm, nb).astype(BF16)
        tok = slice(s * rows, (s + 1) * rows)
        o_ref[tok, :] = h_ref[tok, :] + _dot(na, wa_ref[...]) + _dot(nb, wb_ref[...])


def _mix_out(ya, y, h, ga, gb, wglu, bglu, wa, wb, *, tm=512):
    m, d = h.shape
    batch, n_res, length, wa_w = ya.shape
    wb_w = y.shape[-1]
    tm = min(tm, n_res * length)
    tl = tm // n_res
    tiles = length // tl
    assert tl % (8 * MIX_SPLIT) == 0 and length % tl == 0
    vmem = (2 * tm * (wa_w + wb_w + 2 * d) * 4 + (wglu.size + wa.size + wb.size) * 2
            + 8 * tm * d * 4) * 1.2
    row = pl.BlockSpec((tm, d), lambda i: (i, 0))
    res_major = lambda w: pl.BlockSpec((None, n_res, tl, w),
                                       lambda i: (i // tiles, 0, i % tiles, 0))
    return pl.pallas_call(
        _mix_out_kernel,
        grid=(m // tm,),
        in_specs=[res_major(wa_w), res_major(wb_w), row,
                  _const_spec((1, wa_w)), _const_spec((1, wb_w)),
                  _const_spec(wglu.shape), _const_spec((1, wb_w)),
                  _const_spec(wa.shape), _const_spec(wb.shape)],
        out_specs=row,
        out_shape=jax.ShapeDtypeStruct((m, d), F32),
        compiler_params=_params(("parallel",), vmem),
        name="mix_out",
    )(ya, y, h, ga.reshape(1, -1), gb.reshape(1, -1), wglu, bglu.reshape(1, -1), wa, wb)


def _ple_kernel(h_ref, p_ref, gn_ref, wg_ref, wp_ref, fn_ref, o_ref, *, final):
    h = h_ref[...]
    gate = jax.nn.sigmoid(_dot(_rms(h, gn_ref[...]).astype(BF16), wg_ref[...]))
    proj = _dot(p_ref[...].astype(BF16), wp_ref[...])
    out = h + gate * proj
    o_ref[...] = _rms(out, fn_ref[...]) if final else out


def _ple(h, p, layer, gn, wg, wp, fn, *, final, tm=512):
    m, d = h.shape
    pd = p.shape[2]
    tm = min(tm, m)
    vmem = (2 * tm * (2 * d + pd) * 4 + (wg.size + wp.size) * 2 + 8 * tm * d * 4) * 1.2
    row = lambda w: pl.BlockSpec((tm, w), lambda i: (i, 0))
    return pl.pallas_call(
        functools.partial(_ple_kernel, final=final),
        grid=(m // tm,),
        in_specs=[row(d), pl.BlockSpec((None, tm, pd), lambda i: (layer, i, 0)),
                  _const_spec((1, d)), _const_spec(wg.shape),
                  _const_spec(wp.shape), _const_spec((1, d))],
        out_specs=row(d),
        out_shape=jax.ShapeDtypeStruct((m, d), F32),
        compiler_params=_params(("parallel",), vmem),
        name="ple",
    )(h, p, gn.reshape(1, d), wg, wp, fn.reshape(1, d))


def kernel(x, p, ffn1_norm, ffn1_w_gate, ffn1_w_up, ffn1_w_down, mix_norm, w_in, attn_out_norm, ssm_lambda_re, ssm_lambda_im, ssm_log_dt, ssm_b_re, ssm_b_im, ssm_c_re, ssm_c_im, ssm_d, ssm_w_glu, ssm_b_glu, ssm_out_norm, w_out, ffn2_norm, ffn2_w_gate, ffn2_w_up, ffn2_w_down, ple_norm, ple_w_gate, ple_w_proj, final_norm):
    b, s, d = x.shape
    depth = p.shape[0]
    attn_w = attn_out_norm.shape[1]
    ssm_w = ssm_out_norm.shape[1]
    n_pairs = attn_w // LANES
    h = x.reshape(b * s, d)
    for i in range(depth):
        h = _ffn(h, ffn1_norm[i], _ff_tiles(ffn1_w_gate, i, 1), _ff_tiles(ffn1_w_up, i, 1),
                 _ff_tiles(ffn1_w_down, i, 0))
        z = _in_proj(h, mix_norm[i], w_in[i].astype(BF16), b)
        ya = _attention(z, n_pairs, q_col=0, k_col=n_pairs, v_col=2 * n_pairs)
        weights = _ssm_weights(ssm_lambda_re[i], ssm_lambda_im[i], ssm_log_dt[i],
                               ssm_b_re[i], ssm_b_im[i], ssm_c_re[i], ssm_c_im[i],
                               ssm_d[i], s // SSM_CHUNK)
        y = _ssm(z, 3 * n_pairs, weights)
        wo = w_out[i].astype(BF16)
        h = _mix_out(ya, y, h, attn_out_norm[i], ssm_out_norm[i], ssm_w_glu[i].astype(BF16),
                     ssm_b_glu[i], wo[:attn_w], wo[attn_w:])
        h = _ffn(h, ffn2_norm[i], _ff_tiles(ffn2_w_gate, i, 1), _ff_tiles(ffn2_w_up, i, 1),
                 _ff_tiles(ffn2_w_down, i, 0))
        h = _ple(h, p.reshape(depth, b * s, -1), i, ple_norm[i], ple_w_gate[i].astype(BF16),
                 ple_w_proj[i].astype(BF16), final_norm, final=i == depth - 1)
    return h.reshape(b, s, d)
```

```python
import functools
import math

import jax
import jax.numpy as jnp
from jax import lax
from jax.experimental import pallas as pl
from jax.experimental.pallas import tpu as pltpu

NORM_EPS = 1e-6
MASK_VALUE = -1e30
HEAD_DIM = 64
ATTN_BLOCK = 128
ATTN_SPAN = 128
DILATIONS = (1, 4, 16)
ATTN_UNROLL = 16
SSM_GROUP = 16
SSM_STATE = 64
SSM_CHUNK = 16
TOKEN_RES = 16
LANES = 128
GROUPS_PER_BLOCK = LANES // SSM_GROUP
FF_TILE = 256
FF_STAGE = 512
MIX_SPLIT = 2
IN_SPLIT = 2
VMEM_CAP = 56 * 1024 * 1024

BF16 = jnp.bfloat16
F32 = jnp.float32


def _params(sem, vmem_bytes):
    return pltpu.CompilerParams(
        dimension_semantics=sem, vmem_limit_bytes=min(int(vmem_bytes), VMEM_CAP))


def _rms(x, g):
    ms = jnp.mean(x * x, axis=-1, keepdims=True)
    return x * lax.rsqrt(ms + NORM_EPS) * g


def _dot(a, b):
    return jnp.dot(a, b, preferred_element_type=F32)


def _const_spec(shape):
    return pl.BlockSpec(shape, lambda *_: (0,) * len(shape),
                        pipeline_mode=pl.Buffered(1))


def _ffn_kernel(x_ref, g_ref, wg_ref, wu_ref, wd_ref, o_ref, xn_ref):
    f = pl.program_id(1)

    @pl.when(f == 0)
    def _():
        xn_ref[...] = _rms(x_ref[...], g_ref[...]).astype(BF16)
        o_ref[...] = jnp.zeros_like(o_ref)

    xn = xn_ref[...]
    gate = _dot(xn, wg_ref[...])
    up = _dot(xn, wu_ref[...])
    act = (gate * jax.nn.sigmoid(gate) * up).astype(BF16)
    o_ref[...] += _dot(act, wd_ref[...])

    @pl.when(f == pl.num_programs(1) - 1)
    def _():
        o_ref[...] = x_ref[...] + 0.5 * o_ref[...]


def _ffn(h, g, wg, wu, wd, *, tm=1024):
    m, d = h.shape
    nf, _, tf = wg.shape
    tm = min(tm, m)
    vmem = (4 * tm * d * 4 + tm * d * 2 + 6 * d * tf * 2 + 6 * tm * tf * 4) * 1.25
    row = pl.BlockSpec((tm, d), lambda i, f: (i, 0))
    return pl.pallas_call(
        _ffn_kernel,
        grid=(m // tm, nf),
        in_specs=[
            row,
            pl.BlockSpec((1, d), lambda i, f: (0, 0)),
            pl.BlockSpec((None, d, tf), lambda i, f: (f, 0, 0)),
            pl.BlockSpec((None, d, tf), lambda i, f: (f, 0, 0)),
            pl.BlockSpec((None, tf, d), lambda i, f: (f, 0, 0)),
        ],
        out_specs=row,
        out_shape=jax.ShapeDtypeStruct((m, d), F32),
        scratch_shapes=[pltpu.VMEM((tm, d), BF16)],
        compiler_params=_params(("parallel", "arbitrary"), vmem),
        name="ffn",
    )(h, g.reshape(1, d), wg, wu, wd)


def _ff_tile_kernel(w_ref, o_ref, *, axis, size):
    w = w_ref[...]
    n_tiles = o_ref.shape[0]
    tile = o_ref.shape[1 + axis]
    pos = lax.broadcasted_iota(jnp.int32, w.shape, axis) + pl.program_id(0) * (n_tiles * tile)
    w = jnp.where(pos < size, w, 0.0).astype(o_ref.dtype)
    for t in range(n_tiles):
        o_ref[t] = w[:, t * tile:(t + 1) * tile] if axis == 1 else w[t * tile:(t + 1) * tile]


def _ff_tiles(w, layer, axis, tf=FF_TILE):
    size = w.shape[1 + axis]
    per_step = max(1, FF_STAGE // tf)
    nf = pl.cdiv(size, tf * per_step) * per_step
    tile = list(w.shape[1:])
    tile[axis] = tf
    stage = list(tile)
    stage[axis] = tf * per_step
    index = (lambda f: (layer, 0, f)) if axis == 1 else (lambda f: (layer, f, 0))
    return pl.pallas_call(
        functools.partial(_ff_tile_kernel, axis=axis, size=size),
        grid=(nf // per_step,),
        in_specs=[pl.BlockSpec((None,) + tuple(stage), index)],
        out_specs=pl.BlockSpec((per_step,) + tuple(tile), lambda f: (f, 0, 0)),
        out_shape=jax.ShapeDtypeStruct((nf,) + tuple(tile), BF16),
        compiler_params=_params(("parallel",), 6 * math.prod(stage) * 4),
        name="ff_tiles",
    )(w)


def _residue_perm(n_res, tl, transpose):
    tm = n_res * tl
    assert tl & (tl - 1) == 0
    a = lax.broadcasted_iota(jnp.int32, (tm, tm), 1 if transpose else 0)
    b = lax.broadcasted_iota(jnp.int32, (tm, tm), 0 if transpose else 1)
    hit = b == n_res * (a & (tl - 1)) + (a >> (tl.bit_length() - 1))
    return jnp.where(hit, 1.0, 0.0).astype(BF16)


def _in_proj_kernel(x_ref, g_ref, w_ref, o_ref):
    n_res, tl, _ = o_ref.shape
    sub = tl // IN_SPLIT
    rows = n_res * sub
    perm = _residue_perm(n_res, sub, False)
    for s in range(IN_SPLIT):
        xn = _rms(x_ref[s * rows:(s + 1) * rows, :], g_ref[...]).astype(BF16)
        xn = _dot(perm, xn).astype(BF16)
        res = _dot(xn, w_ref[...])
        for r in range(n_res):
            o_ref[r, s * sub:(s + 1) * sub, :] = res[r * sub:(r + 1) * sub]


def _in_proj(h, g, w, batch, *, tm=512):
    m, d = h.shape
    n = w.shape[1]
    seq = m // batch
    tm = min(tm, seq)
    tl = tm // TOKEN_RES
    tiles = seq // tm
    assert tl % (8 * IN_SPLIT) == 0 and seq % tm == 0
    vmem = (2 * tm * d * 4 + tm * d * 6 + d * n * 2 + 3 * tm * n * 4 + 4 * tm * tm) * 1.2
    return pl.pallas_call(
        _in_proj_kernel,
        grid=(m // tm,),
        in_specs=[
            pl.BlockSpec((tm, d), lambda i: (i, 0)),
            _const_spec((1, d)),
            _const_spec((d, n)),
        ],
        out_specs=pl.BlockSpec((None, TOKEN_RES, tl, n),
                               lambda i: (i // tiles, 0, i % tiles, 0)),
        out_shape=jax.ShapeDtypeStruct((batch, TOKEN_RES, seq // TOKEN_RES, n), F32),
        compiler_params=_params(("parallel",), vmem),
        name="in_proj",
    )(h, g.reshape(1, d), w)


def _attn_kernel(q_ref, k_ref, v_ref, o_ref, m_ref, l_ref, bias_ref, p_sc, mx_sc):
    n_res, length, _ = q_ref.shape
    blk = ATTN_BLOCK
    n_units = n_res * length // blk
    head0 = lax.broadcasted_iota(jnp.int32, (blk, LANES), 1) < HEAD_DIM
    kv_head0 = lax.broadcasted_iota(jnp.int32, (2 * blk, LANES), 1) < HEAD_DIM
    scale = HEAD_DIM ** -0.5 * math.log2(math.e)

    rho = lax.broadcasted_iota(jnp.int32, (blk, 2 * blk), 0)
    kap = lax.broadcasted_iota(jnp.int32, (blk, 2 * blk), 1)
    for pattern, d in enumerate(DILATIONS):
        n_sub = n_res // d
        c_rows = blk // n_sub
        sh = c_rows.bit_length() - 1
        iq, lq = rho >> sh, rho & (c_rows - 1)
        ik, lk = kap >> (sh + 1), kap & (2 * c_rows - 1)
        for shift in range(2):
            dist = n_sub * (lq - lk + shift * c_rows) + (iq - ik)
            bias_ref[2 * pattern + shift] = jnp.where(
                (dist >= 0) & (dist <= ATTN_SPAN), 0.0, MASK_VALUE)

    def probs(q, k, bias, sel):
        qh = jnp.where(sel, q, 0.0).astype(BF16)
        s = lax.dot_general(qh, k, (((1,), (1,)), ((), ())),
                            preferred_element_type=F32) + bias
        m = jnp.max(s, axis=-1, keepdims=True)
        return jnp.exp2(s - m).astype(BF16), m

    n_groups = n_units // ATTN_UNROLL
    for pattern, d in enumerate(DILATIONS):
        first = pattern == 0
        last = pattern == len(DILATIONS) - 1
        n_sub = n_res // d
        c_rows = blk // n_sub
        blocks_per_residue = n_units // d

        def gather(ref, res, start, size, d=d, n_sub=n_sub):
            return jnp.concatenate(
                [ref[d * i + res, pl.ds(start, size), :] for i in range(n_sub)], axis=0)

        def scatter(ref, res, start, val, d=d, n_sub=n_sub, c_rows=c_rows):
            for i in range(n_sub):
                ref[d * i + res, pl.ds(start, c_rows), :] = val[i * c_rows:(i + 1) * c_rows]

        def place(g, uu, c_rows=c_rows, blocks_per_residue=blocks_per_residue):
            u = g * ATTN_UNROLL + uu
            res = u // blocks_per_residue
            bq = u % blocks_per_residue
            kb = jnp.maximum(bq - 1, 0)
            return (res, pl.multiple_of(c_rows * bq, 8), pl.multiple_of(c_rows * kb, 8), bq - kb)

        def score_stage(g, slot, pattern=pattern, c_rows=c_rows, gather=gather, place=place):
            loaded = []
            for uu in range(ATTN_UNROLL):
                res, q0, k0, shift = place(g, uu)
                loaded.append((gather(q_ref, res, q0, c_rows) * scale,
                               gather(k_ref, res, k0, 2 * c_rows).astype(BF16),
                               bias_ref[2 * pattern + shift]))
            for uu, (q, k, bias) in enumerate(loaded):
                p0, m0 = probs(q, k, bias, head0)
                p1, m1 = probs(q, k, bias, jnp.logical_not(head0))
                p_sc[slot, uu, 0] = p0
                p_sc[slot, uu, 1] = p1
                mx_sc[slot, uu] = jnp.where(head0, m0, m1)

        def value_stage(g, slot, first=first, last=last, c_rows=c_rows,
                        gather=gather, scatter=scatter, place=place):
            loaded = []
            for uu in range(ATTN_UNROLL):
                res, q0, k0, _ = place(g, uu)
                old = None if first else tuple(
                    gather(ref, res, q0, c_rows) for ref in (o_ref, m_ref, l_ref))
                loaded.append((res, q0, gather(v_ref, res, k0, 2 * c_rows).astype(BF16),
                               p_sc[slot, uu, 0], p_sc[slot, uu, 1], mx_sc[slot, uu], old))
            results = []
            for res, q0, v, p0, p1, m_new, old in loaded:
                ones = jnp.ones_like(v)
                o0 = _dot(p0, jnp.where(kv_head0, v, ones))
                o1 = _dot(p1, jnp.where(kv_head0, ones, v))
                o_new = jnp.where(head0, o0, o1)
                l_new = pltpu.roll(jnp.where(head0, o1, o0), HEAD_DIM, 1)
                if not first:
                    o_old, m_old, l_old = old
                    m_tot = jnp.maximum(m_old, m_new)
                    w_old = jnp.exp2(m_old - m_tot)
                    w_new = jnp.exp2(m_new - m_tot)
                    o_new = w_old * o_old + w_new * o_new
                    l_new = w_old * l_old + w_new * l_new
                    m_new = m_tot
                results.append((res, q0, o_new, m_new, l_new))
            for res, q0, o_new, m_new, l_new in results:
                if last:
                    scatter(o_ref, res, q0, o_new / l_new)
                else:
                    scatter(o_ref, res, q0, o_new)
                    scatter(m_ref, res, q0, m_new)
                    scatter(l_ref, res, q0, l_new)

        def step(g, carry, score_stage=score_stage, value_stage=value_stage):
            slot = g & 1
            value_stage(g - 1, 1 - slot)
            score_stage(g, slot)
            return carry

        score_stage(0, 0)
        lax.fori_loop(1, n_groups, step, 0)
        value_stage(n_groups - 1, (n_groups - 1) & 1)


def _attention(z, n_pairs, *, q_col, k_col, v_col):
    b, n_res, length, _ = z.shape
    assert n_res == TOKEN_RES == DILATIONS[-1] and all(n_res % d == 0 for d in DILATIONS)
    assert length % ATTN_BLOCK == 0 and length >= 2 * ATTN_BLOCK and ATTN_BLOCK // n_res >= 8
    assert (n_res * length // ATTN_BLOCK) % ATTN_UNROLL == 0

    def spec(col):
        return pl.BlockSpec((None, n_res, length, LANES), lambda bi, h: (bi, 0, 0, col + h))

    slab = n_res * length * LANES * 4
    vmem = 10 * slab * 1.15 + (8 << 20)
    return pl.pallas_call(
        _attn_kernel,
        grid=(b, n_pairs),
        in_specs=[spec(q_col), spec(k_col), spec(v_col)],
        out_specs=spec(0),
        out_shape=jax.ShapeDtypeStruct((b, n_res, length, n_pairs * LANES), F32),
        scratch_shapes=[pltpu.VMEM((n_res, length, LANES), F32),
                        pltpu.VMEM((n_res, length, LANES), F32),
                        pltpu.VMEM((2 * len(DILATIONS), ATTN_BLOCK, 2 * ATTN_BLOCK), F32),
                        pltpu.VMEM((2, ATTN_UNROLL, 2, ATTN_BLOCK, 2 * ATTN_BLOCK), BF16),
                        pltpu.VMEM((2, ATTN_UNROLL, ATTN_BLOCK, LANES), F32)],
        compiler_params=_params(("parallel", "parallel"), vmem),
        name="attention",
    )(z, z, z)


def _ssm_weights(lam_re, lam_im, log_dt, b_re, b_im, c_re, c_im, d_skip, n_chunks):
    hp = lax.Precision.HIGHEST
    g, p = lam_re.shape
    c = b_re.shape[-1]
    t = SSM_CHUNK
    gb = GROUPS_PER_BLOCK
    nj = g // gb
    lr, li = lam_re.astype(F32), lam_im.astype(F32)
    dt = jnp.exp(log_dt.astype(F32))[:, None]
    mag = jnp.exp(lr * dt)
    ar = mag * jnp.cos(li * dt)
    ai = mag * jnp.sin(li * dt)
    nr, ni = ar - 1.0, ai
    den = lr * lr + li * li
    cr = (nr * lr + ni * li) / den
    ci = (ni * lr - nr * li) / den
    br, bi = b_re.astype(F32), b_im.astype(F32)
    bbr = cr[..., None] * br - ci[..., None] * bi
    bbi = cr[..., None] * bi + ci[..., None] * br

    def power(k):
        kk = k.astype(F32)[:, None, None]
        pm = jnp.exp(kk * (lr * dt))
        ang = kk * (li * dt)
        return pm * jnp.cos(ang), pm * jnp.sin(ang)

    pr, pi = power(jnp.arange(t + 1))
    cre, cim = c_re.astype(F32), c_im.astype(F32)
    car = cre[None] * pr[:, :, None, :] - cim[None] * pi[:, :, None, :]
    cai = cre[None] * pi[:, :, None, :] + cim[None] * pr[:, :, None, :]
    kern = jnp.einsum('kgdq,gqc->kgdc', jnp.concatenate([car[:t], -cai[:t]], axis=-1),
                      jnp.concatenate([bbr, bbi], axis=1), precision=hp)
    kern = kern.at[0].add(d_skip.astype(F32).reshape(g, c)[:, :, None] * jnp.eye(c, dtype=F32))

    kr = kern.reshape(t, nj, gb, c, c).transpose(1, 0, 4, 2, 3).reshape(nj, t, c, gb * c)

    rev_r, rev_i = pr[:t][::-1][:, :, None, :], pi[:t][::-1][:, :, None, :]
    bbr_t, bbi_t = bbr.transpose(0, 2, 1)[None], bbi.transpose(0, 2, 1)[None]
    abr = rev_r * bbr_t - rev_i * bbi_t
    abi = rev_r * bbi_t + rev_i * bbr_t

    def block_rows(x):
        return x.reshape(t, nj, gb, c, p).transpose(1, 0, 2, 3, 4).reshape(nj, t * gb * c, p)

    def rows_form(x):
        x = block_rows(x)
        return jnp.concatenate([x, x], axis=-1).astype(BF16)

    def cols_form(x):
        x = block_rows(x).transpose(0, 2, 1)
        return jnp.concatenate([x, x], axis=1).astype(BF16)

    n_steps = max(1, (n_chunks - 1).bit_length())
    sr, si = power(t * (2 ** jnp.arange(n_steps)))
    rows = -(-n_steps // 8) * 8
    sr = jnp.pad(sr.reshape(n_steps, nj, gb * p).transpose(1, 0, 2), ((0, 0), (0, rows - n_steps), (0, 0)))
    si = jnp.pad(si.reshape(n_steps, nj, gb * p).transpose(1, 0, 2), ((0, 0), (0, rows - n_steps), (0, 0)))
    return (kr.astype(BF16), rows_form(abr), rows_form(abi),
            cols_form(car[1:]), cols_form(-cai[1:]), sr, si)


def _ssm_expand(kr_ref, abr_ref, abi_ref, cmr_ref, cmi_ref, m_sc, b_sc, c_sc):
    t, c, gb, p = SSM_CHUNK, SSM_GROUP, GROUPS_PER_BLOCK, SSM_STATE
    shift = c.bit_length() - 1
    half = gb * p
    pairs = LANES // p

    def group_of(idx):
        return (idx >> shift) & (gb - 1)

    m_sc[...] = jnp.zeros_like(m_sc)
    col_g = group_of(lax.broadcasted_iota(jnp.int32, (c, LANES), 1))
    for k in range(t):
        kk = kr_ref[k]
        dk = jnp.concatenate([jnp.where(col_g == g, kk, jnp.zeros_like(kk)) for g in range(gb)], axis=0)
        for tau in range(t - k):
            m_sc[tau * LANES:(tau + 1) * LANES, (tau + k) * LANES:(tau + k + 1) * LANES] = dk

    row_g = group_of(lax.broadcasted_iota(jnp.int32, (t * LANES, LANES), 0))
    lane_half = lax.broadcasted_iota(jnp.int32, (t * LANES, LANES), 1) >> (p.bit_length() - 1)
    for part, ab_ref in enumerate((abr_ref, abi_ref)):
        ab = ab_ref[...]
        for q in range(gb // pairs):
            b_sc[:, part * half + q * LANES:part * half + (q + 1) * LANES] = jnp.where(
                row_g == pairs * q + lane_half, ab, jnp.zeros_like(ab))

    col_g2 = group_of(lax.broadcasted_iota(jnp.int32, (LANES, t * LANES), 1))
    row_half = lax.broadcasted_iota(jnp.int32, (LANES, t * LANES), 0) >> (p.bit_length() - 1)
    for part, cm_ref in enumerate((cmr_ref, cmi_ref)):
        cm = cm_ref[...]
        for q in range(gb // pairs):
            c_sc[part * half + q * LANES:part * half + (q + 1) * LANES, :] = jnp.where(
                col_g2 == pairs * q + row_half, cm, jnp.zeros_like(cm))


def _ssm_kernel(s_ref, kr_ref, abr_ref, abi_ref, cmr_ref, cmi_ref, sr_ref, si_ref, y_ref,
                m_sc, b_sc, c_sc):
    t, n_chunks, _ = s_ref.shape
    half = GROUPS_PER_BLOCK * SSM_STATE

    @pl.when(pl.program_id(1) == 0)
    def _():
        _ssm_expand(kr_ref, abr_ref, abi_ref, cmr_ref, cmi_ref, m_sc, b_sc, c_sc)

    u = jnp.concatenate([s_ref[tau].astype(BF16) for tau in range(t)], axis=1)
    y = _dot(u, m_sc[...])
    x = _dot(u, b_sc[...])
    hr, hi = x[:, :half], x[:, half:]

    row = lax.broadcasted_iota(jnp.int32, (n_chunks, half), 0)

    def shifted(h, k):
        return jnp.where(row >= k, pltpu.roll(h, k, 0), 0.0)

    step = 0
    while (1 << step) < n_chunks:
        k = 1 << step
        pr = sr_ref[step:step + 1, :]
        pi = si_ref[step:step + 1, :]
        zr, zi = shifted(hr, k), shifted(hi, k)
        hr, hi = hr + pr * zr - pi * zi, hi + pr * zi + pi * zr
        step += 1

    h_prev = jnp.concatenate([shifted(hr, 1), shifted(hi, 1)], axis=1).astype(BF16)
    y = jax.nn.gelu(y + _dot(h_prev, c_sc[...]))
    for tau in range(t):
        y_ref[tau] = y[:, tau * LANES:(tau + 1) * LANES]


def _ssm(z, s_col, weights):
    b, n_res, n_chunks, _ = z.shape
    nj = weights[0].shape[0]
    assert n_res == SSM_CHUNK
    assert SSM_GROUP & (SSM_GROUP - 1) == 0 and SSM_STATE & (SSM_STATE - 1) == 0

    def wspec(a):
        nd = a.ndim - 1
        return pl.BlockSpec((None,) + a.shape[1:], lambda j, bi: (j,) + (0,) * nd)

    s = n_res * n_chunks
    width = SSM_CHUNK * LANES
    state = 2 * GROUPS_PER_BLOCK * SSM_STATE
    scratch = (width * width + 2 * width * state) * 2
    wbytes = 2 * sum(math.prod(a.shape[1:]) * a.dtype.itemsize for a in weights)
    vmem = (4 * s * LANES * 4 + wbytes + scratch + n_chunks * (width * 10 + state * 12)) * 1.2
    return pl.pallas_call(
        _ssm_kernel,
        grid=(nj, b),
        in_specs=[pl.BlockSpec((None, n_res, n_chunks, LANES),
                               lambda j, bi: (bi, 0, 0, s_col + j))]
        + [wspec(a) for a in weights],
        out_specs=pl.BlockSpec((None, n_res, n_chunks, LANES), lambda j, bi: (bi, 0, 0, j)),
        out_shape=jax.ShapeDtypeStruct((b, n_res, n_chunks, nj * LANES), F32),
        scratch_shapes=[pltpu.VMEM((width, width), BF16), pltpu.VMEM((width, state), BF16),
                        pltpu.VMEM((state, width), BF16)],
        compiler_params=_params(("parallel", "arbitrary"), vmem),
        name="ssm",
    )(z, *weights)


def _mix_out_kernel(ya_ref, y_ref, h_ref, ga_ref, gb_ref, wglu_ref, bglu_ref,
                    wa_ref, wb_ref, o_ref):
    n_res, tl, _ = ya_ref.shape
    sub = tl // MIX_SPLIT
    rows = n_res * sub
    perm = _residue_perm(n_res, sub, True)
    for s in range(MIX_SPLIT):
        ls = slice(s * sub, (s + 1) * sub)
        y = y_ref[:, ls, :].reshape(rows, -1)
        gate = _dot(y.astype(BF16), wglu_ref[...]) + bglu_ref[...]
        yb = y * jax.nn.sigmoid(gate)
        na = _rms(ya_ref[:, ls, :].reshape(rows, -1), ga_ref[...]).astype(BF16)
        nb = _rms(yb, gb_ref[...]).astype(BF16)
        na = _dot(perm, na).astype(BF16)
        nb = _dot(perm, nb).astype(BF16)
        tok = slice(s * rows, (s + 1) * rows)
        o_ref[tok, :] = h_ref[tok, :] + _dot(na, wa_ref[...]) + _dot(nb, wb_ref[...])


def _mix_out(ya, y, h, ga, gb, wglu, bglu, wa, wb, *, tm=512):
    m, d = h.shape
    batch, n_res, length, wa_w = ya.shape
    wb_w = y.shape[-1]
    tm = min(tm, n_res * length)
    tl = tm // n_res
    tiles = length // tl
    assert tl % (8 * MIX_SPLIT) == 0 and length % tl == 0
    vmem = (2 * tm * (wa_w + wb_w + 2 * d) * 4 + (wglu.size + wa.size + wb.size) * 2
            + 8 * tm * d * 4) * 1.2
    row = pl.BlockSpec((tm, d), lambda i: (i, 0))
    res_major = lambda w: pl.BlockSpec((None, n_res, tl, w),
                                       lambda i: (i // tiles, 0, i % tiles, 0))
    return pl.pallas_call(
        _mix_out_kernel,
        grid=(m // tm,),
        in_specs=[res_major(wa_w), res_major(wb_w), row,
                  _const_spec((1, wa_w)), _const_spec((1, wb_w)),
                  _const_spec(wglu.shape), _const_spec((1, wb_w)),
                  _const_spec(wa.shape), _const_spec(wb.shape)],
        out_specs=row,
        out_shape=jax.ShapeDtypeStruct((m, d), F32),
        compiler_params=_params(("parallel",), vmem),
        name="mix_out",
    )(ya, y, h, ga.reshape(1, -1), gb.reshape(1, -1), wglu, bglu.reshape(1, -1), wa, wb)


def _ple_kernel(h_ref, p_ref, gn_ref, wg_ref, wp_ref, fn_ref, o_ref, *, final):
    h = h_ref[...]
    gate = jax.nn.sigmoid(_dot(_rms(h, gn_ref[...]).astype(BF16), wg_ref[...]))
    proj = _dot(p_ref[...].astype(BF16), wp_ref[...])
    out = h + gate * proj
    o_ref[...] = _rms(out, fn_ref[...]) if final else out


def _ple(h, p, layer, gn, wg, wp, fn, *, final, tm=512):
    m, d = h.shape
    pd = p.shape[2]
    tm = min(tm, m)
    vmem = (2 * tm * (2 * d + pd) * 4 + (wg.size + wp.size) * 2 + 8 * tm * d * 4) * 1.2
    row = lambda w: pl.BlockSpec((tm, w), lambda i: (i, 0))
    return pl.pallas_call(
        functools.partial(_ple_kernel, final=final),
        grid=(m // tm,),
        in_specs=[row(d), pl.BlockSpec((None, tm, pd), lambda i: (layer, i, 0)),
                  _const_spec((1, d)), _const_spec(wg.shape),
                  _const_spec(wp.shape), _const_spec((1, d))],
        out_specs=row(d),
        out_shape=jax.ShapeDtypeStruct((m, d), F32),
        compiler_params=_params(("parallel",), vmem),
        name="ple",
    )(h, p, gn.reshape(1, d), wg, wp, fn.reshape(1, d))


def kernel(x, p, ffn1_norm, ffn1_w_gate, ffn1_w_up, ffn1_w_down, mix_norm, w_in, attn_out_norm, ssm_lambda_re, ssm_lambda_im, ssm_log_dt, ssm_b_re, ssm_b_im, ssm_c_re, ssm_c_im, ssm_d, ssm_w_glu, ssm_b_glu, ssm_out_norm, w_out, ffn2_norm, ffn2_w_gate, ffn2_w_up, ffn2_w_down, ple_norm, ple_w_gate, ple_w_proj, final_norm):
    b, s, d = x.shape
    depth = p.shape[0]
    attn_w = attn_out_norm.shape[1]
    ssm_w = ssm_out_norm.shape[1]
    n_pairs = attn_w // LANES
    h = x.reshape(b * s, d)
    for i in range(depth):
        h = _ffn(h, ffn1_norm[i], _ff_tiles(ffn1_w_gate, i, 1), _ff_tiles(ffn1_w_up, i, 1),
                 _ff_tiles(ffn1_w_down, i, 0))
        z = _in_proj(h, mix_norm[i], w_in[i].astype(BF16), b)
        ya = _attention(z, n_pairs, q_col=0, k_col=n_pairs, v_col=2 * n_pairs)
        weights = _ssm_weights(ssm_lambda_re[i], ssm_lambda_im[i], ssm_log_dt[i],
                               ssm_b_re[i], ssm_b_im[i], ssm_c_re[i], ssm_c_im[i],
                               ssm_d[i], s // SSM_CHUNK)
        y = _ssm(z, 3 * n_pairs, weights)
        wo = w_out[i].astype(BF16)
        h = _mix_out(ya, y, h, attn_out_norm[i], ssm_out_norm[i], ssm_w_glu[i].astype(BF16),
                     ssm_b_glu[i], wo[:attn_w], wo[attn_w:])
        h = _ffn(h, ffn2_norm[i], _ff_tiles(ffn2_w_gate, i, 1), _ff_tiles(ffn2_w_up, i, 1),
                 _ff_tiles(ffn2_w_down, i, 0))
        h = _ple(h, p.reshape(depth, b * s, -1), i, ple_norm[i], ple_w_gate[i].astype(BF16),
                 ple_w_proj[i].astype(BF16), final_norm, final=i == depth - 1)
    return h.reshape(b, s, d)
```

```python
import functools
import math

import jax
import jax.numpy as jnp
from jax import lax
from jax.experimental import pallas as pl
from jax.experimental.pallas import tpu as pltpu

NORM_EPS = 1e-6
MASK_VALUE = -1e30
HEAD_DIM = 64
ATTN_BLOCK = 128
ATTN_SPAN = 128
DILATIONS = (1, 4, 16)
ATTN_UNROLL = 16
SSM_GROUP = 16
SSM_STATE = 64
SSM_CHUNK = 16
TOKEN_RES = 16
LANES = 128
GROUPS_PER_BLOCK = LANES // SSM_GROUP
FF_TILE = 256
FF_STAGE = 512
MIX_SPLIT = 2
IN_SPLIT = 2
SSM_SPLIT = 2
VMEM_CAP = 56 * 1024 * 1024

BF16 = jnp.bfloat16
F32 = jnp.float32


def _params(sem, vmem_bytes):
    return pltpu.CompilerParams(
        dimension_semantics=sem, vmem_limit_bytes=min(int(vmem_bytes), VMEM_CAP))


def _rms(x, g):
    ms = jnp.mean(x * x, axis=-1, keepdims=True)
    return x * lax.rsqrt(ms + NORM_EPS) * g


def _dot(a, b):
    return jnp.dot(a, b, preferred_element_type=F32)


def _const_spec(shape):
    return pl.BlockSpec(shape, lambda *_: (0,) * len(shape),
                        pipeline_mode=pl.Buffered(1))


def _ffn_kernel(x_ref, g_ref, wg_ref, wu_ref, wd_ref, o_ref, xn_ref):
    f = pl.program_id(1)

    @pl.when(f == 0)
    def _():
        xn_ref[...] = _rms(x_ref[...], g_ref[...]).astype(BF16)
        o_ref[...] = jnp.zeros_like(o_ref)

    xn = xn_ref[...]
    gate = _dot(xn, wg_ref[...])
    up = _dot(xn, wu_ref[...])
    act = (gate * jax.nn.sigmoid(gate) * up).astype(BF16)
    o_ref[...] += _dot(act, wd_ref[...])

    @pl.when(f == pl.num_programs(1) - 1)
    def _():
        o_ref[...] = x_ref[...] + 0.5 * o_ref[...]


def _ffn(h, g, wg, wu, wd, *, tm=1024):
    m, d = h.shape
    nf, _, tf = wg.shape
    tm = min(tm, m)
    vmem = (4 * tm * d * 4 + tm * d * 2 + 6 * d * tf * 2 + 6 * tm * tf * 4) * 1.25
    row = pl.BlockSpec((tm, d), lambda i, f: (i, 0))
    return pl.pallas_call(
        _ffn_kernel,
        grid=(m // tm, nf),
        in_specs=[
            row,
            pl.BlockSpec((1, d), lambda i, f: (0, 0)),
            pl.BlockSpec((None, d, tf), lambda i, f: (f, 0, 0)),
            pl.BlockSpec((None, d, tf), lambda i, f: (f, 0, 0)),
            pl.BlockSpec((None, tf, d), lambda i, f: (f, 0, 0)),
        ],
        out_specs=row,
        out_shape=jax.ShapeDtypeStruct((m, d), F32),
        scratch_shapes=[pltpu.VMEM((tm, d), BF16)],
        compiler_params=_params(("parallel", "arbitrary"), vmem),
        name="ffn",
    )(h, g.reshape(1, d), wg, wu, wd)


def _ff_tile_kernel(w_ref, o_ref, *, axis, size):
    w = w_ref[...]
    n_tiles = o_ref.shape[0]
    tile = o_ref.shape[1 + axis]
    pos = lax.broadcasted_iota(jnp.int32, w.shape, axis) + pl.program_id(0) * (n_tiles * tile)
    w = jnp.where(pos < size, w, 0.0).astype(o_ref.dtype)
    for t in range(n_tiles):
        o_ref[t] = w[:, t * tile:(t + 1) * tile] if axis == 1 else w[t * tile:(t + 1) * tile]


def _ff_tiles(w, layer, axis, tf=FF_TILE):
    size = w.shape[1 + axis]
    per_step = max(1, FF_STAGE // tf)
    nf = pl.cdiv(size, tf * per_step) * per_step
    tile = list(w.shape[1:])
    tile[axis] = tf
    stage = list(tile)
    stage[axis] = tf * per_step
    index = (lambda f: (layer, 0, f)) if axis == 1 else (lambda f: (layer, f, 0))
    return pl.pallas_call(
        functools.partial(_ff_tile_kernel, axis=axis, size=size),
        grid=(nf // per_step,),
        in_specs=[pl.BlockSpec((None,) + tuple(stage), index)],
        out_specs=pl.BlockSpec((per_step,) + tuple(tile), lambda f: (f, 0, 0)),
        out_shape=jax.ShapeDtypeStruct((nf,) + tuple(tile), BF16),
        compiler_params=_params(("parallel",), 6 * math.prod(stage) * 4),
        name="ff_tiles",
    )(w)


def _residue_perm(n_res, tl, transpose):
    tm = n_res * tl
    assert tl & (tl - 1) == 0
    a = lax.broadcasted_iota(jnp.int32, (tm, tm), 1 if transpose else 0)
    b = lax.broadcasted_iota(jnp.int32, (tm, tm), 0 if transpose else 1)
    hit = b == n_res * (a & (tl - 1)) + (a >> (tl.bit_length() - 1))
    return jnp.where(hit, 1.0, 0.0).astype(BF16)


def _in_proj_kernel(x_ref, g_ref, w_ref, o_ref):
    n_res, tl, _ = o_ref.shape
    sub = tl // IN_SPLIT
    rows = n_res * sub
    perm = _residue_perm(n_res, sub, False)
    for s in range(IN_SPLIT):
        xn = _rms(x_ref[s * rows:(s + 1) * rows, :], g_ref[...]).astype(BF16)
        xn = _dot(perm, xn).astype(BF16)
        res = _dot(xn, w_ref[...])
        for r in range(n_res):
            o_ref[r, s * sub:(s + 1) * sub, :] = res[r * sub:(r + 1) * sub]


def _in_proj(h, g, w, batch, *, tm=512):
    m, d = h.shape
    n = w.shape[1]
    seq = m // batch
    tm = min(tm, seq)
    tl = tm // TOKEN_RES
    tiles = seq // tm
    assert tl % (8 * IN_SPLIT) == 0 and seq % tm == 0
    vmem = (2 * tm * d * 4 + tm * d * 6 + d * n * 2 + 3 * tm * n * 4 + 4 * tm * tm) * 1.2
    return pl.pallas_call(
        _in_proj_kernel,
        grid=(m // tm,),
        in_specs=[
            pl.BlockSpec((tm, d), lambda i: (i, 0)),
            _const_spec((1, d)),
            _const_spec((d, n)),
        ],
        out_specs=pl.BlockSpec((None, TOKEN_RES, tl, n),
                               lambda i: (i // tiles, 0, i % tiles, 0)),
        out_shape=jax.ShapeDtypeStruct((batch, TOKEN_RES, seq // TOKEN_RES, n), F32),
        compiler_params=_params(("parallel",), vmem),
        name="in_proj",
    )(h, g.reshape(1, d), w)


def _attn_kernel(q_ref, k_ref, v_ref, o_ref, m_ref, l_ref, bias_ref, p_sc, mx_sc):
    n_res, length, _ = q_ref.shape
    blk = ATTN_BLOCK
    n_units = n_res * length // blk
    head0 = lax.broadcasted_iota(jnp.int32, (blk, LANES), 1) < HEAD_DIM
    kv_head0 = lax.broadcasted_iota(jnp.int32, (2 * blk, LANES), 1) < HEAD_DIM
    scale = HEAD_DIM ** -0.5 * math.log2(math.e)

    rho = lax.broadcasted_iota(jnp.int32, (blk, 2 * blk), 0)
    kap = lax.broadcasted_iota(jnp.int32, (blk, 2 * blk), 1)
    for pattern, d in enumerate(DILATIONS):
        n_sub = n_res // d
        c_rows = blk // n_sub
        sh = c_rows.bit_length() - 1
        iq, lq = rho >> sh, rho & (c_rows - 1)
        ik, lk = kap >> (sh + 1), kap & (2 * c_rows - 1)
        for shift in range(2):
            dist = n_sub * (lq - lk + shift * c_rows) + (iq - ik)
            bias_ref[2 * pattern + shift] = jnp.where(
                (dist >= 0) & (dist <= ATTN_SPAN), 0.0, MASK_VALUE)

    def probs(q, k, bias, sel):
        qh = jnp.where(sel, q, 0.0).astype(BF16)
        s = lax.dot_general(qh, k, (((1,), (1,)), ((), ())),
                            preferred_element_type=F32) + bias
        m = jnp.max(s, axis=-1, keepdims=True)
        return jnp.exp2(s - m).astype(BF16), m

    n_groups = n_units // ATTN_UNROLL
    for pattern, d in enumerate(DILATIONS):
        first = pattern == 0
        last = pattern == len(DILATIONS) - 1
        n_sub = n_res // d
        c_rows = blk // n_sub
        blocks_per_residue = n_units // d

        def gather(ref, res, start, size, d=d, n_sub=n_sub):
            return jnp.concatenate(
                [ref[d * i + res, pl.ds(start, size), :] for i in range(n_sub)], axis=0)

        def scatter(ref, res, start, val, d=d, n_sub=n_sub, c_rows=c_rows):
            for i in range(n_sub):
                ref[d * i + res, pl.ds(start, c_rows), :] = val[i * c_rows:(i + 1) * c_rows]

        def place(g, uu, c_rows=c_rows, blocks_per_residue=blocks_per_residue):
            u = g * ATTN_UNROLL + uu
            res = u // blocks_per_residue
            bq = u % blocks_per_residue
            kb = jnp.maximum(bq - 1, 0)
            return (res, pl.multiple_of(c_rows * bq, 8), pl.multiple_of(c_rows * kb, 8), bq - kb)

        def score_stage(g, slot, pattern=pattern, c_rows=c_rows, gather=gather, place=place):
            loaded = []
            for uu in range(ATTN_UNROLL):
                res, q0, k0, shift = place(g, uu)
                loaded.append((gather(q_ref, res, q0, c_rows) * scale,
                               gather(k_ref, res, k0, 2 * c_rows).astype(BF16),
                               bias_ref[2 * pattern + shift]))
            for uu, (q, k, bias) in enumerate(loaded):
                p0, m0 = probs(q, k, bias, head0)
                p1, m1 = probs(q, k, bias, jnp.logical_not(head0))
                p_sc[slot, uu, 0] = p0
                p_sc[slot, uu, 1] = p1
                mx_sc[slot, uu] = jnp.where(head0, m0, m1)

        def value_stage(g, slot, first=first, last=last, c_rows=c_rows,
                        gather=gather, scatter=scatter, place=place):
            loaded = []
            for uu in range(ATTN_UNROLL):
                res, q0, k0, _ = place(g, uu)
                old = None if first else tuple(
                    gather(ref, res, q0, c_rows) for ref in (o_ref, m_ref, l_ref))
                loaded.append((res, q0, gather(v_ref, res, k0, 2 * c_rows).astype(BF16),
                               p_sc[slot, uu, 0], p_sc[slot, uu, 1], mx_sc[slot, uu], old))
            results = []
            for res, q0, v, p0, p1, m_new, old in loaded:
                ones = jnp.ones_like(v)
                o0 = _dot(p0, jnp.where(kv_head0, v, ones))
                o1 = _dot(p1, jnp.where(kv_head0, ones, v))
                o_new = jnp.where(head0, o0, o1)
                l_new = pltpu.roll(jnp.where(head0, o1, o0), HEAD_DIM, 1)
                if not first:
                    o_old, m_old, l_old = old
                    m_tot = jnp.maximum(m_old, m_new)
                    w_old = jnp.exp2(m_old - m_tot)
                    w_new = jnp.exp2(m_new - m_tot)
                    o_new = w_old * o_old + w_new * o_new
                    l_new = w_old * l_old + w_new * l_new
                    m_new = m_tot
                results.append((res, q0, o_new, m_new, l_new))
            for res, q0, o_new, m_new, l_new in results:
                if last:
                    scatter(o_ref, res, q0, o_new / l_new)
                else:
                    scatter(o_ref, res, q0, o_new)
                    scatter(m_ref, res, q0, m_new)
                    scatter(l_ref, res, q0, l_new)

        def step(g, carry, score_stage=score_stage, value_stage=value_stage):
            slot = g & 1
            value_stage(g - 1, 1 - slot)
            score_stage(g, slot)
            return carry

        score_stage(0, 0)
        lax.fori_loop(1, n_groups, step, 0)
        value_stage(n_groups - 1, (n_groups - 1) & 1)


def _attention(z, n_pairs, *, q_col, k_col, v_col):
    b, n_res, length, _ = z.shape
    assert n_res == TOKEN_RES == DILATIONS[-1] and all(n_res % d == 0 for d in DILATIONS)
    assert length % ATTN_BLOCK == 0 and length >= 2 * ATTN_BLOCK and ATTN_BLOCK // n_res >= 8
    assert (n_res * length // ATTN_BLOCK) % ATTN_UNROLL == 0

    def spec(col):
        return pl.BlockSpec((None, n_res, length, LANES), lambda bi, h: (bi, 0, 0, col + h))

    slab = n_res * length * LANES * 4
    vmem = 10 * slab * 1.15 + (8 << 20)
    return pl.pallas_call(
        _attn_kernel,
        grid=(b, n_pairs),
        in_specs=[spec(q_col), spec(k_col), spec(v_col)],
        out_specs=spec(0),
        out_shape=jax.ShapeDtypeStruct((b, n_res, length, n_pairs * LANES), F32),
        scratch_shapes=[pltpu.VMEM((n_res, length, LANES), F32),
                        pltpu.VMEM((n_res, length, LANES), F32),
                        pltpu.VMEM((2 * len(DILATIONS), ATTN_BLOCK, 2 * ATTN_BLOCK), F32),
                        pltpu.VMEM((2, ATTN_UNROLL, 2, ATTN_BLOCK, 2 * ATTN_BLOCK), BF16),
                        pltpu.VMEM((2, ATTN_UNROLL, ATTN_BLOCK, LANES), F32)],
        compiler_params=_params(("parallel", "parallel"), vmem),
        name="attention",
    )(z, z, z)


def _ssm_weights(lam_re, lam_im, log_dt, b_re, b_im, c_re, c_im, d_skip, n_chunks):
    hp = lax.Precision.HIGHEST
    g, p = lam_re.shape
    c = b_re.shape[-1]
    t = SSM_CHUNK
    gb = GROUPS_PER_BLOCK
    nj = g // gb
    lr, li = lam_re.astype(F32), lam_im.astype(F32)
    dt = jnp.exp(log_dt.astype(F32))[:, None]
    mag = jnp.exp(lr * dt)
    ar = mag * jnp.cos(li * dt)
    ai = mag * jnp.sin(li * dt)
    nr, ni = ar - 1.0, ai
    den = lr * lr + li * li
    cr = (nr * lr + ni * li) / den
    ci = (ni * lr - nr * li) / den
    br, bi = b_re.astype(F32), b_im.astype(F32)
    bbr = cr[..., None] * br - ci[..., None] * bi
    bbi = cr[..., None] * bi + ci[..., None] * br

    def power(k):
        kk = k.astype(F32)[:, None, None]
        pm = jnp.exp(kk * (lr * dt))
        ang = kk * (li * dt)
        return pm * jnp.cos(ang), pm * jnp.sin(ang)

    pr, pi = power(jnp.arange(t + 1))
    cre, cim = c_re.astype(F32), c_im.astype(F32)
    car = cre[None] * pr[:, :, None, :] - cim[None] * pi[:, :, None, :]
    cai = cre[None] * pi[:, :, None, :] + cim[None] * pr[:, :, None, :]
    kern = jnp.einsum('kgdq,gqc->kgdc', jnp.concatenate([car[:t], -cai[:t]], axis=-1),
                      jnp.concatenate([bbr, bbi], axis=1), precision=hp)
    kern = kern.at[0].add(d_skip.astype(F32).reshape(g, c)[:, :, None] * jnp.eye(c, dtype=F32))

    kr = kern.reshape(t, nj, gb, c, c).transpose(1, 0, 4, 2, 3).reshape(nj, t, c, gb * c)

    rev_r, rev_i = pr[:t][::-1][:, :, None, :], pi[:t][::-1][:, :, None, :]
    bbr_t, bbi_t = bbr.transpose(0, 2, 1)[None], bbi.transpose(0, 2, 1)[None]
    abr = rev_r * bbr_t - rev_i * bbi_t
    abi = rev_r * bbi_t + rev_i * bbr_t

    def block_rows(x):
        return x.reshape(t, nj, gb, c, p).transpose(1, 0, 2, 3, 4).reshape(nj, t * gb * c, p)

    def rows_form(x):
        x = block_rows(x)
        return jnp.concatenate([x, x], axis=-1).astype(BF16)

    def cols_form(x):
        x = block_rows(x).transpose(0, 2, 1)
        return jnp.concatenate([x, x], axis=1).astype(BF16)

    n_steps = max(1, (n_chunks - 1).bit_length())
    sr, si = power(t * (2 ** jnp.arange(n_steps)))
    rows = -(-n_steps // 8) * 8
    sr = jnp.pad(sr.reshape(n_steps, nj, gb * p).transpose(1, 0, 2), ((0, 0), (0, rows - n_steps), (0, 0)))
    si = jnp.pad(si.reshape(n_steps, nj, gb * p).transpose(1, 0, 2), ((0, 0), (0, rows - n_steps), (0, 0)))
    return (kr.astype(BF16), rows_form(abr), rows_form(abi),
            cols_form(car[1:]), cols_form(-cai[1:]), sr, si)


def _ssm_expand(kr_ref, abr_ref, abi_ref, cmr_ref, cmi_ref, m_sc, b_sc, c_sc):
    t, c, gb, p = SSM_CHUNK, SSM_GROUP, GROUPS_PER_BLOCK, SSM_STATE
    shift = c.bit_length() - 1
    half = gb * p
    pairs = LANES // p

    def group_of(idx):
        return (idx >> shift) & (gb - 1)

    m_sc[...] = jnp.zeros_like(m_sc)
    col_g = group_of(lax.broadcasted_iota(jnp.int32, (c, LANES), 1))
    for k in range(t):
        kk = kr_ref[k]
        dk = jnp.concatenate([jnp.where(col_g == g, kk, jnp.zeros_like(kk)) for g in range(gb)], axis=0)
        for tau in range(t - k):
            m_sc[tau * LANES:(tau + 1) * LANES, (tau + k) * LANES:(tau + k + 1) * LANES] = dk

    row_g = group_of(lax.broadcasted_iota(jnp.int32, (t * LANES, LANES), 0))
    lane_half = lax.broadcasted_iota(jnp.int32, (t * LANES, LANES), 1) >> (p.bit_length() - 1)
    for part, ab_ref in enumerate((abr_ref, abi_ref)):
        ab = ab_ref[...]
        for q in range(gb // pairs):
            b_sc[:, part * half + q * LANES:part * half + (q + 1) * LANES] = jnp.where(
                row_g == pairs * q + lane_half, ab, jnp.zeros_like(ab))

    col_g2 = group_of(lax.broadcasted_iota(jnp.int32, (LANES, t * LANES), 1))
    row_half = lax.broadcasted_iota(jnp.int32, (LANES, t * LANES), 0) >> (p.bit_length() - 1)
    for part, cm_ref in enumerate((cmr_ref, cmi_ref)):
        cm = cm_ref[...]
        for q in range(gb // pairs):
            c_sc[part * half + q * LANES:part * half + (q + 1) * LANES, :] = jnp.where(
                col_g2 == pairs * q + row_half, cm, jnp.zeros_like(cm))


def _ssm_kernel(s_ref, kr_ref, abr_ref, abi_ref, cmr_ref, cmi_ref, sr_ref, si_ref, y_ref,
                m_sc, b_sc, c_sc):
    t, n_chunks, _ = s_ref.shape
    half = GROUPS_PER_BLOCK * SSM_STATE

    @pl.when(pl.program_id(1) == 0)
    def _():
        _ssm_expand(kr_ref, abr_ref, abi_ref, cmr_ref, cmi_ref, m_sc, b_sc, c_sc)

    u = jnp.concatenate([s_ref[tau].astype(BF16) for tau in range(t)], axis=1)
    y = _dot(u, m_sc[...])

    width = half // SSM_SPLIT
    row = lax.broadcasted_iota(jnp.int32, (n_chunks, width), 0)

    def shifted(h, k):
        return jnp.where(row >= k, pltpu.roll(h, k, 0), 0.0)

    for part in range(SSM_SPLIT):
        re = slice(part * width, (part + 1) * width)
        im = slice(half + part * width, half + (part + 1) * width)
        hr = _dot(u, b_sc[:, re])
        hi = _dot(u, b_sc[:, im])
        step = 0
        while (1 << step) < n_chunks:
            k = 1 << step
            pr = sr_ref[step:step + 1, re]
            pi = si_ref[step:step + 1, re]
            zr, zi = shifted(hr, k), shifted(hi, k)
            hr, hi = hr + pr * zr - pi * zi, hi + pr * zi + pi * zr
            step += 1
        y = (y + _dot(shifted(hr, 1).astype(BF16), c_sc[re, :])
             + _dot(shifted(hi, 1).astype(BF16), c_sc[im, :]))

    y = jax.nn.gelu(y)
    for tau in range(t):
        y_ref[tau] = y[:, tau * LANES:(tau + 1) * LANES]


def _ssm(z, s_col, weights):
    b, n_res, n_chunks, _ = z.shape
    nj = weights[0].shape[0]
    assert n_res == SSM_CHUNK
    assert SSM_GROUP & (SSM_GROUP - 1) == 0 and SSM_STATE & (SSM_STATE - 1) == 0

    def wspec(a):
        nd = a.ndim - 1
        return pl.BlockSpec((None,) + a.shape[1:], lambda j, bi: (j,) + (0,) * nd)

    s = n_res * n_chunks
    width = SSM_CHUNK * LANES
    state = 2 * GROUPS_PER_BLOCK * SSM_STATE
    scratch = (width * width + 2 * width * state) * 2
    wbytes = 2 * sum(math.prod(a.shape[1:]) * a.dtype.itemsize for a in weights)
    vmem = (4 * s * LANES * 4 + wbytes + scratch + n_chunks * (width * 10 + state * 12)) * 1.2
    return pl.pallas_call(
        _ssm_kernel,
        grid=(nj, b),
        in_specs=[pl.BlockSpec((None, n_res, n_chunks, LANES),
                               lambda j, bi: (bi, 0, 0, s_col + j))]
        + [wspec(a) for a in weights],
        out_specs=pl.BlockSpec((None, n_res, n_chunks, LANES), lambda j, bi: (bi, 0, 0, j)),
        out_shape=jax.ShapeDtypeStruct((b, n_res, n_chunks, nj * LANES), F32),
        scratch_shapes=[pltpu.VMEM((width, width), BF16), pltpu.VMEM((width, state), BF16),
                        pltpu.VMEM((state, width), BF16)],
        compiler_params=_params(("parallel", "arbitrary"), vmem),
        name="ssm",
    )(z, *weights)


def _mix_out_kernel(ya_ref, y_ref, h_ref, ga_ref, gb_ref, wglu_ref, bglu_ref,
                    wa_ref, wb_ref, o_ref):
    n_res, tl, _ = ya_ref.shape
    sub = tl // MIX_SPLIT
    rows = n_res * sub
    perm = _residue_perm(n_res, sub, True)
    for s in range(MIX_SPLIT):
        ls = slice(s * sub, (s + 1) * sub)
        y = y_ref[:, ls, :].reshape(rows, -1)
        gate = _dot(y.astype(BF16), wglu_ref[...]) + bglu_ref[...]
        yb = y * jax.nn.sigmoid(gate)
        na = _rms(ya_ref[:, ls, :].reshape(rows, -1), ga_ref[...]).astype(BF16)
        nb = _rms(yb, gb_ref[...]).astype(BF16)
        na = _dot(perm, na).astype(BF16)
        nb = _dot(perm, nb).astype(BF16)
        tok = slice(s * rows, (s + 1) * rows)
        o_ref[tok, :] = h_ref[tok, :] + _dot(na, wa_ref[...]) + _dot(nb, wb_ref[...])


def _mix_out(ya, y, h, ga, gb, wglu, bglu, wa, wb, *, tm=512):
    m, d = h.shape
    batch, n_res, length, wa_w = ya.shape
    wb_w = y.shape[-1]
    tm = min(tm, n_res * length)
    tl = tm // n_res
    tiles = length // tl
    assert tl % (8 * MIX_SPLIT) == 0 and length % tl == 0
    vmem = (2 * tm * (wa_w + wb_w + 2 * d) * 4 + (wglu.size + wa.size + wb.size) * 2
            + 8 * tm * d * 4) * 1.2
    row = pl.BlockSpec((tm, d), lambda i: (i, 0))
    res_major = lambda w: pl.BlockSpec((None, n_res, tl, w),
                                       lambda i: (i // tiles, 0, i % tiles, 0))
    return pl.pallas_call(
        _mix_out_kernel,
        grid=(m // tm,),
        in_specs=[res_major(wa_w), res_major(wb_w), row,
                  _const_spec((1, wa_w)), _const_spec((1, wb_w)),
                  _const_spec(wglu.shape), _const_spec((1, wb_w)),
                  _const_spec(wa.shape), _const_spec(wb.shape)],
        out_specs=row,
        out_shape=jax.ShapeDtypeStruct((m, d), F32),
        compiler_params=_params(("parallel",), vmem),
        name="mix_out",
    )(ya, y, h, ga.reshape(1, -1), gb.reshape(1, -1), wglu, bglu.reshape(1, -1), wa, wb)


def _ple_kernel(h_ref, p_ref, gn_ref, wg_ref, wp_ref, fn_ref, o_ref, *, final):
    h = h_ref[...]
    gate = jax.nn.sigmoid(_dot(_rms(h, gn_ref[...]).astype(BF16), wg_ref[...]))
    proj = _dot(p_ref[...].astype(BF16), wp_ref[...])
    out = h + gate * proj
    o_ref[...] = _rms(out, fn_ref[...]) if final else out


def _ple(h, p, layer, gn, wg, wp, fn, *, final, tm=512):
    m, d = h.shape
    pd = p.shape[2]
    tm = min(tm, m)
    vmem = (2 * tm * (2 * d + pd) * 4 + (wg.size + wp.size) * 2 + 8 * tm * d * 4) * 1.2
    row = lambda w: pl.BlockSpec((tm, w), lambda i: (i, 0))
    return pl.pallas_call(
        functools.partial(_ple_kernel, final=final),
        grid=(m // tm,),
        in_specs=[row(d), pl.BlockSpec((None, tm, pd), lambda i: (layer, i, 0)),
                  _const_spec((1, d)), _const_spec(wg.shape),
                  _const_spec(wp.shape), _const_spec((1, d))],
        out_specs=row(d),
        out_shape=jax.ShapeDtypeStruct((m, d), F32),
        compiler_params=_params(("parallel",), vmem),
        name="ple",
    )(h, p, gn.reshape(1, d), wg, wp, fn.reshape(1, d))


def kernel(x, p, ffn1_norm, ffn1_w_gate, ffn1_w_up, ffn1_w_down, mix_norm, w_in, attn_out_norm, ssm_lambda_re, ssm_lambda_im, ssm_log_dt, ssm_b_re, ssm_b_im, ssm_c_re, ssm_c_im, ssm_d, ssm_w_glu, ssm_b_glu, ssm_out_norm, w_out, ffn2_norm, ffn2_w_gate, ffn2_w_up, ffn2_w_down, ple_norm, ple_w_gate, ple_w_proj, final_norm):
    b, s, d = x.shape
    depth = p.shape[0]
    attn_w = attn_out_norm.shape[1]
    ssm_w = ssm_out_norm.shape[1]
    n_pairs = attn_w // LANES
    h = x.reshape(b * s, d)
    for i in range(depth):
        h = _ffn(h, ffn1_norm[i], _ff_tiles(ffn1_w_gate, i, 1), _ff_tiles(ffn1_w_up, i, 1),
                 _ff_tiles(ffn1_w_down, i, 0))
        z = _in_proj(h, mix_norm[i], w_in[i].astype(BF16), b)
        ya = _attention(z, n_pairs, q_col=0, k_col=n_pairs, v_col=2 * n_pairs)
        weights = _ssm_weights(ssm_lambda_re[i], ssm_lambda_im[i], ssm_log_dt[i],
                               ssm_b_re[i], ssm_b_im[i], ssm_c_re[i], ssm_c_im[i],
                               ssm_d[i], s // SSM_CHUNK)
        y = _ssm(z, 3 * n_pairs, weights)
        wo = w_out[i].astype(BF16)
        h = _mix_out(ya, y, h, attn_out_norm[i], ssm_out_norm[i], ssm_w_glu[i].astype(BF16),
                     ssm_b_glu[i], wo[:attn_w], wo[attn_w:])
        h = _ffn(h, ffn2_norm[i], _ff_tiles(ffn2_w_gate, i, 1), _ff_tiles(ffn2_w_up, i, 1),
                 _ff_tiles(ffn2_w_down, i, 0))
        h = _ple(h, p.reshape(depth, b * s, -1), i, ple_norm[i], ple_w_gate[i].astype(BF16),
                 ple_w_proj[i].astype(BF16), final_norm, final=i == depth - 1)
    return h.reshape(b, s, d)
```

```python
import functools
import math

import jax
import jax.numpy as jnp
from jax import lax
from jax.experimental import pallas as pl
from jax.experimental.pallas import tpu as pltpu

NORM_EPS = 1e-6
MASK_VALUE = -1e30
HEAD_DIM = 64
ATTN_BLOCK = 128
ATTN_SPAN = 128
DILATIONS = (1, 4, 16)
ATTN_UNROLL = 16
SSM_GROUP = 16
SSM_STATE = 64
SSM_CHUNK = 16
TOKEN_RES = 16
LANES = 128
GROUPS_PER_BLOCK = LANES // SSM_GROUP
FF_TILE = 256
FF_STAGE = 512
MIX_SPLIT = 2
IN_SPLIT = 2
VMEM_CAP = 56 * 1024 * 1024

BF16 = jnp.bfloat16
F32 = jnp.float32


def _params(sem, vmem_bytes):
    return pltpu.CompilerParams(
        dimension_semantics=sem, vmem_limit_bytes=min(int(vmem_bytes), VMEM_CAP))


def _rms(x, g):
    ms = jnp.mean(x * x, axis=-1, keepdims=True)
    return x * lax.rsqrt(ms + NORM_EPS) * g


def _dot(a, b):
    return jnp.dot(a, b, preferred_element_type=F32)


def _sigmoid(x):
    return 0.5 * jnp.tanh(0.5 * x) + 0.5


def _const_spec(shape):
    return pl.BlockSpec(shape, lambda *_: (0,) * len(shape),
                        pipeline_mode=pl.Buffered(1))


def _ffn_kernel(x_ref, g_ref, wg_ref, wu_ref, wd_ref, o_ref, xn_ref):
    f = pl.program_id(1)

    @pl.when(f == 0)
    def _():
        xn_ref[...] = _rms(x_ref[...], g_ref[...]).astype(BF16)
        o_ref[...] = jnp.zeros_like(o_ref)

    xn = xn_ref[...]
    gate = _dot(xn, wg_ref[...])
    up = _dot(xn, wu_ref[...])
    act = (gate * jax.nn.sigmoid(gate) * up).astype(BF16)
    o_ref[...] += _dot(act, wd_ref[...])

    @pl.when(f == pl.num_programs(1) - 1)
    def _():
        o_ref[...] = x_ref[...] + 0.5 * o_ref[...]


def _ffn(h, g, wg, wu, wd, *, tm=1024):
    m, d = h.shape
    nf, _, tf = wg.shape
    tm = min(tm, m)
    vmem = (4 * tm * d * 4 + tm * d * 2 + 6 * d * tf * 2 + 6 * tm * tf * 4) * 1.25
    row = pl.BlockSpec((tm, d), lambda i, f: (i, 0))
    return pl.pallas_call(
        _ffn_kernel,
        grid=(m // tm, nf),
        in_specs=[
            row,
            pl.BlockSpec((1, d), lambda i, f: (0, 0)),
            pl.BlockSpec((None, d, tf), lambda i, f: (f, 0, 0)),
            pl.BlockSpec((None, d, tf), lambda i, f: (f, 0, 0)),
            pl.BlockSpec((None, tf, d), lambda i, f: (f, 0, 0)),
        ],
        out_specs=row,
        out_shape=jax.ShapeDtypeStruct((m, d), F32),
        scratch_shapes=[pltpu.VMEM((tm, d), BF16)],
        compiler_params=_params(("parallel", "arbitrary"), vmem),
        name="ffn",
    )(h, g.reshape(1, d), wg, wu, wd)


def _ff_tile_kernel(w_ref, o_ref, *, axis, size):
    w = w_ref[...]
    n_tiles = o_ref.shape[0]
    tile = o_ref.shape[1 + axis]
    pos = lax.broadcasted_iota(jnp.int32, w.shape, axis) + pl.program_id(0) * (n_tiles * tile)
    w = jnp.where(pos < size, w, 0.0).astype(o_ref.dtype)
    for t in range(n_tiles):
        o_ref[t] = w[:, t * tile:(t + 1) * tile] if axis == 1 else w[t * tile:(t + 1) * tile]


def _ff_tiles(w, layer, axis, tf=FF_TILE):
    size = w.shape[1 + axis]
    per_step = max(1, FF_STAGE // tf)
    nf = pl.cdiv(size, tf * per_step) * per_step
    tile = list(w.shape[1:])
    tile[axis] = tf
    stage = list(tile)
    stage[axis] = tf * per_step
    index = (lambda f: (layer, 0, f)) if axis == 1 else (lambda f: (layer, f, 0))
    return pl.pallas_call(
        functools.partial(_ff_tile_kernel, axis=axis, size=size),
        grid=(nf // per_step,),
        in_specs=[pl.BlockSpec((None,) + tuple(stage), index)],
        out_specs=pl.BlockSpec((per_step,) + tuple(tile), lambda f: (f, 0, 0)),
        out_shape=jax.ShapeDtypeStruct((nf,) + tuple(tile), BF16),
        compiler_params=_params(("parallel",), 6 * math.prod(stage) * 4),
        name="ff_tiles",
    )(w)


def _residue_perm(n_res, tl, transpose):
    tm = n_res * tl
    assert tl & (tl - 1) == 0
    a = lax.broadcasted_iota(jnp.int32, (tm, tm), 1 if transpose else 0)
    b = lax.broadcasted_iota(jnp.int32, (tm, tm), 0 if transpose else 1)
    hit = b == n_res * (a & (tl - 1)) + (a >> (tl.bit_length() - 1))
    return jnp.where(hit, 1.0, 0.0).astype(BF16)


def _in_proj_kernel(x_ref, g_ref, w_ref, o_ref):
    n_res, tl, _ = o_ref.shape
    sub = tl // IN_SPLIT
    rows = n_res * sub
    perm = _residue_perm(n_res, sub, False)
    for s in range(IN_SPLIT):
        xn = _rms(x_ref[s * rows:(s + 1) * rows, :], g_ref[...]).astype(BF16)
        xn = _dot(perm, xn).astype(BF16)
        res = _dot(xn, w_ref[...])
        for r in range(n_res):
            o_ref[r, s * sub:(s + 1) * sub, :] = res[r * sub:(r + 1) * sub]


def _in_proj(h, g, w, batch, *, tm=512):
    m, d = h.shape
    n = w.shape[1]
    seq = m // batch
    tm = min(tm, seq)
    tl = tm // TOKEN_RES
    tiles = seq // tm
    assert tl % (8 * IN_SPLIT) == 0 and seq % tm == 0
    vmem = (2 * tm * d * 4 + tm * d * 6 + d * n * 2 + 3 * tm * n * 4 + 4 * tm * tm) * 1.2
    return pl.pallas_call(
        _in_proj_kernel,
        grid=(m // tm,),
        in_specs=[
            pl.BlockSpec((tm, d), lambda i: (i, 0)),
            _const_spec((1, d)),
            _const_spec((d, n)),
        ],
        out_specs=pl.BlockSpec((None, TOKEN_RES, tl, n),
                               lambda i: (i // tiles, 0, i % tiles, 0)),
        out_shape=jax.ShapeDtypeStruct((batch, TOKEN_RES, seq // TOKEN_RES, n), F32),
        compiler_params=_params(("parallel",), vmem),
        name="in_proj",
    )(h, g.reshape(1, d), w)


def _attn_kernel(q_ref, k_ref, v_ref, o_ref, m_ref, l_ref, bias_ref, p_sc, mx_sc):
    n_res, length, _ = q_ref.shape
    blk = ATTN_BLOCK
    n_units = n_res * length // blk
    head0 = lax.broadcasted_iota(jnp.int32, (blk, LANES), 1) < HEAD_DIM
    kv_head0 = lax.broadcasted_iota(jnp.int32, (2 * blk, LANES), 1) < HEAD_DIM
    scale = HEAD_DIM ** -0.5 * math.log2(math.e)

    rho = lax.broadcasted_iota(jnp.int32, (blk, 2 * blk), 0)
    kap = lax.broadcasted_iota(jnp.int32, (blk, 2 * blk), 1)
    for pattern, d in enumerate(DILATIONS):
        n_sub = n_res // d
        c_rows = blk // n_sub
        sh = c_rows.bit_length() - 1
        iq, lq = rho >> sh, rho & (c_rows - 1)
        ik, lk = kap >> (sh + 1), kap & (2 * c_rows - 1)
        for shift in range(2):
            dist = n_sub * (lq - lk + shift * c_rows) + (iq - ik)
            bias_ref[2 * pattern + shift] = jnp.where(
                (dist >= 0) & (dist <= ATTN_SPAN), 0.0, MASK_VALUE)

    def probs(q, k, bias, sel):
        qh = jnp.where(sel, q, 0.0).astype(BF16)
        s = lax.dot_general(qh, k, (((1,), (1,)), ((), ())),
                            preferred_element_type=F32) + bias
        m = jnp.max(s, axis=-1, keepdims=True)
        return jnp.exp2(s - m).astype(BF16), m

    n_groups = n_units // ATTN_UNROLL
    for pattern, d in enumerate(DILATIONS):
        first = pattern == 0
        last = pattern == len(DILATIONS) - 1
        n_sub = n_res // d
        c_rows = blk // n_sub
        blocks_per_residue = n_units // d

        def gather(ref, res, start, size, d=d, n_sub=n_sub):
            return jnp.concatenate(
                [ref[d * i + res, pl.ds(start, size), :] for i in range(n_sub)], axis=0)

        def scatter(ref, res, start, val, d=d, n_sub=n_sub, c_rows=c_rows):
            for i in range(n_sub):
                ref[d * i + res, pl.ds(start, c_rows), :] = val[i * c_rows:(i + 1) * c_rows]

        def place(g, uu, c_rows=c_rows, blocks_per_residue=blocks_per_residue):
            u = g * ATTN_UNROLL + uu
            res = u // blocks_per_residue
            bq = u % blocks_per_residue
            kb = jnp.maximum(bq - 1, 0)
            return (res, pl.multiple_of(c_rows * bq, 8), pl.multiple_of(c_rows * kb, 8), bq - kb)

        def score_stage(g, slot, pattern=pattern, c_rows=c_rows, gather=gather, place=place):
            loaded = []
            for uu in range(ATTN_UNROLL):
                res, q0, k0, shift = place(g, uu)
                loaded.append((gather(q_ref, res, q0, c_rows) * scale,
                               gather(k_ref, res, k0, 2 * c_rows).astype(BF16),
                               bias_ref[2 * pattern + shift]))
            for uu, (q, k, bias) in enumerate(loaded):
                p0, m0 = probs(q, k, bias, head0)
                p1, m1 = probs(q, k, bias, jnp.logical_not(head0))
                p_sc[slot, uu, 0] = p0
                p_sc[slot, uu, 1] = p1
                mx_sc[slot, uu] = jnp.where(head0, m0, m1)

        def value_stage(g, slot, first=first, last=last, c_rows=c_rows,
                        gather=gather, scatter=scatter, place=place):
            loaded = []
            for uu in range(ATTN_UNROLL):
                res, q0, k0, _ = place(g, uu)
                old = None if first else tuple(
                    gather(ref, res, q0, c_rows) for ref in (o_ref, m_ref, l_ref))
                loaded.append((res, q0, gather(v_ref, res, k0, 2 * c_rows).astype(BF16),
                               p_sc[slot, uu, 0], p_sc[slot, uu, 1], mx_sc[slot, uu], old))
            results = []
            for res, q0, v, p0, p1, m_new, old in loaded:
                ones = jnp.ones_like(v)
                o0 = _dot(p0, jnp.where(kv_head0, v, ones))
                o1 = _dot(p1, jnp.where(kv_head0, ones, v))
                o_new = jnp.where(head0, o0, o1)
                l_new = pltpu.roll(jnp.where(head0, o1, o0), HEAD_DIM, 1)
                if not first:
                    o_old, m_old, l_old = old
                    m_tot = jnp.maximum(m_old, m_new)
                    w_old = jnp.exp2(m_old - m_tot)
                    w_new = jnp.exp2(m_new - m_tot)
                    o_new = w_old * o_old + w_new * o_new
                    l_new = w_old * l_old + w_new * l_new
                    m_new = m_tot
                results.append((res, q0, o_new, m_new, l_new))
            for res, q0, o_new, m_new, l_new in results:
                if last:
                    scatter(o_ref, res, q0, o_new / l_new)
                else:
                    scatter(o_ref, res, q0, o_new)
                    scatter(m_ref, res, q0, m_new)
                    scatter(l_ref, res, q0, l_new)

        def step(g, carry, score_stage=score_stage, value_stage=value_stage):
            slot = g & 1
            value_stage(g - 1, 1 - slot)
            score_stage(g, slot)
            return carry

        score_stage(0, 0)
        lax.fori_loop(1, n_groups, step, 0)
        value_stage(n_groups - 1, (n_groups - 1) & 1)


def _attention(z, n_pairs, *, q_col, k_col, v_col):
    b, n_res, length, _ = z.shape
    assert n_res == TOKEN_RES == DILATIONS[-1] and all(n_res % d == 0 for d in DILATIONS)
    assert length % ATTN_BLOCK == 0 and length >= 2 * ATTN_BLOCK and ATTN_BLOCK // n_res >= 8
    assert (n_res * length // ATTN_BLOCK) % ATTN_UNROLL == 0

    def spec(col):
        return pl.BlockSpec((None, n_res, length, LANES), lambda bi, h: (bi, 0, 0, col + h))

    slab = n_res * length * LANES * 4
    vmem = 10 * slab * 1.15 + (8 << 20)
    return pl.pallas_call(
        _attn_kernel,
        grid=(b, n_pairs),
        in_specs=[spec(q_col), spec(k_col), spec(v_col)],
        out_specs=spec(0),
        out_shape=jax.ShapeDtypeStruct((b, n_res, length, n_pairs * LANES), F32),
        scratch_shapes=[pltpu.VMEM((n_res, length, LANES), F32),
                        pltpu.VMEM((n_res, length, LANES), F32),
                        pltpu.VMEM((2 * len(DILATIONS), ATTN_BLOCK, 2 * ATTN_BLOCK), F32),
                        pltpu.VMEM((2, ATTN_UNROLL, 2, ATTN_BLOCK, 2 * ATTN_BLOCK), BF16),
                        pltpu.VMEM((2, ATTN_UNROLL, ATTN_BLOCK, LANES), F32)],
        compiler_params=_params(("parallel", "parallel"), vmem),
        name="attention",
    )(z, z, z)


def _ssm_weights(lam_re, lam_im, log_dt, b_re, b_im, c_re, c_im, d_skip, n_chunks):
    hp = lax.Precision.HIGHEST
    g, p = lam_re.shape
    c = b_re.shape[-1]
    t = SSM_CHUNK
    gb = GROUPS_PER_BLOCK
    nj = g // gb
    lr, li = lam_re.astype(F32), lam_im.astype(F32)
    dt = jnp.exp(log_dt.astype(F32))[:, None]
    mag = jnp.exp(lr * dt)
    ar = mag * jnp.cos(li * dt)
    ai = mag * jnp.sin(li * dt)
    nr, ni = ar - 1.0, ai
    den = lr * lr + li * li
    cr = (nr * lr + ni * li) / den
    ci = (ni * lr - nr * li) / den
    br, bi = b_re.astype(F32), b_im.astype(F32)
    bbr = cr[..., None] * br - ci[..., None] * bi
    bbi = cr[..., None] * bi + ci[..., None] * br

    def power(k):
        kk = k.astype(F32)[:, None, None]
        pm = jnp.exp(kk * (lr * dt))
        ang = kk * (li * dt)
        return pm * jnp.cos(ang), pm * jnp.sin(ang)

    pr, pi = power(jnp.arange(t + 1))
    cre, cim = c_re.astype(F32), c_im.astype(F32)
    car = cre[None] * pr[:, :, None, :] - cim[None] * pi[:, :, None, :]
    cai = cre[None] * pi[:, :, None, :] + cim[None] * pr[:, :, None, :]
    kern = jnp.einsum('kgdq,gqc->kgdc', jnp.concatenate([car[:t], -cai[:t]], axis=-1),
                      jnp.concatenate([bbr, bbi], axis=1), precision=hp)
    kern = kern.at[0].add(d_skip.astype(F32).reshape(g, c)[:, :, None] * jnp.eye(c, dtype=F32))

    kr = kern.reshape(t, nj, gb, c, c).transpose(1, 0, 4, 2, 3).reshape(nj, t, c, gb * c)

    rev_r, rev_i = pr[:t][::-1][:, :, None, :], pi[:t][::-1][:, :, None, :]
    bbr_t, bbi_t = bbr.transpose(0, 2, 1)[None], bbi.transpose(0, 2, 1)[None]
    abr = rev_r * bbr_t - rev_i * bbi_t
    abi = rev_r * bbi_t + rev_i * bbr_t

    def block_rows(x):
        return x.reshape(t, nj, gb, c, p).transpose(1, 0, 2, 3, 4).reshape(nj, t * gb * c, p)

    def rows_form(x):
        x = block_rows(x)
        return jnp.concatenate([x, x], axis=-1).astype(BF16)

    def cols_form(x):
        x = block_rows(x).transpose(0, 2, 1)
        return jnp.concatenate([x, x], axis=1).astype(BF16)

    n_steps = max(1, (n_chunks - 1).bit_length())
    sr, si = power(t * (2 ** jnp.arange(n_steps)))
    rows = -(-n_steps // 8) * 8
    sr = jnp.pad(sr.reshape(n_steps, nj, gb * p).transpose(1, 0, 2), ((0, 0), (0, rows - n_steps), (0, 0)))
    si = jnp.pad(si.reshape(n_steps, nj, gb * p).transpose(1, 0, 2), ((0, 0), (0, rows - n_steps), (0, 0)))
    return (kr.astype(BF16), rows_form(abr), rows_form(abi),
            cols_form(car[1:]), cols_form(-cai[1:]), sr, si)


def _ssm_expand(kr_ref, abr_ref, abi_ref, cmr_ref, cmi_ref, m_sc, b_sc, c_sc):
    t, c, gb, p = SSM_CHUNK, SSM_GROUP, GROUPS_PER_BLOCK, SSM_STATE
    shift = c.bit_length() - 1
    half = gb * p
    pairs = LANES // p

    def group_of(idx):
        return (idx >> shift) & (gb - 1)

    m_sc[...] = jnp.zeros_like(m_sc)
    col_g = group_of(lax.broadcasted_iota(jnp.int32, (c, LANES), 1))
    for k in range(t):
        kk = kr_ref[k]
        dk = jnp.concatenate([jnp.where(col_g == g, kk, jnp.zeros_like(kk)) for g in range(gb)], axis=0)
        for tau in range(t - k):
            m_sc[tau * LANES:(tau + 1) * LANES, (tau + k) * LANES:(tau + k + 1) * LANES] = dk

    row_g = group_of(lax.broadcasted_iota(jnp.int32, (t * LANES, LANES), 0))
    lane_half = lax.broadcasted_iota(jnp.int32, (t * LANES, LANES), 1) >> (p.bit_length() - 1)
    for part, ab_ref in enumerate((abr_ref, abi_ref)):
        ab = ab_ref[...]
        for q in range(gb // pairs):
            b_sc[:, part * half + q * LANES:part * half + (q + 1) * LANES] = jnp.where(
                row_g == pairs * q + lane_half, ab, jnp.zeros_like(ab))

    col_g2 = group_of(lax.broadcasted_iota(jnp.int32, (LANES, t * LANES), 1))
    row_half = lax.broadcasted_iota(jnp.int32, (LANES, t * LANES), 0) >> (p.bit_length() - 1)
    for part, cm_ref in enumerate((cmr_ref, cmi_ref)):
        cm = cm_ref[...]
        for q in range(gb // pairs):
            c_sc[part * half + q * LANES:part * half + (q + 1) * LANES, :] = jnp.where(
                col_g2 == pairs * q + row_half, cm, jnp.zeros_like(cm))


def _ssm_kernel(s_ref, kr_ref, abr_ref, abi_ref, cmr_ref, cmi_ref, sr_ref, si_ref, y_ref,
                m_sc, b_sc, c_sc):
    t, n_chunks, _ = s_ref.shape
    half = GROUPS_PER_BLOCK * SSM_STATE

    @pl.when(pl.program_id(1) == 0)
    def _():
        _ssm_expand(kr_ref, abr_ref, abi_ref, cmr_ref, cmi_ref, m_sc, b_sc, c_sc)

    u = jnp.concatenate([s_ref[tau].astype(BF16) for tau in range(t)], axis=1)
    y = _dot(u, m_sc[...])
    x = _dot(u, b_sc[...])
    hr, hi = x[:, :half], x[:, half:]

    row = lax.broadcasted_iota(jnp.int32, (n_chunks, half), 0)

    def shifted(h, k):
        return jnp.where(row >= k, pltpu.roll(h, k, 0), 0.0)

    step = 0
    while (1 << step) < n_chunks:
        k = 1 << step
        pr = sr_ref[step:step + 1, :]
        pi = si_ref[step:step + 1, :]
        zr, zi = shifted(hr, k), shifted(hi, k)
        hr, hi = hr + pr * zr - pi * zi, hi + pr * zi + pi * zr
        step += 1

    h_prev = jnp.concatenate([shifted(hr, 1), shifted(hi, 1)], axis=1).astype(BF16)
    y = jax.nn.gelu(y + _dot(h_prev, c_sc[...]))
    for tau in range(t):
        y_ref[tau] = y[:, tau * LANES:(tau + 1) * LANES]


def _ssm(z, s_col, weights):
    b, n_res, n_chunks, _ = z.shape
    nj = weights[0].shape[0]
    assert n_res == SSM_CHUNK
    assert SSM_GROUP & (SSM_GROUP - 1) == 0 and SSM_STATE & (SSM_STATE - 1) == 0

    def wspec(a):
        nd = a.ndim - 1
        return pl.BlockSpec((None,) + a.shape[1:], lambda j, bi: (j,) + (0,) * nd)

    s = n_res * n_chunks
    width = SSM_CHUNK * LANES
    state = 2 * GROUPS_PER_BLOCK * SSM_STATE
    scratch = (width * width + 2 * width * state) * 2
    wbytes = 2 * sum(math.prod(a.shape[1:]) * a.dtype.itemsize for a in weights)
    vmem = (4 * s * LANES * 4 + wbytes + scratch + n_chunks * (width * 10 + state * 12)) * 1.2
    return pl.pallas_call(
        _ssm_kernel,
        grid=(nj, b),
        in_specs=[pl.BlockSpec((None, n_res, n_chunks, LANES),
                               lambda j, bi: (bi, 0, 0, s_col + j))]
        + [wspec(a) for a in weights],
        out_specs=pl.BlockSpec((None, n_res, n_chunks, LANES), lambda j, bi: (bi, 0, 0, j)),
        out_shape=jax.ShapeDtypeStruct((b, n_res, n_chunks, nj * LANES), F32),
        scratch_shapes=[pltpu.VMEM((width, width), BF16), pltpu.VMEM((width, state), BF16),
                        pltpu.VMEM((state, width), BF16)],
        compiler_params=_params(("parallel", "arbitrary"), vmem),
        name="ssm",
    )(z, *weights)


def _mix_out_kernel(ya_ref, y_ref, h_ref, ga_ref, gb_ref, wglu_ref, bglu_ref,
                    wa_ref, wb_ref, o_ref):
    n_res, tl, _ = ya_ref.shape
    sub = tl // MIX_SPLIT
    rows = n_res * sub
    perm = _residue_perm(n_res, sub, True)
    for s in range(MIX_SPLIT):
        ls = slice(s * sub, (s + 1) * sub)
        y = y_ref[:, ls, :].reshape(rows, -1)
        gate = _dot(y.astype(BF16), wglu_ref[...]) + bglu_ref[...]
        yb = y * _sigmoid(gate)
        na = _rms(ya_ref[:, ls, :].reshape(rows, -1), ga_ref[...]).astype(BF16)
        nb = _rms(yb, gb_ref[...]).astype(BF16)
        na = _dot(perm, na).astype(BF16)
        nb = _dot(perm, nb).astype(BF16)
        tok = slice(s * rows, (s + 1) * rows)
        o_ref[tok, :] = h_ref[tok, :] + _dot(na, wa_ref[...]) + _dot(nb, wb_ref[...])


def _mix_out(ya, y, h, ga, gb, wglu, bglu, wa, wb, *, tm=512):
    m, d = h.shape
    batch, n_res, length, wa_w = ya.shape
    wb_w = y.shape[-1]
    tm = min(tm, n_res * length)
    tl = tm // n_res
    tiles = length // tl
    assert tl % (8 * MIX_SPLIT) == 0 and length % tl == 0
    vmem = (2 * tm * (wa_w + wb_w + 2 * d) * 4 + (wglu.size + wa.size + wb.size) * 2
            + 8 * tm * d * 4) * 1.2
    row = pl.BlockSpec((tm, d), lambda i: (i, 0))
    res_major = lambda w: pl.BlockSpec((None, n_res, tl, w),
                                       lambda i: (i // tiles, 0, i % tiles, 0))
    return pl.pallas_call(
        _mix_out_kernel,
        grid=(m // tm,),
        in_specs=[res_major(wa_w), res_major(wb_w), row,
                  _const_spec((1, wa_w)), _const_spec((1, wb_w)),
                  _const_spec(wglu.shape), _const_spec((1, wb_w)),
                  _const_spec(wa.shape), _const_spec(wb.shape)],
        out_specs=row,
        out_shape=jax.ShapeDtypeStruct((m, d), F32),
        compiler_params=_params(("parallel",), vmem),
        name="mix_out",
    )(ya, y, h, ga.reshape(1, -1), gb.reshape(1, -1), wglu, bglu.reshape(1, -1), wa, wb)


def _ple_kernel(h_ref, p_ref, gn_ref, wg_ref, wp_ref, fn_ref, o_ref, *, final):
    h = h_ref[...]
    gate = _sigmoid(_dot(_rms(h, gn_ref[...]).astype(BF16), wg_ref[...]))
    proj = _dot(p_ref[...].astype(BF16), wp_ref[...])
    out = h + gate * proj
    o_ref[...] = _rms(out, fn_ref[...]) if final else out


def _ple(h, p, layer, gn, wg, wp, fn, *, final, tm=512):
    m, d = h.shape
    pd = p.shape[2]
    tm = min(tm, m)
    vmem = (2 * tm * (2 * d + pd) * 4 + (wg.size + wp.size) * 2 + 8 * tm * d * 4) * 1.2
    row = lambda w: pl.BlockSpec((tm, w), lambda i: (i, 0))
    return pl.pallas_call(
        functools.partial(_ple_kernel, final=final),
        grid=(m // tm,),
        in_specs=[row(d), pl.BlockSpec((None, tm, pd), lambda i: (layer, i, 0)),
                  _const_spec((1, d)), _const_spec(wg.shape),
                  _const_spec(wp.shape), _const_spec((1, d))],
        out_specs=row(d),
        out_shape=jax.ShapeDtypeStruct((m, d), F32),
        compiler_params=_params(("parallel",), vmem),
        name="ple",
    )(h, p, gn.reshape(1, d), wg, wp, fn.reshape(1, d))


def kernel(x, p, ffn1_norm, ffn1_w_gate, ffn1_w_up, ffn1_w_down, mix_norm, w_in, attn_out_norm, ssm_lambda_re, ssm_lambda_im, ssm_log_dt, ssm_b_re, ssm_b_im, ssm_c_re, ssm_c_im, ssm_d, ssm_w_glu, ssm_b_glu, ssm_out_norm, w_out, ffn2_norm, ffn2_w_gate, ffn2_w_up, ffn2_w_down, ple_norm, ple_w_gate, ple_w_proj, final_norm):
    b, s, d = x.shape
    depth = p.shape[0]
    attn_w = attn_out_norm.shape[1]
    ssm_w = ssm_out_norm.shape[1]
    n_pairs = attn_w // LANES
    h = x.reshape(b * s, d)
    for i in range(depth):
        h = _ffn(h, ffn1_norm[i], _ff_tiles(ffn1_w_gate, i, 1), _ff_tiles(ffn1_w_up, i, 1),
                 _ff_tiles(ffn1_w_down, i, 0))
        z = _in_proj(h, mix_norm[i], w_in[i].astype(BF16), b)
        ya = _attention(z, n_pairs, q_col=0, k_col=n_pairs, v_col=2 * n_pairs)
        weights = _ssm_weights(ssm_lambda_re[i], ssm_lambda_im[i], ssm_log_dt[i],
                               ssm_b_re[i], ssm_b_im[i], ssm_c_re[i], ssm_c_im[i],
                               ssm_d[i], s // SSM_CHUNK)
        y = _ssm(z, 3 * n_pairs, weights)
        wo = w_out[i].astype(BF16)
        h = _mix_out(ya, y, h, attn_out_norm[i], ssm_out_norm[i], ssm_w_glu[i].astype(BF16),
                     ssm_b_glu[i], wo[:attn_w], wo[attn_w:])
        h = _ffn(h, ffn2_norm[i], _ff_tiles(ffn2_w_gate, i, 1), _ff_tiles(ffn2_w_up, i, 1),
                 _ff_tiles(ffn2_w_down, i, 0))
        h = _ple(h, p.reshape(depth, b * s, -1), i, ple_norm[i], ple_w_gate[i].astype(BF16),
                 ple_w_proj[i].astype(BF16), final_norm, final=i == depth - 1)
    return h.reshape(b, s, d)
```

```python
import functools
import math

import jax
import jax.numpy as jnp
from jax import lax
from jax.experimental import pallas as pl
from jax.experimental.pallas import tpu as pltpu

NORM_EPS = 1e-6
MASK_VALUE = -1e30
HEAD_DIM = 64
ATTN_BLOCK = 128
ATTN_SPAN = 128
DILATIONS = (1, 4, 16)
ATTN_UNROLL = 16
SSM_GROUP = 16
SSM_STATE = 64
SSM_CHUNK = 16
TOKEN_RES = 16
LANES = 128
GROUPS_PER_BLOCK = LANES // SSM_GROUP
FF_TILE = 256
FF_STEP_TILES = 2
FF_STAGE = 512
MIX_SPLIT = 2
IN_SPLIT = 2
VMEM_CAP = 60 * 1024 * 1024

BF16 = jnp.bfloat16
F32 = jnp.float32


def _params(sem, vmem_bytes):
    return pltpu.CompilerParams(
        dimension_semantics=sem, vmem_limit_bytes=min(int(vmem_bytes), VMEM_CAP))


def _rms(x, g):
    ms = jnp.mean(x * x, axis=-1, keepdims=True)
    return x * lax.rsqrt(ms + NORM_EPS) * g


def _dot(a, b):
    return jnp.dot(a, b, preferred_element_type=F32)


def _const_spec(shape):
    return pl.BlockSpec(shape, lambda *_: (0,) * len(shape),
                        pipeline_mode=pl.Buffered(1))


def _ffn_kernel(x_ref, g_ref, wg_ref, wu_ref, wd_ref, o_ref, xn_ref):
    f = pl.program_id(1)

    @pl.when(f == 0)
    def _():
        xn_ref[...] = _rms(x_ref[...], g_ref[...]).astype(BF16)
        o_ref[...] = jnp.zeros_like(o_ref)

    xn = xn_ref[...]
    n_tiles, tf, d = wd_ref.shape
    acts = []
    for t in range(n_tiles):
        gate = _dot(xn, wg_ref[t])
        up = _dot(xn, wu_ref[t])
        acts.append((gate * jax.nn.sigmoid(gate) * up).astype(BF16))
    o_ref[...] += _dot(jnp.concatenate(acts, axis=1), wd_ref[...].reshape(n_tiles * tf, d))

    @pl.when(f == pl.num_programs(1) - 1)
    def _():
        o_ref[...] = x_ref[...] + 0.5 * o_ref[...]


def _ffn(h, g, wg, wu, wd, *, tm=1024):
    m, d = h.shape
    nf, _, tf = wg.shape
    tm = min(tm, m)
    per_step = FF_STEP_TILES if nf % FF_STEP_TILES == 0 else 1
    vmem = (4 * tm * d * 4 + tm * d * 2 + 6 * d * tf * per_step * 2 + 6 * tm * tf * 4
            + 2 * tm * d * 4) * 1.1
    row = pl.BlockSpec((tm, d), lambda i, f: (i, 0))
    return pl.pallas_call(
        _ffn_kernel,
        grid=(m // tm, nf // per_step),
        in_specs=[
            row,
            pl.BlockSpec((1, d), lambda i, f: (0, 0)),
            pl.BlockSpec((per_step, d, tf), lambda i, f: (f, 0, 0)),
            pl.BlockSpec((per_step, d, tf), lambda i, f: (f, 0, 0)),
            pl.BlockSpec((per_step, tf, d), lambda i, f: (f, 0, 0)),
        ],
        out_specs=row,
        out_shape=jax.ShapeDtypeStruct((m, d), F32),
        scratch_shapes=[pltpu.VMEM((tm, d), BF16)],
        compiler_params=_params(("parallel", "arbitrary"), vmem),
        name="ffn",
    )(h, g.reshape(1, d), wg, wu, wd)


def _ff_tile_kernel(w_ref, o_ref, *, axis, size):
    w = w_ref[...]
    n_tiles = o_ref.shape[0]
    tile = o_ref.shape[1 + axis]
    pos = lax.broadcasted_iota(jnp.int32, w.shape, axis) + pl.program_id(0) * (n_tiles * tile)
    w = jnp.where(pos < size, w, 0.0).astype(o_ref.dtype)
    for t in range(n_tiles):
        o_ref[t] = w[:, t * tile:(t + 1) * tile] if axis == 1 else w[t * tile:(t + 1) * tile]


def _ff_tiles(w, layer, axis, tf=FF_TILE):
    size = w.shape[1 + axis]
    per_step = max(1, FF_STAGE // tf)
    nf = pl.cdiv(size, tf * per_step) * per_step
    tile = list(w.shape[1:])
    tile[axis] = tf
    stage = list(tile)
    stage[axis] = tf * per_step
    index = (lambda f: (layer, 0, f)) if axis == 1 else (lambda f: (layer, f, 0))
    return pl.pallas_call(
        functools.partial(_ff_tile_kernel, axis=axis, size=size),
        grid=(nf // per_step,),
        in_specs=[pl.BlockSpec((None,) + tuple(stage), index)],
        out_specs=pl.BlockSpec((per_step,) + tuple(tile), lambda f: (f, 0, 0)),
        out_shape=jax.ShapeDtypeStruct((nf,) + tuple(tile), BF16),
        compiler_params=_params(("parallel",), 6 * math.prod(stage) * 4),
        name="ff_tiles",
    )(w)


def _residue_perm(n_res, tl, transpose):
    tm = n_res * tl
    assert tl & (tl - 1) == 0
    a = lax.broadcasted_iota(jnp.int32, (tm, tm), 1 if transpose else 0)
    b = lax.broadcasted_iota(jnp.int32, (tm, tm), 0 if transpose else 1)
    hit = b == n_res * (a & (tl - 1)) + (a >> (tl.bit_length() - 1))
    return jnp.where(hit, 1.0, 0.0).astype(BF16)


def _in_proj_kernel(x_ref, g_ref, w_ref, o_ref):
    n_res, tl, _ = o_ref.shape
    sub = tl // IN_SPLIT
    rows = n_res * sub
    perm = _residue_perm(n_res, sub, False)
    for s in range(IN_SPLIT):
        xn = _rms(x_ref[s * rows:(s + 1) * rows, :], g_ref[...]).astype(BF16)
        xn = _dot(perm, xn).astype(BF16)
        res = _dot(xn, w_ref[...])
        for r in range(n_res):
            o_ref[r, s * sub:(s + 1) * sub, :] = res[r * sub:(r + 1) * sub]


def _in_proj(h, g, w, batch, *, tm=512):
    m, d = h.shape
    n = w.shape[1]
    seq = m // batch
    tm = min(tm, seq)
    tl = tm // TOKEN_RES
    tiles = seq // tm
    assert tl % (8 * IN_SPLIT) == 0 and seq % tm == 0
    vmem = (2 * tm * d * 4 + tm * d * 6 + d * n * 2 + 3 * tm * n * 4 + 4 * tm * tm) * 1.2
    return pl.pallas_call(
        _in_proj_kernel,
        grid=(m // tm,),
        in_specs=[
            pl.BlockSpec((tm, d), lambda i: (i, 0)),
            _const_spec((1, d)),
            _const_spec((d, n)),
        ],
        out_specs=pl.BlockSpec((None, TOKEN_RES, tl, n),
                               lambda i: (i // tiles, 0, i % tiles, 0)),
        out_shape=jax.ShapeDtypeStruct((batch, TOKEN_RES, seq // TOKEN_RES, n), F32),
        compiler_params=_params(("parallel",), vmem),
        name="in_proj",
    )(h, g.reshape(1, d), w)


def _attn_kernel(q_ref, k_ref, v_ref, o_ref, m_ref, l_ref, bias_ref, p_sc, mx_sc):
    n_res, length, _ = q_ref.shape
    blk = ATTN_BLOCK
    n_units = n_res * length // blk
    head0 = lax.broadcasted_iota(jnp.int32, (blk, LANES), 1) < HEAD_DIM
    kv_head0 = lax.broadcasted_iota(jnp.int32, (2 * blk, LANES), 1) < HEAD_DIM
    scale = HEAD_DIM ** -0.5 * math.log2(math.e)

    rho = lax.broadcasted_iota(jnp.int32, (blk, 2 * blk), 0)
    kap = lax.broadcasted_iota(jnp.int32, (blk, 2 * blk), 1)
    for pattern, d in enumerate(DILATIONS):
        n_sub = n_res // d
        c_rows = blk // n_sub
        sh = c_rows.bit_length() - 1
        iq, lq = rho >> sh, rho & (c_rows - 1)
        ik, lk = kap >> (sh + 1), kap & (2 * c_rows - 1)
        for shift in range(2):
            dist = n_sub * (lq - lk + shift * c_rows) + (iq - ik)
            bias_ref[2 * pattern + shift] = jnp.where(
                (dist >= 0) & (dist <= ATTN_SPAN), 0.0, MASK_VALUE)

    def probs(q, k, bias, sel):
        qh = jnp.where(sel, q, 0.0).astype(BF16)
        s = lax.dot_general(qh, k, (((1,), (1,)), ((), ())),
                            preferred_element_type=F32) + bias
        m = jnp.max(s, axis=-1, keepdims=True)
        return jnp.exp2(s - m).astype(BF16), m

    n_groups = n_units // ATTN_UNROLL
    for pattern, d in enumerate(DILATIONS):
        first = pattern == 0
        last = pattern == len(DILATIONS) - 1
        n_sub = n_res // d
        c_rows = blk // n_sub
        blocks_per_residue = n_units // d

        def gather(ref, res, start, size, d=d, n_sub=n_sub):
            return jnp.concatenate(
                [ref[d * i + res, pl.ds(start, size), :] for i in range(n_sub)], axis=0)

        def scatter(ref, res, start, val, d=d, n_sub=n_sub, c_rows=c_rows):
            for i in range(n_sub):
                ref[d * i + res, pl.ds(start, c_rows), :] = val[i * c_rows:(i + 1) * c_rows]

        def place(g, uu, c_rows=c_rows, blocks_per_residue=blocks_per_residue):
            u = g * ATTN_UNROLL + uu
            res = u // blocks_per_residue
            bq = u % blocks_per_residue
            kb = jnp.maximum(bq - 1, 0)
            return (res, pl.multiple_of(c_rows * bq, 8), pl.multiple_of(c_rows * kb, 8), bq - kb)

        def score_stage(g, slot, pattern=pattern, c_rows=c_rows, gather=gather, place=place):
            loaded = []
            for uu in range(ATTN_UNROLL):
                res, q0, k0, shift = place(g, uu)
                loaded.append((gather(q_ref, res, q0, c_rows) * scale,
                               gather(k_ref, res, k0, 2 * c_rows).astype(BF16),
                               bias_ref[2 * pattern + shift]))
            for uu, (q, k, bias) in enumerate(loaded):
                p0, m0 = probs(q, k, bias, head0)
                p1, m1 = probs(q, k, bias, jnp.logical_not(head0))
                p_sc[slot, uu, 0] = p0
                p_sc[slot, uu, 1] = p1
                mx_sc[slot, uu] = jnp.where(head0, m0, m1)

        def value_stage(g, slot, first=first, last=last, c_rows=c_rows,
                        gather=gather, scatter=scatter, place=place):
            loaded = []
            for uu in range(ATTN_UNROLL):
                res, q0, k0, _ = place(g, uu)
                old = None if first else tuple(
                    gather(ref, res, q0, c_rows) for ref in (o_ref, m_ref, l_ref))
                loaded.append((res, q0, gather(v_ref, res, k0, 2 * c_rows).astype(BF16),
                               p_sc[slot, uu, 0], p_sc[slot, uu, 1], mx_sc[slot, uu], old))
            results = []
            for res, q0, v, p0, p1, m_new, old in loaded:
                ones = jnp.ones_like(v)
                o0 = _dot(p0, jnp.where(kv_head0, v, ones))
                o1 = _dot(p1, jnp.where(kv_head0, ones, v))
                o_new = jnp.where(head0, o0, o1)
                l_new = pltpu.roll(jnp.where(head0, o1, o0), HEAD_DIM, 1)
                if not first:
                    o_old, m_old, l_old = old
                    m_tot = jnp.maximum(m_old, m_new)
                    w_old = jnp.exp2(m_old - m_tot)
                    w_new = jnp.exp2(m_new - m_tot)
                    o_new = w_old * o_old + w_new * o_new
                    l_new = w_old * l_old + w_new * l_new
                    m_new = m_tot
                results.append((res, q0, o_new, m_new, l_new))
            for res, q0, o_new, m_new, l_new in results:
                if last:
                    scatter(o_ref, res, q0, o_new / l_new)
                else:
                    scatter(o_ref, res, q0, o_new)
                    scatter(m_ref, res, q0, m_new)
                    scatter(l_ref, res, q0, l_new)

        def step(g, carry, score_stage=score_stage, value_stage=value_stage):
            slot = g & 1
            value_stage(g - 1, 1 - slot)
            score_stage(g, slot)
            return carry

        score_stage(0, 0)
        lax.fori_loop(1, n_groups, step, 0)
        value_stage(n_groups - 1, (n_groups - 1) & 1)


def _attention(z, n_pairs, *, q_col, k_col, v_col):
    b, n_res, length, _ = z.shape
    assert n_res == TOKEN_RES == DILATIONS[-1] and all(n_res % d == 0 for d in DILATIONS)
    assert length % ATTN_BLOCK == 0 and length >= 2 * ATTN_BLOCK and ATTN_BLOCK // n_res >= 8
    assert (n_res * length // ATTN_BLOCK) % ATTN_UNROLL == 0

    def spec(col):
        return pl.BlockSpec((None, n_res, length, LANES), lambda bi, h: (bi, 0, 0, col + h))

    slab = n_res * length * LANES * 4
    vmem = 10 * slab * 1.15 + (8 << 20)
    return pl.pallas_call(
        _attn_kernel,
        grid=(b, n_pairs),
        in_specs=[spec(q_col), spec(k_col), spec(v_col)],
        out_specs=spec(0),
        out_shape=jax.ShapeDtypeStruct((b, n_res, length, n_pairs * LANES), F32),
        scratch_shapes=[pltpu.VMEM((n_res, length, LANES), F32),
                        pltpu.VMEM((n_res, length, LANES), F32),
                        pltpu.VMEM((2 * len(DILATIONS), ATTN_BLOCK, 2 * ATTN_BLOCK), F32),
                        pltpu.VMEM((2, ATTN_UNROLL, 2, ATTN_BLOCK, 2 * ATTN_BLOCK), BF16),
                        pltpu.VMEM((2, ATTN_UNROLL, ATTN_BLOCK, LANES), F32)],
        compiler_params=_params(("parallel", "parallel"), vmem),
        name="attention",
    )(z, z, z)


def _ssm_weights(lam_re, lam_im, log_dt, b_re, b_im, c_re, c_im, d_skip, n_chunks):
    hp = lax.Precision.HIGHEST
    g, p = lam_re.shape
    c = b_re.shape[-1]
    t = SSM_CHUNK
    gb = GROUPS_PER_BLOCK
    nj = g // gb
    lr, li = lam_re.astype(F32), lam_im.astype(F32)
    dt = jnp.exp(log_dt.astype(F32))[:, None]
    mag = jnp.exp(lr * dt)
    ar = mag * jnp.cos(li * dt)
    ai = mag * jnp.sin(li * dt)
    nr, ni = ar - 1.0, ai
    den = lr * lr + li * li
    cr = (nr * lr + ni * li) / den
    ci = (ni * lr - nr * li) / den
    br, bi = b_re.astype(F32), b_im.astype(F32)
    bbr = cr[..., None] * br - ci[..., None] * bi
    bbi = cr[..., None] * bi + ci[..., None] * br

    def power(k):
        kk = k.astype(F32)[:, None, None]
        pm = jnp.exp(kk * (lr * dt))
        ang = kk * (li * dt)
        return pm * jnp.cos(ang), pm * jnp.sin(ang)

    pr, pi = power(jnp.arange(t + 1))
    cre, cim = c_re.astype(F32), c_im.astype(F32)
    car = cre[None] * pr[:, :, None, :] - cim[None] * pi[:, :, None, :]
    cai = cre[None] * pi[:, :, None, :] + cim[None] * pr[:, :, None, :]
    kern = jnp.einsum('kgdq,gqc->kgdc', jnp.concatenate([car[:t], -cai[:t]], axis=-1),
                      jnp.concatenate([bbr, bbi], axis=1), precision=hp)
    kern = kern.at[0].add(d_skip.astype(F32).reshape(g, c)[:, :, None] * jnp.eye(c, dtype=F32))

    kr = kern.reshape(t, nj, gb, c, c).transpose(1, 0, 4, 2, 3).reshape(nj, t, c, gb * c)

    rev_r, rev_i = pr[:t][::-1][:, :, None, :], pi[:t][::-1][:, :, None, :]
    bbr_t, bbi_t = bbr.transpose(0, 2, 1)[None], bbi.transpose(0, 2, 1)[None]
    abr = rev_r * bbr_t - rev_i * bbi_t
    abi = rev_r * bbi_t + rev_i * bbr_t

    def block_rows(x):
        return x.reshape(t, nj, gb, c, p).transpose(1, 0, 2, 3, 4).reshape(nj, t * gb * c, p)

    def rows_form(x):
        x = block_rows(x)
        return jnp.concatenate([x, x], axis=-1).astype(BF16)

    def cols_form(x):
        x = block_rows(x).transpose(0, 2, 1)
        return jnp.concatenate([x, x], axis=1).astype(BF16)

    n_steps = max(1, (n_chunks - 1).bit_length())
    sr, si = power(t * (2 ** jnp.arange(n_steps)))
    rows = -(-n_steps // 8) * 8
    sr = jnp.pad(sr.reshape(n_steps, nj, gb * p).transpose(1, 0, 2), ((0, 0), (0, rows - n_steps), (0, 0)))
    si = jnp.pad(si.reshape(n_steps, nj, gb * p).transpose(1, 0, 2), ((0, 0), (0, rows - n_steps), (0, 0)))
    return (kr.astype(BF16), rows_form(abr), rows_form(abi),
            cols_form(car[1:]), cols_form(-cai[1:]), sr, si)


def _ssm_expand(kr_ref, abr_ref, abi_ref, cmr_ref, cmi_ref, m_sc, b_sc, c_sc):
    t, c, gb, p = SSM_CHUNK, SSM_GROUP, GROUPS_PER_BLOCK, SSM_STATE
    shift = c.bit_length() - 1
    half = gb * p
    pairs = LANES // p

    def group_of(idx):
        return (idx >> shift) & (gb - 1)

    m_sc[...] = jnp.zeros_like(m_sc)
    col_g = group_of(lax.broadcasted_iota(jnp.int32, (c, LANES), 1))
    for k in range(t):
        kk = kr_ref[k]
        dk = jnp.concatenate([jnp.where(col_g == g, kk, jnp.zeros_like(kk)) for g in range(gb)], axis=0)
        for tau in range(t - k):
            m_sc[tau * LANES:(tau + 1) * LANES, (tau + k) * LANES:(tau + k + 1) * LANES] = dk

    row_g = group_of(lax.broadcasted_iota(jnp.int32, (t * LANES, LANES), 0))
    lane_half = lax.broadcasted_iota(jnp.int32, (t * LANES, LANES), 1) >> (p.bit_length() - 1)
    for part, ab_ref in enumerate((abr_ref, abi_ref)):
        ab = ab_ref[...]
        for q in range(gb // pairs):
            b_sc[:, part * half + q * LANES:part * half + (q + 1) * LANES] = jnp.where(
                row_g == pairs * q + lane_half, ab, jnp.zeros_like(ab))

    col_g2 = group_of(lax.broadcasted_iota(jnp.int32, (LANES, t * LANES), 1))
    row_half = lax.broadcasted_iota(jnp.int32, (LANES, t * LANES), 0) >> (p.bit_length() - 1)
    for part, cm_ref in enumerate((cmr_ref, cmi_ref)):
        cm = cm_ref[...]
        for q in range(gb // pairs):
            c_sc[part * half + q * LANES:part * half + (q + 1) * LANES, :] = jnp.where(
                col_g2 == pairs * q + row_half, cm, jnp.zeros_like(cm))


def _ssm_kernel(s_ref, kr_ref, abr_ref, abi_ref, cmr_ref, cmi_ref, sr_ref, si_ref, y_ref,
                m_sc, b_sc, c_sc):
    t, n_chunks, _ = s_ref.shape
    half = GROUPS_PER_BLOCK * SSM_STATE

    @pl.when(pl.program_id(1) == 0)
    def _():
        _ssm_expand(kr_ref, abr_ref, abi_ref, cmr_ref, cmi_ref, m_sc, b_sc, c_sc)

    u = jnp.concatenate([s_ref[tau].astype(BF16) for tau in range(t)], axis=1)
    y = _dot(u, m_sc[...])
    x = _dot(u, b_sc[...])
    hr, hi = x[:, :half], x[:, half:]

    row = lax.broadcasted_iota(jnp.int32, (n_chunks, half), 0)

    def shifted(h, k):
        return jnp.where(row >= k, pltpu.roll(h, k, 0), 0.0)

    step = 0
    while (1 << step) < n_chunks:
        k = 1 << step
        pr = sr_ref[step:step + 1, :]
        pi = si_ref[step:step + 1, :]
        zr, zi = shifted(hr, k), shifted(hi, k)
        hr, hi = hr + pr * zr - pi * zi, hi + pr * zi + pi * zr
        step += 1

    h_prev = jnp.concatenate([shifted(hr, 1), shifted(hi, 1)], axis=1).astype(BF16)
    y = jax.nn.gelu(y + _dot(h_prev, c_sc[...]))
    for tau in range(t):
        y_ref[tau] = y[:, tau * LANES:(tau + 1) * LANES]


def _ssm(z, s_col, weights):
    b, n_res, n_chunks, _ = z.shape
    nj = weights[0].shape[0]
    assert n_res == SSM_CHUNK
    assert SSM_GROUP & (SSM_GROUP - 1) == 0 and SSM_STATE & (SSM_STATE - 1) == 0

    def wspec(a):
        nd = a.ndim - 1
        return pl.BlockSpec((None,) + a.shape[1:], lambda j, bi: (j,) + (0,) * nd)

    s = n_res * n_chunks
    width = SSM_CHUNK * LANES
    state = 2 * GROUPS_PER_BLOCK * SSM_STATE
    scratch = (width * width + 2 * width * state) * 2
    wbytes = 2 * sum(math.prod(a.shape[1:]) * a.dtype.itemsize for a in weights)
    vmem = (4 * s * LANES * 4 + wbytes + scratch + n_chunks * (width * 10 + state * 12)) * 1.2
    return pl.pallas_call(
        _ssm_kernel,
        grid=(nj, b),
        in_specs=[pl.BlockSpec((None, n_res, n_chunks, LANES),
                               lambda j, bi: (bi, 0, 0, s_col + j))]
        + [wspec(a) for a in weights],
        out_specs=pl.BlockSpec((None, n_res, n_chunks, LANES), lambda j, bi: (bi, 0, 0, j)),
        out_shape=jax.ShapeDtypeStruct((b, n_res, n_chunks, nj * LANES), F32),
        scratch_shapes=[pltpu.VMEM((width, width), BF16), pltpu.VMEM((width, state), BF16),
                        pltpu.VMEM((state, width), BF16)],
        compiler_params=_params(("parallel", "arbitrary"), vmem),
        name="ssm",
    )(z, *weights)


def _mix_out_kernel(ya_ref, y_ref, h_ref, ga_ref, gb_ref, wglu_ref, bglu_ref,
                    wa_ref, wb_ref, o_ref):
    n_res, tl, _ = ya_ref.shape
    sub = tl // MIX_SPLIT
    rows = n_res * sub
    perm = _residue_perm(n_res, sub, True)
    for s in range(MIX_SPLIT):
        ls = slice(s * sub, (s + 1) * sub)
        y = y_ref[:, ls, :].reshape(rows, -1)
        gate = _dot(y.astype(BF16), wglu_ref[...]) + bglu_ref[...]
        yb = y * jax.nn.sigmoid(gate)
        na = _rms(ya_ref[:, ls, :].reshape(rows, -1), ga_ref[...]).astype(BF16)
        nb = _rms(yb, gb_ref[...]).astype(BF16)
        na = _dot(perm, na).astype(BF16)
        nb = _dot(perm, nb).astype(BF16)
        tok = slice(s * rows, (s + 1) * rows)
        o_ref[tok, :] = h_ref[tok, :] + _dot(na, wa_ref[...]) + _dot(nb, wb_ref[...])


def _mix_out(ya, y, h, ga, gb, wglu, bglu, wa, wb, *, tm=512):
    m, d = h.shape
    batch, n_res, length, wa_w = ya.shape
    wb_w = y.shape[-1]
    tm = min(tm, n_res * length)
    tl = tm // n_res
    tiles = length // tl
    assert tl % (8 * MIX_SPLIT) == 0 and length % tl == 0
    vmem = (2 * tm * (wa_w + wb_w + 2 * d) * 4 + (wglu.size + wa.size + wb.size) * 2
            + 8 * tm * d * 4) * 1.2
    row = pl.BlockSpec((tm, d), lambda i: (i, 0))
    res_major = lambda w: pl.BlockSpec((None, n_res, tl, w),
                                       lambda i: (i // tiles, 0, i % tiles, 0))
    return pl.pallas_call(
        _mix_out_kernel,
        grid=(m // tm,),
        in_specs=[res_major(wa_w), res_major(wb_w), row,
                  _const_spec((1, wa_w)), _const_spec((1, wb_w)),
                  _const_spec(wglu.shape), _const_spec((1, wb_w)),
                  _const_spec(wa.shape), _const_spec(wb.shape)],
        out_specs=row,
        out_shape=jax.ShapeDtypeStruct((m, d), F32),
        compiler_params=_params(("parallel",), vmem),
        name="mix_out",
    )(ya, y, h, ga.reshape(1, -1), gb.reshape(1, -1), wglu, bglu.reshape(1, -1), wa, wb)


def _ple_kernel(h_ref, p_ref, gn_ref, wg_ref, wp_ref, fn_ref, o_ref, *, final):
    h = h_ref[...]
    gate = jax.nn.sigmoid(_dot(_rms(h, gn_ref[...]).astype(BF16), wg_ref[...]))
    proj = _dot(p_ref[...].astype(BF16), wp_ref[...])
    out = h + gate * proj
    o_ref[...] = _rms(out, fn_ref[...]) if final else out


def _ple(h, p, layer, gn, wg, wp, fn, *, final, tm=512):
    m, d = h.shape
    pd = p.shape[2]
    tm = min(tm, m)
    vmem = (2 * tm * (2 * d + pd) * 4 + (wg.size + wp.size) * 2 + 8 * tm * d * 4) * 1.2
    row = lambda w: pl.BlockSpec((tm, w), lambda i: (i, 0))
    return pl.pallas_call(
        functools.partial(_ple_kernel, final=final),
        grid=(m // tm,),
        in_specs=[row(d), pl.BlockSpec((None, tm, pd), lambda i: (layer, i, 0)),
                  _const_spec((1, d)), _const_spec(wg.shape),
                  _const_spec(wp.shape), _const_spec((1, d))],
        out_specs=row(d),
        out_shape=jax.ShapeDtypeStruct((m, d), F32),
        compiler_params=_params(("parallel",), vmem),
        name="ple",
    )(h, p, gn.reshape(1, d), wg, wp, fn.reshape(1, d))


def kernel(x, p, ffn1_norm, ffn1_w_gate, ffn1_w_up, ffn1_w_down, mix_norm, w_in, attn_out_norm, ssm_lambda_re, ssm_lambda_im, ssm_log_dt, ssm_b_re, ssm_b_im, ssm_c_re, ssm_c_im, ssm_d, ssm_w_glu, ssm_b_glu, ssm_out_norm, w_out, ffn2_norm, ffn2_w_gate, ffn2_w_up, ffn2_w_down, ple_norm, ple_w_gate, ple_w_proj, final_norm):
    b, s, d = x.shape
    depth = p.shape[0]
    attn_w = attn_out_norm.shape[1]
    ssm_w = ssm_out_norm.shape[1]
    n_pairs = attn_w // LANES
    h = x.reshape(b * s, d)
    for i in range(depth):
        h = _ffn(h, ffn1_norm[i], _ff_tiles(ffn1_w_gate, i, 1), _ff_tiles(ffn1_w_up, i, 1),
                 _ff_tiles(ffn1_w_down, i, 0))
        z = _in_proj(h, mix_norm[i], w_in[i].astype(BF16), b)
        ya = _attention(z, n_pairs, q_col=0, k_col=n_pairs, v_col=2 * n_pairs)
        weights = _ssm_weights(ssm_lambda_re[i], ssm_lambda_im[i], ssm_log_dt[i],
                               ssm_b_re[i], ssm_b_im[i], ssm_c_re[i], ssm_c_im[i],
                               ssm_d[i], s // SSM_CHUNK)
        y = _ssm(z, 3 * n_pairs, weights)
        wo = w_out[i].astype(BF16)
        h = _mix_out(ya, y, h, attn_out_norm[i], ssm_out_norm[i], ssm_w_glu[i].astype(BF16),
                     ssm_b_glu[i], wo[:attn_w], wo[attn_w:])
        h = _ffn(h, ffn2_norm[i], _ff_tiles(ffn2_w_gate, i, 1), _ff_tiles(ffn2_w_up, i, 1),
                 _ff_tiles(ffn2_w_down, i, 0))
        h = _ple(h, p.reshape(depth, b * s, -1), i, ple_norm[i], ple_w_gate[i].astype(BF16),
                 ple_w_proj[i].astype(BF16), final_norm, final=i == depth - 1)
    return h.reshape(b, s, d)
```

```python
import functools
import math

import jax
import jax.numpy as jnp
from jax import lax
from jax.experimental import pallas as pl
from jax.experimental.pallas import tpu as pltpu

NORM_EPS = 1e-6
MASK_VALUE = -1e30
HEAD_DIM = 64
ATTN_BLOCK = 128
ATTN_SPAN = 128
DILATIONS = (1, 4, 16)
ATTN_UNROLL = 16
SSM_GROUP = 16
SSM_STATE = 64
SSM_CHUNK = 16
TOKEN_RES = 16
LANES = 128
GROUPS_PER_BLOCK = LANES // SSM_GROUP
FF_TILE = 256
FF_STEP_TILES = 2
FF_STAGE = 512
MIX_SPLIT = 2
IN_SPLIT = 2
VMEM_CAP = 60 * 1024 * 1024

BF16 = jnp.bfloat16
F32 = jnp.float32


def _params(sem, vmem_bytes):
    return pltpu.CompilerParams(
        dimension_semantics=sem, vmem_limit_bytes=min(int(vmem_bytes), VMEM_CAP))


def _rms(x, g):
    ms = jnp.mean(x * x, axis=-1, keepdims=True)
    return x * lax.rsqrt(ms + NORM_EPS) * g


def _dot(a, b):
    return jnp.dot(a, b, preferred_element_type=F32)


def _const_spec(shape):
    return pl.BlockSpec(shape, lambda *_: (0,) * len(shape),
                        pipeline_mode=pl.Buffered(1))


def _ffn_kernel(x_ref, g_ref, wg_ref, wu_ref, wd_ref, o_ref, xn_ref):
    f = pl.program_id(1)

    @pl.when(f == 0)
    def _():
        xn_ref[...] = _rms(x_ref[...], g_ref[...]).astype(BF16)
        o_ref[...] = jnp.zeros_like(o_ref)

    xn = xn_ref[...]
    n_tiles, tf, d = wd_ref.shape
    acts = []
    for t in range(n_tiles):
        gate = _dot(xn, wg_ref[t])
        up = _dot(xn, wu_ref[t])
        acts.append((gate * jax.nn.sigmoid(gate) * up).astype(BF16))
    o_ref[...] += _dot(jnp.concatenate(acts, axis=1), wd_ref[...].reshape(n_tiles * tf, d))

    @pl.when(f == pl.num_programs(1) - 1)
    def _():
        o_ref[...] = x_ref[...] + 0.5 * o_ref[...]


def _ffn(h, g, wg, wu, wd, *, tm=1024):
    m, d = h.shape
    nf, _, tf = wg.shape
    tm = min(tm, m)
    per_step = FF_STEP_TILES if nf % FF_STEP_TILES == 0 else 1
    vmem = (4 * tm * d * 4 + tm * d * 2 + 6 * d * tf * per_step * 2 + 6 * tm * tf * 4
            + 2 * tm * d * 4) * 1.1
    row = pl.BlockSpec((tm, d), lambda i, f: (i, 0))
    return pl.pallas_call(
        _ffn_kernel,
        grid=(m // tm, nf // per_step),
        in_specs=[
            row,
            pl.BlockSpec((1, d), lambda i, f: (0, 0)),
            pl.BlockSpec((per_step, d, tf), lambda i, f: (f, 0, 0)),
            pl.BlockSpec((per_step, d, tf), lambda i, f: (f, 0, 0)),
            pl.BlockSpec((per_step, tf, d), lambda i, f: (f, 0, 0)),
        ],
        out_specs=row,
        out_shape=jax.ShapeDtypeStruct((m, d), F32),
        scratch_shapes=[pltpu.VMEM((tm, d), BF16)],
        compiler_params=_params(("parallel", "arbitrary"), vmem),
        name="ffn",
    )(h, g.reshape(1, d), wg, wu, wd)


def _ff_tile_kernel(w_ref, o_ref, *, axis, size):
    w = w_ref[...]
    n_tiles = o_ref.shape[0]
    tile = o_ref.shape[1 + axis]
    pos = lax.broadcasted_iota(jnp.int32, w.shape, axis) + pl.program_id(0) * (n_tiles * tile)
    w = jnp.where(pos < size, w, 0.0).astype(o_ref.dtype)
    for t in range(n_tiles):
        o_ref[t] = w[:, t * tile:(t + 1) * tile] if axis == 1 else w[t * tile:(t + 1) * tile]


def _ff_tiles(w, layer, axis, tf=FF_TILE):
    size = w.shape[1 + axis]
    per_step = max(1, FF_STAGE // tf)
    nf = pl.cdiv(size, tf * per_step) * per_step
    tile = list(w.shape[1:])
    tile[axis] = tf
    stage = list(tile)
    stage[axis] = tf * per_step
    index = (lambda f: (layer, 0, f)) if axis == 1 else (lambda f: (layer, f, 0))
    return pl.pallas_call(
        functools.partial(_ff_tile_kernel, axis=axis, size=size),
        grid=(nf // per_step,),
        in_specs=[pl.BlockSpec((None,) + tuple(stage), index)],
        out_specs=pl.BlockSpec((per_step,) + tuple(tile), lambda f: (f, 0, 0)),
        out_shape=jax.ShapeDtypeStruct((nf,) + tuple(tile), BF16),
        compiler_params=_params(("parallel",), 6 * math.prod(stage) * 4),
        name="ff_tiles",
    )(w)


def _residue_perm(n_res, tl, transpose):
    tm = n_res * tl
    assert tl & (tl - 1) == 0
    a = lax.broadcasted_iota(jnp.int32, (tm, tm), 1 if transpose else 0)
    b = lax.broadcasted_iota(jnp.int32, (tm, tm), 0 if transpose else 1)
    hit = b == n_res * (a & (tl - 1)) + (a >> (tl.bit_length() - 1))
    return jnp.where(hit, 1.0, 0.0).astype(BF16)


def _in_proj_kernel(x_ref, g_ref, w_ref, o_ref):
    n_res, tl, _ = o_ref.shape
    sub = tl // IN_SPLIT
    rows = n_res * sub
    perm = _residue_perm(n_res, sub, False)
    for s in range(IN_SPLIT):
        xn = _rms(x_ref[s * rows:(s + 1) * rows, :], g_ref[...]).astype(BF16)
        xn = _dot(perm, xn).astype(BF16)
        res = _dot(xn, w_ref[...])
        for r in range(n_res):
            o_ref[r, s * sub:(s + 1) * sub, :] = res[r * sub:(r + 1) * sub]


def _in_proj(h, g, w, batch, *, tm=512):
    m, d = h.shape
    n = w.shape[1]
    seq = m // batch
    tm = min(tm, seq)
    tl = tm // TOKEN_RES
    tiles = seq // tm
    assert tl % (8 * IN_SPLIT) == 0 and seq % tm == 0
    vmem = (2 * tm * d * 4 + tm * d * 6 + d * n * 2 + 3 * tm * n * 4 + 4 * tm * tm) * 1.2
    return pl.pallas_call(
        _in_proj_kernel,
        grid=(m // tm,),
        in_specs=[
            pl.BlockSpec((tm, d), lambda i: (i, 0)),
            _const_spec((1, d)),
            _const_spec((d, n)),
        ],
        out_specs=pl.BlockSpec((None, TOKEN_RES, tl, n),
                               lambda i: (i // tiles, 0, i % tiles, 0)),
        out_shape=jax.ShapeDtypeStruct((batch, TOKEN_RES, seq // TOKEN_RES, n), F32),
        compiler_params=_params(("parallel",), vmem),
        name="in_proj",
    )(h, g.reshape(1, d), w)


def _attn_kernel(q_ref, k_ref, v_ref, o_ref, m_ref, l_ref, bias_ref, p_sc, mx_sc):
    n_res, length, _ = q_ref.shape
    blk = ATTN_BLOCK
    n_units = n_res * length // blk
    head0 = lax.broadcasted_iota(jnp.int32, (blk, LANES), 1) < HEAD_DIM
    kv_head0 = lax.broadcasted_iota(jnp.int32, (2 * blk, LANES), 1) < HEAD_DIM
    scale = HEAD_DIM ** -0.5 * math.log2(math.e)

    rho = lax.broadcasted_iota(jnp.int32, (blk, 2 * blk), 0)
    kap = lax.broadcasted_iota(jnp.int32, (blk, 2 * blk), 1)
    for pattern, d in enumerate(DILATIONS):
        n_sub = n_res // d
        c_rows = blk // n_sub
        sh = c_rows.bit_length() - 1
        iq, lq = rho >> sh, rho & (c_rows - 1)
        ik, lk = kap >> (sh + 1), kap & (2 * c_rows - 1)
        for shift in range(2):
            dist = n_sub * (lq - lk + shift * c_rows) + (iq - ik)
            bias_ref[2 * pattern + shift] = jnp.where(
                (dist >= 0) & (dist <= ATTN_SPAN), 0.0, MASK_VALUE)

    def probs(q, k, bias, sel):
        qh = jnp.where(sel, q, 0.0).astype(BF16)
        s = lax.dot_general(qh, k, (((1,), (1,)), ((), ())),
                            preferred_element_type=F32) + bias
        m = jnp.max(s, axis=-1, keepdims=True)
        return jnp.exp2(s - m).astype(BF16), m

    n_groups = n_units // ATTN_UNROLL
    for pattern, d in enumerate(DILATIONS):
        first = pattern == 0
        last = pattern == len(DILATIONS) - 1
        n_sub = n_res // d
        c_rows = blk // n_sub
        blocks_per_residue = n_units // d

        def gather(ref, res, start, size, d=d, n_sub=n_sub):
            return jnp.concatenate(
                [ref[d * i + res, pl.ds(start, size), :] for i in range(n_sub)], axis=0)

        def scatter(ref, res, start, val, d=d, n_sub=n_sub, c_rows=c_rows):
            for i in range(n_sub):
                ref[d * i + res, pl.ds(start, c_rows), :] = val[i * c_rows:(i + 1) * c_rows]

        def place(g, uu, c_rows=c_rows, blocks_per_residue=blocks_per_residue):
            u = g * ATTN_UNROLL + uu
            res = u // blocks_per_residue
            bq = u % blocks_per_residue
            kb = jnp.maximum(bq - 1, 0)
            return (res, pl.multiple_of(c_rows * bq, 8), pl.multiple_of(c_rows * kb, 8), bq - kb)

        def score_stage(g, slot, pattern=pattern, c_rows=c_rows, gather=gather, place=place):
            loaded = []
            for uu in range(ATTN_UNROLL):
                res, q0, k0, shift = place(g, uu)
                loaded.append((gather(q_ref, res, q0, c_rows) * scale,
                               gather(k_ref, res, k0, 2 * c_rows).astype(BF16),
                               bias_ref[2 * pattern + shift]))
            for uu, (q, k, bias) in enumerate(loaded):
                p0, m0 = probs(q, k, bias, head0)
                p1, m1 = probs(q, k, bias, jnp.logical_not(head0))
                p_sc[slot, uu, 0] = p0
                p_sc[slot, uu, 1] = p1
                mx_sc[slot, uu] = jnp.where(head0, m0, m1)

        def value_stage(g, slot, first=first, last=last, c_rows=c_rows,
                        gather=gather, scatter=scatter, place=place):
            loaded = []
            for uu in range(ATTN_UNROLL):
                res, q0, k0, _ = place(g, uu)
                old = None if first else tuple(
                    gather(ref, res, q0, c_rows) for ref in (o_ref, m_ref, l_ref))
                loaded.append((res, q0, gather(v_ref, res, k0, 2 * c_rows).astype(BF16),
                               p_sc[slot, uu, 0], p_sc[slot, uu, 1], mx_sc[slot, uu], old))
            results = []
            for res, q0, v, p0, p1, m_new, old in loaded:
                ones = jnp.ones_like(v)
                o0 = _dot(p0, jnp.where(kv_head0, v, ones))
                o1 = _dot(p1, jnp.where(kv_head0, ones, v))
                o_new = jnp.where(head0, o0, o1)
                l_new = pltpu.roll(jnp.where(head0, o1, o0), HEAD_DIM, 1)
                if not first:
                    o_old, m_old, l_old = old
                    m_tot = jnp.maximum(m_old, m_new)
                    w_old = jnp.exp2(m_old - m_tot)
                    w_new = jnp.exp2(m_new - m_tot)
                    o_new = w_old * o_old + w_new * o_new
                    l_new = w_old * l_old + w_new * l_new
                    m_new = m_tot
                results.append((res, q0, o_new, m_new, l_new))
            for res, q0, o_new, m_new, l_new in results:
                if last:
                    scatter(o_ref, res, q0, o_new / l_new)
                else:
                    scatter(o_ref, res, q0, o_new)
                    scatter(m_ref, res, q0, m_new)
                    scatter(l_ref, res, q0, l_new)

        def step(g, carry, score_stage=score_stage, value_stage=value_stage):
            slot = g & 1
            value_stage(g - 1, 1 - slot)
            score_stage(g, slot)
            return carry

        score_stage(0, 0)
        lax.fori_loop(1, n_groups, step, 0)
        value_stage(n_groups - 1, (n_groups - 1) & 1)


def _attention(z, n_pairs, *, q_col, k_col, v_col):
    b, n_res, length, _ = z.shape
    assert n_res == TOKEN_RES == DILATIONS[-1] and all(n_res % d == 0 for d in DILATIONS)
    assert length % ATTN_BLOCK == 0 and length >= 2 * ATTN_BLOCK and ATTN_BLOCK // n_res >= 8
    assert (n_res * length // ATTN_BLOCK) % ATTN_UNROLL == 0

    def spec(col):
        return pl.BlockSpec((None, n_res, length, LANES), lambda bi, h: (bi, 0, 0, col + h))

    slab = n_res * length * LANES * 4
    vmem = 10 * slab * 1.15 + (8 << 20)
    return pl.pallas_call(
        _attn_kernel,
        grid=(b, n_pairs),
        in_specs=[spec(q_col), spec(k_col), spec(v_col)],
        out_specs=spec(0),
        out_shape=jax.ShapeDtypeStruct((b, n_res, length, n_pairs * LANES), F32),
        scratch_shapes=[pltpu.VMEM((n_res, length, LANES), F32),
                        pltpu.VMEM((n_res, length, LANES), F32),
                        pltpu.VMEM((2 * len(DILATIONS), ATTN_BLOCK, 2 * ATTN_BLOCK), F32),
                        pltpu.VMEM((2, ATTN_UNROLL, 2, ATTN_BLOCK, 2 * ATTN_BLOCK), BF16),
                        pltpu.VMEM((2, ATTN_UNROLL, ATTN_BLOCK, LANES), F32)],
        compiler_params=_params(("parallel", "parallel"), vmem),
        name="attention",
    )(z, z, z)


def _ssm_weights(lam_re, lam_im, log_dt, b_re, b_im, c_re, c_im, d_skip, n_chunks):
    hp = lax.Precision.HIGH
    g, p = lam_re.shape
    c = b_re.shape[-1]
    t = SSM_CHUNK
    gb = GROUPS_PER_BLOCK
    nj = g // gb
    lr, li = lam_re.astype(F32), lam_im.astype(F32)
    dt = jnp.exp(log_dt.astype(F32))[:, None]
    mag = jnp.exp(lr * dt)
    ar = mag * jnp.cos(li * dt)
    ai = mag * jnp.sin(li * dt)
    nr, ni = ar - 1.0, ai
    den = lr * lr + li * li
    cr = (nr * lr + ni * li) / den
    ci = (ni * lr - nr * li) / den
    br, bi = b_re.astype(F32), b_im.astype(F32)
    bbr = cr[..., None] * br - ci[..., None] * bi
    bbi = cr[..., None] * bi + ci[..., None] * br

    def power(k):
        kk = k.astype(F32)[:, None, None]
        pm = jnp.exp(kk * (lr * dt))
        ang = kk * (li * dt)
        return pm * jnp.cos(ang), pm * jnp.sin(ang)

    pr, pi = power(jnp.arange(t + 1))
    cre, cim = c_re.astype(F32), c_im.astype(F32)
    car = cre[None] * pr[:, :, None, :] - cim[None] * pi[:, :, None, :]
    cai = cre[None] * pi[:, :, None, :] + cim[None] * pr[:, :, None, :]
    kern = jnp.einsum('kgdq,gqc->kgdc', jnp.concatenate([car[:t], -cai[:t]], axis=-1),
                      jnp.concatenate([bbr, bbi], axis=1), precision=hp)
    kern = kern.at[0].add(d_skip.astype(F32).reshape(g, c)[:, :, None] * jnp.eye(c, dtype=F32))

    kr = kern.reshape(t, nj, gb, c, c).transpose(1, 0, 4, 2, 3).reshape(nj, t, c, gb * c)

    rev_r, rev_i = pr[:t][::-1][:, :, None, :], pi[:t][::-1][:, :, None, :]
    bbr_t, bbi_t = bbr.transpose(0, 2, 1)[None], bbi.transpose(0, 2, 1)[None]
    abr = rev_r * bbr_t - rev_i * bbi_t
    abi = rev_r * bbi_t + rev_i * bbr_t

    def block_rows(x):
        return x.reshape(t, nj, gb, c, p).transpose(1, 0, 2, 3, 4).reshape(nj, t * gb * c, p)

    def rows_form(x):
        x = block_rows(x)
        return jnp.concatenate([x, x], axis=-1).astype(BF16)

    def cols_form(x):
        x = block_rows(x).transpose(0, 2, 1)
        return jnp.concatenate([x, x], axis=1).astype(BF16)

    n_steps = max(1, (n_chunks - 1).bit_length())
    sr, si = power(t * (2 ** jnp.arange(n_steps)))
    rows = -(-n_steps // 8) * 8
    sr = jnp.pad(sr.reshape(n_steps, nj, gb * p).transpose(1, 0, 2), ((0, 0), (0, rows - n_steps), (0, 0)))
    si = jnp.pad(si.reshape(n_steps, nj, gb * p).transpose(1, 0, 2), ((0, 0), (0, rows - n_steps), (0, 0)))
    return (kr.astype(BF16), rows_form(abr), rows_form(abi),
            cols_form(car[1:]), cols_form(-cai[1:]), sr, si)


def _ssm_expand(kr_ref, abr_ref, abi_ref, cmr_ref, cmi_ref, m_sc, b_sc, c_sc):
    t, c, gb, p = SSM_CHUNK, SSM_GROUP, GROUPS_PER_BLOCK, SSM_STATE
    shift = c.bit_length() - 1
    half = gb * p
    pairs = LANES // p

    def group_of(idx):
        return (idx >> shift) & (gb - 1)

    m_sc[...] = jnp.zeros_like(m_sc)
    col_g = group_of(lax.broadcasted_iota(jnp.int32, (c, LANES), 1))
    for k in range(t):
        kk = kr_ref[k]
        dk = jnp.concatenate([jnp.where(col_g == g, kk, jnp.zeros_like(kk)) for g in range(gb)], axis=0)
        for tau in range(t - k):
            m_sc[tau * LANES:(tau + 1) * LANES, (tau + k) * LANES:(tau + k + 1) * LANES] = dk

    row_g = group_of(lax.broadcasted_iota(jnp.int32, (t * LANES, LANES), 0))
    lane_half = lax.broadcasted_iota(jnp.int32, (t * LANES, LANES), 1) >> (p.bit_length() - 1)
    for part, ab_ref in enumerate((abr_ref, abi_ref)):
        ab = ab_ref[...]
        for q in range(gb // pairs):
            b_sc[:, part * half + q * LANES:part * half + (q + 1) * LANES] = jnp.where(
                row_g == pairs * q + lane_half, ab, jnp.zeros_like(ab))

    col_g2 = group_of(lax.broadcasted_iota(jnp.int32, (LANES, t * LANES), 1))
    row_half = lax.broadcasted_iota(jnp.int32, (LANES, t * LANES), 0) >> (p.bit_length() - 1)
    for part, cm_ref in enumerate((cmr_ref, cmi_ref)):
        cm = cm_ref[...]
        for q in range(gb // pairs):
            c_sc[part * half + q * LANES:part * half + (q + 1) * LANES, :] = jnp.where(
                col_g2 == pairs * q + row_half, cm, jnp.zeros_like(cm))


def _ssm_kernel(s_ref, kr_ref, abr_ref, abi_ref, cmr_ref, cmi_ref, sr_ref, si_ref, y_ref,
                m_sc, b_sc, c_sc):
    t, n_chunks, _ = s_ref.shape
    half = GROUPS_PER_BLOCK * SSM_STATE

    @pl.when(pl.program_id(1) == 0)
    def _():
        _ssm_expand(kr_ref, abr_ref, abi_ref, cmr_ref, cmi_ref, m_sc, b_sc, c_sc)

    u = jnp.concatenate([s_ref[tau].astype(BF16) for tau in range(t)], axis=1)
    y = _dot(u, m_sc[...])
    x = _dot(u, b_sc[...])
    hr, hi = x[:, :half], x[:, half:]

    row = lax.broadcasted_iota(jnp.int32, (n_chunks, half), 0)

    def shifted(h, k):
        return jnp.where(row >= k, pltpu.roll(h, k, 0), 0.0)

    step = 0
    while (1 << step) < n_chunks:
        k = 1 << step
        pr = sr_ref[step:step + 1, :]
        pi = si_ref[step:step + 1, :]
        zr, zi = shifted(hr, k), shifted(hi, k)
        hr, hi = hr + pr * zr - pi * zi, hi + pr * zi + pi * zr
        step += 1

    h_prev = jnp.concatenate([shifted(hr, 1), shifted(hi, 1)], axis=1).astype(BF16)
    y = jax.nn.gelu(y + _dot(h_prev, c_sc[...]))
    for tau in range(t):
        y_ref[tau] = y[:, tau * LANES:(tau + 1) * LANES]


def _ssm(z, s_col, weights):
    b, n_res, n_chunks, _ = z.shape
    nj = weights[0].shape[0]
    assert n_res == SSM_CHUNK
    assert SSM_GROUP & (SSM_GROUP - 1) == 0 and SSM_STATE & (SSM_STATE - 1) == 0

    def wspec(a):
        nd = a.ndim - 1
        return pl.BlockSpec((None,) + a.shape[1:], lambda j, bi: (j,) + (0,) * nd)

    s = n_res * n_chunks
    width = SSM_CHUNK * LANES
    state = 2 * GROUPS_PER_BLOCK * SSM_STATE
    scratch = (width * width + 2 * width * state) * 2
    wbytes = 2 * sum(math.prod(a.shape[1:]) * a.dtype.itemsize for a in weights)
    vmem = (4 * s * LANES * 4 + wbytes + scratch + n_chunks * (width * 10 + state * 12)) * 1.2
    return pl.pallas_call(
        _ssm_kernel,
        grid=(nj, b),
        in_specs=[pl.BlockSpec((None, n_res, n_chunks, LANES),
                               lambda j, bi: (bi, 0, 0, s_col + j))]
        + [wspec(a) for a in weights],
        out_specs=pl.BlockSpec((None, n_res, n_chunks, LANES), lambda j, bi: (bi, 0, 0, j)),
        out_shape=jax.ShapeDtypeStruct((b, n_res, n_chunks, nj * LANES), F32),
        scratch_shapes=[pltpu.VMEM((width, width), BF16), pltpu.VMEM((width, state), BF16),
                        pltpu.VMEM((state, width), BF16)],
        compiler_params=_params(("parallel", "arbitrary"), vmem),
        name="ssm",
    )(z, *weights)


def _mix_out_kernel(ya_ref, y_ref, h_ref, ga_ref, gb_ref, wglu_ref, bglu_ref,
                    wa_ref, wb_ref, o_ref):
    n_res, tl, _ = ya_ref.shape
    sub = tl // MIX_SPLIT
    rows = n_res * sub
    perm = _residue_perm(n_res, sub, True)
    for s in range(MIX_SPLIT):
        ls = slice(s * sub, (s + 1) * sub)
        y = y_ref[:, ls, :].reshape(rows, -1)
        gate = _dot(y.astype(BF16), wglu_ref[...]) + bglu_ref[...]
        yb = y * jax.nn.sigmoid(gate)
        na = _rms(ya_ref[:, ls, :].reshape(rows, -1), ga_ref[...]).astype(BF16)
        nb = _rms(yb, gb_ref[...]).astype(BF16)
        na = _dot(perm, na).astype(BF16)
        nb = _dot(perm, nb).astype(BF16)
        tok = slice(s * rows, (s + 1) * rows)
        o_ref[tok, :] = h_ref[tok, :] + _dot(na, wa_ref[...]) + _dot(nb, wb_ref[...])


def _mix_out(ya, y, h, ga, gb, wglu, bglu, wa, wb, *, tm=512):
    m, d = h.shape
    batch, n_res, length, wa_w = ya.shape
    wb_w = y.shape[-1]
    tm = min(tm, n_res * length)
    tl = tm // n_res
    tiles = length // tl
    assert tl % (8 * MIX_SPLIT) == 0 and length % tl == 0
    vmem = (2 * tm * (wa_w + wb_w + 2 * d) * 4 + (wglu.size + wa.size + wb.size) * 2
            + 8 * tm * d * 4) * 1.2
    row = pl.BlockSpec((tm, d), lambda i: (i, 0))
    res_major = lambda w: pl.BlockSpec((None, n_res, tl, w),
                                       lambda i: (i // tiles, 0, i % tiles, 0))
    return pl.pallas_call(
        _mix_out_kernel,
        grid=(m // tm,),
        in_specs=[res_major(wa_w), res_major(wb_w), row,
                  _const_spec((1, wa_w)), _const_spec((1, wb_w)),
                  _const_spec(wglu.shape), _const_spec((1, wb_w)),
                  _const_spec(wa.shape), _const_spec(wb.shape)],
        out_specs=row,
        out_shape=jax.ShapeDtypeStruct((m, d), F32),
        compiler_params=_params(("parallel",), vmem),
        name="mix_out",
    )(ya, y, h, ga.reshape(1, -1), gb.reshape(1, -1), wglu, bglu.reshape(1, -1), wa, wb)


def _ple_kernel(h_ref, p_ref, gn_ref, wg_ref, wp_ref, fn_ref, o_ref, *, final):
    h = h_ref[...]
    gate = jax.nn.sigmoid(_dot(_rms(h, gn_ref[...]).astype(BF16), wg_ref[...]))
    proj = _dot(p_ref[...].astype(BF16), wp_ref[...])
    out = h + gate * proj
    o_ref[...] = _rms(out, fn_ref[...]) if final else out


def _ple(h, p, layer, gn, wg, wp, fn, *, final, tm=512):
    m, d = h.shape
    pd = p.shape[2]
    tm = min(tm, m)
    vmem = (2 * tm * (2 * d + pd) * 4 + (wg.size + wp.size) * 2 + 8 * tm * d * 4) * 1.2
    row = lambda w: pl.BlockSpec((tm, w), lambda i: (i, 0))
    return pl.pallas_call(
        functools.partial(_ple_kernel, final=final),
        grid=(m // tm,),
        in_specs=[row(d), pl.BlockSpec((None, tm, pd), lambda i: (layer, i, 0)),
                  _const_spec((1, d)), _const_spec(wg.shape),
                  _const_spec(wp.shape), _const_spec((1, d))],
        out_specs=row(d),
        out_shape=jax.ShapeDtypeStruct((m, d), F32),
        compiler_params=_params(("parallel",), vmem),
        name="ple",
    )(h, p, gn.reshape(1, d), wg, wp, fn.reshape(1, d))


def kernel(x, p, ffn1_norm, ffn1_w_gate, ffn1_w_up, ffn1_w_down, mix_norm, w_in, attn_out_norm, ssm_lambda_re, ssm_lambda_im, ssm_log_dt, ssm_b_re, ssm_b_im, ssm_c_re, ssm_c_im, ssm_d, ssm_w_glu, ssm_b_glu, ssm_out_norm, w_out, ffn2_norm, ffn2_w_gate, ffn2_w_up, ffn2_w_down, ple_norm, ple_w_gate, ple_w_proj, final_norm):
    b, s, d = x.shape
    depth = p.shape[0]
    attn_w = attn_out_norm.shape[1]
    ssm_w = ssm_out_norm.shape[1]
    n_pairs = attn_w // LANES
    h = x.reshape(b * s, d)
    for i in range(depth):
        h = _ffn(h, ffn1_norm[i], _ff_tiles(ffn1_w_gate, i, 1), _ff_tiles(ffn1_w_up, i, 1),
                 _ff_tiles(ffn1_w_down, i, 0))
        z = _in_proj(h, mix_norm[i], w_in[i].astype(BF16), b)
        ya = _attention(z, n_pairs, q_col=0, k_col=n_pairs, v_col=2 * n_pairs)
        weights = _ssm_weights(ssm_lambda_re[i], ssm_lambda_im[i], ssm_log_dt[i],
                               ssm_b_re[i], ssm_b_im[i], ssm_c_re[i], ssm_c_im[i],
                               ssm_d[i], s // SSM_CHUNK)
        y = _ssm(z, 3 * n_pairs, weights)
        wo = w_out[i].astype(BF16)
        h = _mix_out(ya, y, h, attn_out_norm[i], ssm_out_norm[i], ssm_w_glu[i].astype(BF16),
                     ssm_b_glu[i], wo[:attn_w], wo[attn_w:])
        h = _ffn(h, ffn2_norm[i], _ff_tiles(ffn2_w_gate, i, 1), _ff_tiles(ffn2_w_up, i, 1),
                 _ff_tiles(ffn2_w_down, i, 0))
        h = _ple(h, p.reshape(depth, b * s, -1), i, ple_norm[i], ple_w_gate[i].astype(BF16),
                 ple_w_proj[i].astype(BF16), final_norm, final=i == depth - 1)
    return h.reshape(b, s, d)
```
